```python
import math
import jax, jax.numpy as jnp
from jax import lax
import numpy as np

D_MODEL = 1024
BATCH = 2
SEQ = 8192
DEPTH = 4
DEC_BATCH = 128
DEC_SEQ = 8
PAST_LEN = 8192
PAGE_SIZE = 128

N_A_LAYERS = DEPTH // 2
N_B_LAYERS = DEPTH - N_A_LAYERS
A_HEADS = 4
A_DQK = D_MODEL // (2 * A_HEADS)
A_DV = D_MODEL // A_HEADS
A_QK_W = A_HEADS * A_DQK
A_V_W = A_HEADS * A_DV
A_PROJ = 2 * A_QK_W + 2 * A_V_W + 2 * A_HEADS
A_CHUNK = 64
GATE_CAP = 15.0
B_HEAD_DIM = 64
B_Q_HEADS = D_MODEL // B_HEAD_DIM
B_KV_HEADS = 4
B_GROUP = B_Q_HEADS // B_KV_HEADS
KV_WIDTH = B_KV_HEADS * B_HEAD_DIM
WINDOW = 128
ROT_DIM = B_HEAD_DIM // 4
ROPE_THETA = 500000.0
D_FF = 2816
EPS = 1e-6

kernel_name = "yoco_mlstm_swa_sink_macaron_step"


def rms_norm(x, g):
    xf = x.astype(jnp.float32)
    y = xf * lax.rsqrt(jnp.mean(xf * xf, axis=-1, keepdims=True) + EPS)
    return (y * g.astype(jnp.float32)).astype(x.dtype)


def swiglu(h, w_gate, w_up, w_down):
    return (jax.nn.silu(h @ w_gate) * (h @ w_up)) @ w_down


def partial_rope(x, pos):
    inv_freq = jnp.exp(-math.log(ROPE_THETA) * jnp.arange(0, ROT_DIM, 2, dtype=jnp.float32) / ROT_DIM)
    ang = pos.astype(jnp.float32)[:, None] * inv_freq[None, :]
    cos = jnp.cos(ang)[None, :, None, :]
    sin = jnp.sin(ang)[None, :, None, :]
    xr = x[..., :ROT_DIM].astype(jnp.float32)
    x1, x2 = xr[..., :ROT_DIM // 2], xr[..., ROT_DIM // 2:]
    rot = jnp.concatenate([x1 * cos - x2 * sin, x2 * cos + x1 * sin], axis=-1).astype(x.dtype)
    return jnp.concatenate([rot, x[..., ROT_DIM:]], axis=-1)


def mlstm_chunked(q, k, v, ig, lf, C0, n0, m0):
    B, S, H, _ = q.shape
    L = A_CHUNK if S % A_CHUNK == 0 else S
    NC = S // L

    def chunks(t):
        t = t.astype(jnp.float32).reshape((B, NC, L, H) + t.shape[3:])
        return jnp.swapaxes(jnp.moveaxis(t, 1, 0), 2, 3)

    tril = jnp.tril(jnp.ones((L, L), dtype=bool))

    def step(carry, xs):
        C, n, m = carry
        qc, kc, vc, ic, fc = xs
        b = jnp.cumsum(fc, axis=-1)
        dmat = jnp.where(tril, b[..., :, None] - b[..., None, :] + ic[..., None, :], -jnp.inf)
        g = b + m[..., None]
        mt = jnp.maximum(g, dmat.max(axis=-1))
        w_inter = jnp.exp(g - mt)
        s = jnp.einsum("bhtd,bhsd->bhts", qc, kc) * jnp.exp(dmat - mt[..., None])
        num = jnp.einsum("bhts,bhsv->bhtv", s, vc) + w_inter[..., None] * jnp.einsum("bhtd,bhdv->bhtv", qc, C)
        den = s.sum(axis=-1) + w_inter * jnp.einsum("bhtd,bhd->bht", qc, n)
        hc = num / jnp.maximum(jnp.abs(den), jnp.exp(-mt))[..., None]
        m_new = mt[..., -1]
        a = jnp.exp(b[..., -1:] - b + ic - m_new[..., None])
        decay = jnp.exp(b[..., -1] + m - m_new)
        C_new = decay[..., None, None] * C + jnp.einsum("bhs,bhsd,bhsv->bhdv", a, kc, vc)
        n_new = decay[..., None] * n + jnp.einsum("bhs,bhsd->bhd", a, kc)
        return (C_new, n_new, m_new), hc

    carry0 = (C0.astype(jnp.float32), n0.astype(jnp.float32), m0.astype(jnp.float32))
    (C, n, m), hs = lax.scan(step, carry0, (chunks(q), chunks(k), chunks(v), chunks(ig), chunks(lf)))
    h = jnp.swapaxes(jnp.moveaxis(hs, 0, 1), 2, 3).reshape(B, S, H, -1)
    return h, C.astype(C0.dtype), n.astype(n0.dtype), m.astype(m0.dtype)


def mlstm_mixer(h, w_in, b_gate, head_norm, w_out, C0, n0, m0):
    B, S, _ = h.shape
    proj = h @ w_in
    q, k, v, o, gates = jnp.split(proj, [A_QK_W, 2 * A_QK_W, 2 * A_QK_W + A_V_W, 2 * A_QK_W + 2 * A_V_W], axis=-1)
    gates = gates.astype(jnp.float32) + b_gate.astype(jnp.float32)
    gates = GATE_CAP * jnp.tanh(gates / GATE_CAP)
    ig = gates[..., :A_HEADS]
    lf = jax.nn.log_sigmoid(gates[..., A_HEADS:])
    q = q.reshape(B, S, A_HEADS, A_DQK)
    k = k.reshape(B, S, A_HEADS, A_DQK) * (A_DQK ** -0.5)
    v = v.reshape(B, S, A_HEADS, A_DV)
    hh, C, n, m = mlstm_chunked(q, k, v, ig, lf, C0, n0, m0)
    hh = rms_norm(hh, head_norm.reshape(A_HEADS, A_DV))
    hh = hh.reshape(B, S, A_V_W).astype(h.dtype) * jax.nn.sigmoid(o)
    return hh @ w_out, C, n, m


def sink_attention(q, k, v, q_pos, k_pos, sinks):
    s = jnp.einsum("bnqkgd,bnskd->bnkgqs", q, k).astype(jnp.float32) * (B_HEAD_DIM ** -0.5)
    diff = q_pos[:, :, None] - k_pos[:, None, :]
    mask = (diff >= 0) & (diff < WINDOW) & (k_pos[:, None, :] >= 0)
    s = jnp.where(mask[None, :, None, None, :, :], s, -jnp.inf)
    sk = sinks.astype(jnp.float32).reshape(B_KV_HEADS, B_GROUP)[None, None, :, :, None, None]
    mx = jnp.maximum(s.max(axis=-1, keepdims=True), sk)
    p = jnp.exp(s - mx)
    p = p / (p.sum(axis=-1, keepdims=True) + jnp.exp(sk - mx))
    return jnp.einsum("bnkgqs,bnskd->bnqkgd", p.astype(v.dtype), v)


def swa_banded(q, k, v, sinks):
    B, S = q.shape[:2]
    NB = S // WINDOW
    qb = q.reshape(B, NB, WINDOW, B_KV_HEADS, B_GROUP, B_HEAD_DIM)
    kb = k.reshape(B, NB, WINDOW, B_KV_HEADS, B_HEAD_DIM)
    vb = v.reshape(B, NB, WINDOW, B_KV_HEADS, B_HEAD_DIM)
    kk = jnp.concatenate([jnp.concatenate([jnp.zeros_like(kb[:, :1]), kb[:, :-1]], axis=1), kb], axis=2)
    vv = jnp.concatenate([jnp.concatenate([jnp.zeros_like(vb[:, :1]), vb[:, :-1]], axis=1), vb], axis=2)
    pos = jnp.arange(S, dtype=jnp.int32).reshape(NB, WINDOW)
    kpos = jnp.concatenate([pos - WINDOW, pos], axis=1)
    o = sink_attention(qb, kk, vv, pos, kpos, sinks)
    return o.reshape(B, S, B_Q_HEADS * B_HEAD_DIM)


def swa_direct(q, keys, vals, q_pos, k_pos, sinks):
    B, T = q.shape[:2]
    o = sink_attention(q.reshape(B, 1, T, B_KV_HEADS, B_GROUP, B_HEAD_DIM), keys[:, None], vals[:, None],
                       q_pos[None], k_pos[None], sinks)
    return o.reshape(B, T, B_Q_HEADS * B_HEAD_DIM)


def trunk(x, pos, C0, n0, m0, k_past, v_past, p):
    B, T, _ = x.shape
    new_C, new_n, new_m = [], [], []
    keys = vals = k_pos = buf_k = buf_v = None
    for l in range(DEPTH):
        if l == N_A_LAYERS:
            kv = rms_norm(x, p["kv_norm"]) @ p["w_kv"] + p["b_kv"]
            k_new = partial_rope(kv[..., :KV_WIDTH].reshape(B, T, B_KV_HEADS, B_HEAD_DIM), pos)
            v_new = kv[..., KV_WIDTH:].reshape(B, T, B_KV_HEADS, B_HEAD_DIM)
            if k_past is None:
                keys, vals = k_new, v_new
                wb = min(WINDOW, T)
            else:
                wb = k_past.shape[1]
                keys = jnp.concatenate([k_past, k_new], axis=1)
                vals = jnp.concatenate([v_past, v_new], axis=1)
                k_pos = jnp.concatenate([pos[0] - wb + jnp.arange(wb, dtype=jnp.int32), pos])
            buf_k, buf_v = keys[:, -wb:], vals[:, -wb:]
        x = x + 0.5 * swiglu(rms_norm(x, p["ffn_norm"][l, 0]), p["ffn_w_gate"][l, 0], p["ffn_w_up"][l, 0], p["ffn_w_down"][l, 0])
        h = rms_norm(x, p["mix_norm"][l])
        if l < N_A_LAYERS:
            y, C, n, m = mlstm_mixer(h, p["a_w_in"][l], p["a_b_gate"][l], p["a_head_norm"][l], p["a_w_out"][l], C0[l], n0[l], m0[l])
            new_C.append(C)
            new_n.append(n)
            new_m.append(m)
        else:
            j = l - N_A_LAYERS
            q = partial_rope((h @ p["b_w_q"][j] + p["b_b_q"][j]).reshape(B, T, B_Q_HEADS, B_HEAD_DIM), pos)
            if k_past is None:
                o = swa_banded(q, keys, vals, p["b_sinks"][j])
            else:
                o = swa_direct(q, keys, vals, pos, k_pos, p["b_sinks"][j])
            y = o @ p["b_w_o"][j] + p["b_b_o"][j]
        x = x + y
        x = x + 0.5 * swiglu(rms_norm(x, p["ffn_norm"][l, 1]), p["ffn_w_gate"][l, 1], p["ffn_w_up"][l, 1], p["ffn_w_down"][l, 1])
    return rms_norm(x, p["final_norm"]), jnp.stack(new_C), jnp.stack(new_n), jnp.stack(new_m), buf_k, buf_v


def setup_inputs(seed: int = 0) -> dict:
    key = jax.random.key(seed)
    ks = jax.random.split(key, 32)
    f32 = jnp.float32

    def nrm(k, shape, scale):
        return jax.random.normal(k, shape, f32) * scale

    WB = min(WINDOW, PAST_LEN)
    f_bias = jnp.linspace(3.0, 6.0, A_HEADS, dtype=f32)[None, :] + nrm(ks[14], (N_A_LAYERS, A_HEADS), 0.1)
    i_bias = nrm(ks[15], (N_A_LAYERS, A_HEADS), 0.1)
    return {
        "x_prompt": nrm(ks[0], (BATCH, SEQ, D_MODEL), 1.0),
        "x_sample": nrm(ks[1], (DEC_BATCH, DEC_SEQ, D_MODEL), 1.0),
        "state_mlstm_C": nrm(ks[2], (N_A_LAYERS, DEC_BATCH, A_HEADS, A_DQK, A_DV), 0.1),
        "state_mlstm_n": nrm(ks[3], (N_A_LAYERS, DEC_BATCH, A_HEADS, A_DQK), 0.1),
        "state_mlstm_m": nrm(ks[4], (N_A_LAYERS, DEC_BATCH, A_HEADS), 0.5),
        "cache_swa_k": nrm(ks[5], (DEC_BATCH, WB, B_KV_HEADS, B_HEAD_DIM), 1.0),
        "cache_swa_v": nrm(ks[6], (DEC_BATCH, WB, B_KV_HEADS, B_HEAD_DIM), 1.0),
        "ffn_norm": 1.0 + nrm(ks[7], (DEPTH, 2, D_MODEL), 0.02),
        "ffn_w_gate": nrm(ks[8], (DEPTH, 2, D_MODEL, D_FF), D_MODEL ** -0.5),
        "ffn_w_up": nrm(ks[9], (DEPTH, 2, D_MODEL, D_FF), D_MODEL ** -0.5),
        "ffn_w_down": nrm(ks[10], (DEPTH, 2, D_FF, D_MODEL), D_FF ** -0.5),
        "mix_norm": 1.0 + nrm(ks[11], (DEPTH, D_MODEL), 0.02),
        "a_w_in": nrm(ks[12], (N_A_LAYERS, D_MODEL, A_PROJ), D_MODEL ** -0.5),
        "a_b_gate": jnp.concatenate([i_bias, f_bias], axis=-1),
        "a_head_norm": 1.0 + nrm(ks[16], (N_A_LAYERS, A_V_W), 0.02),
        "a_w_out": nrm(ks[17], (N_A_LAYERS, A_V_W, D_MODEL), A_V_W ** -0.5),
        "kv_norm": 1.0 + nrm(ks[18], (D_MODEL,), 0.02),
        "w_kv": nrm(ks[19], (D_MODEL, 2 * KV_WIDTH), D_MODEL ** -0.5),
        "b_kv": nrm(ks[20], (2 * KV_WIDTH,), 0.02),
        "b_w_q": nrm(ks[21], (N_B_LAYERS, D_MODEL, B_Q_HEADS * B_HEAD_DIM), D_MODEL ** -0.5),
        "b_b_q": nrm(ks[22], (N_B_LAYERS, B_Q_HEADS * B_HEAD_DIM), 0.02),
        "b_sinks": nrm(ks[23], (N_B_LAYERS, B_Q_HEADS), 1.0),
        "b_w_o": nrm(ks[24], (N_B_LAYERS, B_Q_HEADS * B_HEAD_DIM, D_MODEL), (B_Q_HEADS * B_HEAD_DIM) ** -0.5),
        "b_b_o": nrm(ks[25], (N_B_LAYERS, D_MODEL), 0.02),
        "final_norm": 1.0 + nrm(ks[26], (D_MODEL,), 0.02),
    }


def reference(x_prompt, x_sample, state_mlstm_C, state_mlstm_n, state_mlstm_m, cache_swa_k, cache_swa_v,
              ffn_norm, ffn_w_gate, ffn_w_up, ffn_w_down, mix_norm, a_w_in, a_b_gate, a_head_norm, a_w_out,
              kv_norm, w_kv, b_kv, b_w_q, b_b_q, b_sinks, b_w_o, b_b_o, final_norm):
    p = {"ffn_norm": ffn_norm, "ffn_w_gate": ffn_w_gate, "ffn_w_up": ffn_w_up, "ffn_w_down": ffn_w_down,
         "mix_norm": mix_norm, "a_w_in": a_w_in, "a_b_gate": a_b_gate, "a_head_norm": a_head_norm,
         "a_w_out": a_w_out, "kv_norm": kv_norm, "w_kv": w_kv, "b_kv": b_kv, "b_w_q": b_w_q,
         "b_b_q": b_b_q, "b_sinks": b_sinks, "b_w_o": b_w_o, "b_b_o": b_b_o, "final_norm": final_norm}
    Bp, Sp, _ = x_prompt.shape
    Ts = x_sample.shape[1]
    zC = jnp.zeros((N_A_LAYERS, Bp, A_HEADS, A_DQK, A_DV), jnp.float32)
    zn = jnp.zeros((N_A_LAYERS, Bp, A_HEADS, A_DQK), jnp.float32)
    zm = jnp.zeros((N_A_LAYERS, Bp, A_HEADS), jnp.float32)
    y_prompt, p_C, p_n, p_m, p_k, p_v = trunk(x_prompt, jnp.arange(Sp, dtype=jnp.int32), zC, zn, zm, None, None, p)
    s_pos = PAST_LEN + jnp.arange(Ts, dtype=jnp.int32)
    y_sample, s_C, s_n, s_m, s_k, s_v = trunk(x_sample, s_pos, state_mlstm_C, state_mlstm_n, state_mlstm_m,
                                              cache_swa_k, cache_swa_v, p)
    return (y_prompt, y_sample, p_C, p_n, p_m, p_k, p_v, s_C, s_n, s_m, s_k, s_v)
```

```python
import functools
import math

import jax
import jax.numpy as jnp
from jax import lax
from jax.experimental import pallas as pl
from jax.experimental.pallas import tpu as pltpu

F32 = jnp.float32
BF16 = jnp.bfloat16

EPS = 1e-6
GATE_CAP = 15.0
A_HEADS = 4
A_CHUNK = 64
B_HEAD_DIM = 64
B_KV_HEADS = 4
WINDOW = 128
ROT_DIM = B_HEAD_DIM // 4
ROPE_THETA = 500000.0
PAST_LEN = 8192

LANES = 128
V7X_VMEM_LIMIT_BYTES = 56 * 1024 * 1024
TOKEN_TILE = 512
FF_TILE = 256


def _rms(x, g):
    return (x * lax.rsqrt(jnp.mean(x * x, axis=-1, keepdims=True) + EPS)) * g


def _dot(a, b):
    return jnp.dot(a, b, preferred_element_type=F32)


def _dot_nt(a, b):
    return lax.dot_general(a, b, (((1,), (1,)), ((), ())), preferred_element_type=F32)


def _dot_tn(a, b):
    return lax.dot_general(a, b, (((0,), (0,)), ((), ())), preferred_element_type=F32)


def _dot_f32(a, b):
    return jnp.dot(a, b, preferred_element_type=F32, precision=lax.Precision.HIGHEST)


def _resident(shape):
    zeros = (0,) * len(shape)
    return pl.BlockSpec(shape, lambda *_: zeros, pipeline_mode=pl.Buffered(1))


def _rows(tm, width):
    return pl.BlockSpec((tm, width), lambda i: (i, 0))


def _params(*sem):
    return pltpu.CompilerParams(dimension_semantics=sem, vmem_limit_bytes=V7X_VMEM_LIMIT_BYTES)


def _token_tile(n):
    tm = TOKEN_TILE
    while n % tm:
        tm //= 2
    return tm


def _ffn_kernel(*refs, premix, final, tf):
    refs = list(refs)
    x_ref = refs.pop(0)
    if premix:
        hg_ref, wmix_ref, bmix_ref = refs.pop(0), refs.pop(0), refs.pop(0)
    gn_ref, wg_ref, wu_ref, wd_ref = refs.pop(0), refs.pop(0), refs.pop(0), refs.pop(0)
    if final:
        fg_ref = refs.pop(0)
    o_ref, a_ref = refs

    x = x_ref[...]
    if premix:
        x = x + (_dot(hg_ref[...].astype(BF16), wmix_ref[...]) + bmix_ref[...])
    h = _rms(x, gn_ref[...]).astype(BF16)
    for j in range(wg_ref.shape[1] // tf):
        sl = slice(j * tf, (j + 1) * tf)
        g = _dot(h, wg_ref[:, sl])
        u = _dot(h, wu_ref[:, sl])
        a_ref[:, sl] = ((g * jax.nn.sigmoid(g)) * u).astype(BF16)
    out = x + 0.5 * _dot(a_ref[...], wd_ref[...])
    if final:
        out = _rms(out, fg_ref[...])
    o_ref[...] = out


def _ffn(x, gn, wg, wu, wd, mix=None, final_gain=None, name="ffn"):
    n, d = x.shape
    dff = wg.shape[1]
    tm = _token_tile(n)
    tf = FF_TILE if dff % FF_TILE == 0 else dff
    args, specs = [x], [_rows(tm, d)]
    if mix is not None:
        hg, wmix, bmix = mix
        args += [hg, wmix, bmix]
        specs += [_rows(tm, hg.shape[1]), _resident(wmix.shape), _resident(bmix.shape)]
    args += [gn, wg, wu, wd]
    specs += [_resident(gn.shape), _resident(wg.shape), _resident(wu.shape), _resident(wd.shape)]
    if final_gain is not None:
        args.append(final_gain)
        specs.append(_resident(final_gain.shape))
    kern = functools.partial(_ffn_kernel, premix=mix is not None, final=final_gain is not None, tf=tf)
    return pl.pallas_call(
        kern,
        grid=(n // tm,),
        in_specs=specs,
        out_specs=_rows(tm, d),
        out_shape=jax.ShapeDtypeStruct((n, d), F32),
        scratch_shapes=[pltpu.VMEM((tm, dff), BF16)],
        compiler_params=_params("parallel"),
        name=name,
    )(*args)


def _inproj_kernel(x_ref, gn_ref, wqk_ref, wv_ref, wo_ref, wgt_ref, bg_ref,
                   q_ref, k_ref, v_ref, o_ref, g_ref, *, k_scale):
    h = _rms(x_ref[...], gn_ref[...]).astype(BF16)
    dqk = q_ref.shape[1]
    qk = _dot(h, wqk_ref[...])
    q_ref[...] = qk[:, :dqk].astype(q_ref.dtype)
    k_ref[...] = (qk[:, dqk:] * k_scale).astype(k_ref.dtype)
    v_ref[...] = _dot(h, wv_ref[...]).astype(v_ref.dtype)
    o_ref[...] = _dot(h, wo_ref[...])
    gates = _dot(h, wgt_ref[...]) + bg_ref[...]
    gates = GATE_CAP * jnp.tanh(gates / GATE_CAP)
    logsig = jnp.minimum(gates, 0.0) - jnp.log1p(jnp.exp(-jnp.abs(gates)))
    lane = lax.broadcasted_iota(jnp.int32, gates.shape, 1)
    g_ref[...] = jnp.where(lane < A_HEADS, gates, logsig)


def _inproj(x, gn, wqk, wv, wo, wgt, bg, act_dtype, name):
    n, d = x.shape
    tm = _token_tile(n)
    dqk = wqk.shape[1] // 2
    dv = wv.shape[1]
    kern = functools.partial(_inproj_kernel, k_scale=(dqk // A_HEADS) ** -0.5)
    return pl.pallas_call(
        kern,
        grid=(n // tm,),
        in_specs=[_rows(tm, d)] + [_resident(a.shape) for a in (gn, wqk, wv, wo, wgt, bg)],
        out_specs=[_rows(tm, dqk), _rows(tm, dqk), _rows(tm, dv), _rows(tm, dv), _rows(tm, LANES)],
        out_shape=[jax.ShapeDtypeStruct((n, dqk), act_dtype), jax.ShapeDtypeStruct((n, dqk), act_dtype),
                   jax.ShapeDtypeStruct((n, dv), act_dtype), jax.ShapeDtypeStruct((n, dv), F32),
                   jax.ShapeDtypeStruct((n, LANES), F32)],
        compiler_params=_params("parallel"),
        name=name,
    )(x, gn, wqk, wv, wo, wgt, bg)


def _rope(x, cos, sin):
    width = x.shape[1]
    reps = width // LANES
    cosw = jnp.concatenate([cos] * reps, axis=1)
    sinw = jnp.concatenate([sin] * reps, axis=1)
    half = ROT_DIM // 2
    lane = lax.broadcasted_iota(jnp.int32, x.shape, 1)
    first_half = (lane & (B_HEAD_DIM - 1)) < half
    partner = jnp.where(first_half, pltpu.roll(x, width - half, 1), pltpu.roll(x, half, 1))
    return x * cosw + partner * sinw


def _kv_kernel(x_ref, gn_ref, w_ref, b_ref, cos_ref, sin_ref, k_ref, v_ref):
    h = _rms(x_ref[...], gn_ref[...]).astype(BF16)
    kv = _dot(h, w_ref[...]) + b_ref[...]
    kw = k_ref.shape[1]
    k_ref[...] = _rope(kv[:, :kw], cos_ref[...], sin_ref[...])
    v_ref[...] = kv[:, kw:]


def _pos_spec(tm, table):
    nblk = table.shape[0] // tm
    return pl.BlockSpec((tm, LANES), lambda i: (i % nblk, 0))


def _kv(x, gn, w, b, cos, sin, name):
    n, d = x.shape
    tm = _token_tile(n)
    kw = w.shape[1] // 2
    return pl.pallas_call(
        _kv_kernel,
        grid=(n // tm,),
        in_specs=[_rows(tm, d), _resident(gn.shape), _resident(w.shape), _resident(b.shape),
                  _pos_spec(tm, cos), _pos_spec(tm, sin)],
        out_specs=[_rows(tm, kw), _rows(tm, kw)],
        out_shape=[jax.ShapeDtypeStruct((n, kw), F32), jax.ShapeDtypeStruct((n, kw), F32)],
        compiler_params=_params("parallel"),
        name=name,
    )(x, gn, w, b, cos, sin)


def _qproj_kernel(x_ref, gn_ref, w_ref, b_ref, cos_ref, sin_ref, q_ref, *, q_scale):
    h = _rms(x_ref[...], gn_ref[...]).astype(BF16)
    q = _dot(h, w_ref[...]) + b_ref[...]
    q_ref[...] = (_rope(q, cos_ref[...], sin_ref[...]) * q_scale).astype(q_ref.dtype)


def _qproj(x, gn, w, b, cos, sin, act_dtype, name):
    n, d = x.shape
    tm = _token_tile(n)
    kern = functools.partial(_qproj_kernel, q_scale=B_HEAD_DIM ** -0.5)
    return pl.pallas_call(
        kern,
        grid=(n // tm,),
        in_specs=[_rows(tm, d), _resident(gn.shape), _resident(w.shape), _resident(b.shape),
                  _pos_spec(tm, cos), _pos_spec(tm, sin)],
        out_specs=_rows(tm, w.shape[1]),
        out_shape=jax.ShapeDtypeStruct((n, w.shape[1]), act_dtype),
        compiler_params=_params("parallel"),
        name=name,
    )(x, gn, w, b, cos, sin)


def _mlstm_kernel(q_ref, k_ref, v_ref, o_ref, gtm_ref, gt_ref, hn_ref, c0_ref, n0_ref, m0_ref,
                  out_ref, c_ref, n_ref, m_ref, *, bblk, chunk):
    L = chunk
    dqk = q_ref.shape[2] // A_HEADS
    dv = v_ref.shape[2] // A_HEADS

    @pl.when(pl.program_id(1) == 0)
    def _():
        c_ref[...] = c0_ref[...]
        n_ref[...] = n0_ref[...]
        m_ref[...] = m0_ref[...]

    row = lax.broadcasted_iota(jnp.int32, (L, L), 0)
    col = lax.broadcasted_iota(jnp.int32, (L, L), 1)
    tril = col <= row
    lower_ones = tril.astype(F32)
    upper_ones = (row <= col).astype(F32)

    for b in range(bblk):
        gt = gt_ref[b, 0]
        gtm = gtm_ref[b]
        csum_rows = _dot_f32(gt, upper_ones)
        csum_cols = _dot_f32(lower_ones, gtm)
        for h in range(A_HEADS):
            i_row = gt[h:h + 1, :]
            b_row = csum_rows[A_HEADS + h:A_HEADS + h + 1, :]
            i_col = gtm[:, h:h + 1]
            b_col = csum_cols[:, A_HEADS + h:A_HEADS + h + 1]
            b_last = b_col[L - 1:L, :]
            m_prev = m_ref[b, h:h + 1, :]
            n_prev = n_ref[b, h:h + 1, :]
            c_prev = c_ref[b, h]

            qh = q_ref[b, :, h * dqk:(h + 1) * dqk].astype(BF16)
            kh = k_ref[b, :, h * dqk:(h + 1) * dqk].astype(BF16)
            vh = v_ref[b, :, h * dv:(h + 1) * dv].astype(BF16)

            dmat = jnp.where(tril, (b_col - b_row) + i_row, -jnp.inf)
            g = b_col + m_prev
            mt = jnp.maximum(g, jnp.max(dmat, axis=-1, keepdims=True))
            w_inter = jnp.exp(g - mt)
            s = _dot_nt(qh, kh) * jnp.exp(dmat - mt)
            num = _dot(s.astype(BF16), vh) + w_inter * _dot(qh, c_prev.astype(BF16))
            qn = jnp.sum(qh.astype(F32) * n_prev, axis=-1, keepdims=True)
            den = jnp.sum(s, axis=-1, keepdims=True) + w_inter * qn
            hc = num / jnp.maximum(jnp.abs(den), jnp.exp(-mt))

            m_new = mt[L - 1:L, :]
            a_col = jnp.exp(((b_last - b_col) + i_col) - m_new)
            decay = jnp.exp((b_last + m_prev) - m_new)
            ka = kh.astype(F32) * a_col
            c_ref[b, h] = decay * c_prev + _dot_tn(ka.astype(BF16), vh)
            n_ref[b, h:h + 1, :] = decay * n_prev + jnp.sum(ka, axis=0, keepdims=True)
            m_ref[b, h:h + 1, :] = m_new

            hn = _rms(hc, hn_ref[:, h * dv:(h + 1) * dv])
            gate = jax.nn.sigmoid(o_ref[b, :, h * dv:(h + 1) * dv])
            out_ref[b, :, h * dv:(h + 1) * dv] = (hn * gate).astype(out_ref.dtype)


def _mlstm(q, k, v, o, gates, head_norm, c0, n0, m0, bblk, out_dtype, name):
    B, S, qw = q.shape
    vw = v.shape[2]
    L = A_CHUNK if S % A_CHUNK == 0 else S
    nc = S // L
    ng = 2 * A_HEADS
    gt = jnp.swapaxes(gates[:, :, :ng].reshape(B, nc, L, ng), 2, 3)
    kern = functools.partial(_mlstm_kernel, bblk=bblk, chunk=L)
    seq = lambda w: pl.BlockSpec((bblk, L, w), lambda i, c: (i, c, 0))
    state = lambda a: pl.BlockSpec((bblk,) + a.shape[1:], lambda i, c: (i,) + (0,) * (a.ndim - 1))
    return pl.pallas_call(
        kern,
        grid=(B // bblk, nc),
        in_specs=[seq(qw), seq(qw), seq(vw), seq(vw), seq(LANES),
                  pl.BlockSpec((bblk, 1, ng, L), lambda i, c: (i, c, 0, 0)),
                  pl.BlockSpec(head_norm.shape, lambda i, c: (0, 0)),
                  state(c0), state(n0), state(m0)],
        out_specs=[seq(vw), state(c0), state(n0), state(m0)],
        out_shape=[jax.ShapeDtypeStruct((B, S, vw), out_dtype),
                   jax.ShapeDtypeStruct(c0.shape, F32), jax.ShapeDtypeStruct(n0.shape, F32),
                   jax.ShapeDtypeStruct(m0.shape, F32)],
        compiler_params=_params("parallel", "arbitrary"),
        name=name,
    )(q, k, v, o, gates, gt, head_norm, c0, n0, m0)


def _softmax_pv(scores, masks, vals, sink_col):
    scores = [jnp.where(m, s, -jnp.inf) for s, m in zip(scores, masks)]
    mx = sink_col
    for s in scores:
        mx = jnp.maximum(mx, jnp.max(s, axis=-1, keepdims=True))
    denom = jnp.exp(sink_col - mx)
    acc = None
    for s, v in zip(scores, vals):
        p = jnp.exp(s - mx)
        denom = denom + jnp.sum(p, axis=-1, keepdims=True)
        pv = _dot(p.astype(BF16), v)
        acc = pv if acc is None else acc + pv
    return acc / denom


def _sink_column(sinks_ref, kv_head, group, rows):
    ridx = lax.broadcasted_iota(jnp.int32, (group * rows, 1), 0)
    col = jnp.full((group * rows, 1), sinks_ref[0, kv_head * group], F32)
    for g in range(1, group):
        col = jnp.where(ridx >= g * rows, sinks_ref[0, kv_head * group + g], col)
    return col


def _swa_prompt_kernel(sinks_ref, q_ref, kp_ref, kc_ref, vp_ref, vc_ref, o_ref, *, group):
    W = q_ref.shape[1]
    hd = B_HEAD_DIM
    has_prev = pl.program_id(1) > 0
    t = lax.broadcasted_iota(jnp.int32, (group * W, W), 0) & (W - 1)
    s = lax.broadcasted_iota(jnp.int32, (group * W, W), 1)
    mask_prev = jnp.logical_and(s > t, has_prev)
    mask_cur = s <= t
    for kh in range(B_KV_HEADS):
        ksl = slice(kh * hd, (kh + 1) * hd)
        q = jnp.concatenate(
            [q_ref[0, :, (kh * group + g) * hd:(kh * group + g + 1) * hd] for g in range(group)], axis=0)
        q = q.astype(BF16)
        sc_prev = _dot_nt(q, kp_ref[0, :, ksl].astype(BF16))
        sc_cur = _dot_nt(q, kc_ref[0, :, ksl].astype(BF16))
        out = _softmax_pv([sc_prev, sc_cur], [mask_prev, mask_cur],
                          [vp_ref[0, :, ksl].astype(BF16), vc_ref[0, :, ksl].astype(BF16)],
                          _sink_column(sinks_ref, kh, group, W))
        for g in range(group):
            lo = (kh * group + g) * hd
            o_ref[0, :, lo:lo + hd] = out[g * W:(g + 1) * W, :].astype(o_ref.dtype)


def _swa_prompt(q, k, v, sinks, out_dtype, name):
    B, S, qw = q.shape
    kw = k.shape[2]
    nb = S // WINDOW
    group = qw // kw
    cur = lambda w: pl.BlockSpec((1, WINDOW, w), lambda b, j: (b, j, 0))
    prev = lambda w: pl.BlockSpec((1, WINDOW, w), lambda b, j: (b, jnp.maximum(j - 1, 0), 0))
    kern = functools.partial(_swa_prompt_kernel, group=group)
    return pl.pallas_call(
        kern,
        grid=(B, nb),
        in_specs=[pl.BlockSpec(memory_space=pltpu.SMEM), cur(qw), prev(kw), cur(kw), prev(kw), cur(kw)],
        out_specs=cur(qw),
        out_shape=jax.ShapeDtypeStruct((B, S, qw), out_dtype),
        compiler_params=_params("parallel", "parallel"),
        name=name,
    )(sinks, q, k, k, v, v)


def _swa_sample_kernel(sinks_ref, q_ref, kc_ref, kn_ref, vc_ref, vn_ref, o_ref, *, group, bblk):
    T = q_ref.shape[1]
    wb = kc_ref.shape[1]
    hd = B_HEAD_DIM
    t = lax.broadcasted_iota(jnp.int32, (group * T, wb), 0) % T
    i = lax.broadcasted_iota(jnp.int32, (group * T, wb), 1)
    mask_buf = jnp.logical_and((t + wb) - i < WINDOW, (PAST_LEN - wb) + i >= 0)
    tn = lax.broadcasted_iota(jnp.int32, (group * T, T), 0) % T
    u = lax.broadcasted_iota(jnp.int32, (group * T, T), 1)
    mask_new = u <= tn
    for b in range(bblk):
        for kh in range(B_KV_HEADS):
            ksl = slice(kh * hd, (kh + 1) * hd)
            q = jnp.concatenate(
                [q_ref[b, :, (kh * group + g) * hd:(kh * group + g + 1) * hd] for g in range(group)], axis=0)
            q = q.astype(BF16)
            sc_buf = _dot_nt(q, kc_ref[b, :, ksl].astype(BF16))
            sc_new = _dot_nt(q, kn_ref[b, :, ksl].astype(BF16))
            out = _softmax_pv([sc_buf, sc_new], [mask_buf, mask_new],
                              [vc_ref[b, :, ksl].astype(BF16), vn_ref[b, :, ksl].astype(BF16)],
                              _sink_column(sinks_ref, kh, group, T))
            for g in range(group):
                lo = (kh * group + g) * hd
                o_ref[b, :, lo:lo + hd] = out[g * T:(g + 1) * T, :].astype(o_ref.dtype)


def _swa_sample(q, k_buf, k_new, v_buf, v_new, sinks, bblk, name):
    B, T, qw = q.shape
    wb, kw = k_buf.shape[1:]
    group = qw // kw
    blk = lambda r, w: pl.BlockSpec((bblk, r, w), lambda i: (i, 0, 0))
    kern = functools.partial(_swa_sample_kernel, group=group, bblk=bblk)
    return pl.pallas_call(
        kern,
        grid=(B // bblk,),
        in_specs=[pl.BlockSpec(memory_space=pltpu.SMEM), blk(T, qw), blk(wb, kw), blk(T, kw),
                  blk(wb, kw), blk(T, kw)],
        out_specs=blk(T, qw),
        out_shape=jax.ShapeDtypeStruct((B, T, qw), F32),
        compiler_params=_params("parallel"),
        name=name,
    )(sinks, q, k_buf, k_new, v_buf, v_new)


def _rope_tables(pos):
    half = ROT_DIM // 2
    inv_freq = jnp.exp(-math.log(ROPE_THETA) * jnp.arange(0, ROT_DIM, 2, dtype=F32) / ROT_DIM)
    ang = pos.astype(F32)[:, None] * inv_freq[None, :]
    d = jnp.arange(LANES, dtype=jnp.int32) % B_HEAD_DIM
    ang_l = jnp.take(ang, d % half, axis=1)
    rot = (d < ROT_DIM)[None, :]
    sign = jnp.where(d < half, -1.0, 1.0)[None, :]
    cos = jnp.where(rot, jnp.cos(ang_l), 1.0)
    sin = jnp.where(rot, jnp.sin(ang_l) * sign, 0.0)
    return cos.astype(F32), sin.astype(F32)


def _largest_divisor(n, cap):
    d = min(n, cap)
    while n % d:
        d -= 1
    return d


def _trunk(x, pos_tables, state, cache, p, tag, act_dtype):
    B, T, D = x.shape
    n_a = p["a_wqk"].shape[0]
    depth = p["ffn_norm"].shape[0]
    cos, sin = pos_tables
    xs = x.reshape(B * T, D)
    new_c, new_n, new_m = [], [], []
    k_new = v_new = None
    mix = None
    for l in range(depth):
        if l == n_a:
            if mix is not None:
                raise AssertionError("pending mixer output")
            k_new, v_new = _kv(xs, p["kv_norm"], p["w_kv"], p["b_kv"], cos, sin, name=f"kv_{tag}")
        xs = _ffn(xs, p["ffn_norm"][l, 0], p["ffn_wg"][l, 0], p["ffn_wu"][l, 0], p["ffn_wd"][l, 0],
                  name=f"ffn_{tag}_{l}a")
        if l < n_a:
            q, k, v, o, gates = _inproj(xs, p["mix_norm"][l], p["a_wqk"][l], p["a_wv"][l], p["a_wo"][l],
                                        p["a_wgt"][l], p["a_bg"][l], act_dtype, name=f"inproj_{tag}_{l}")
            c0, n0, m0 = state
            seq = lambda a: a.reshape(B, T, a.shape[1])
            bblk = _largest_divisor(B, 2 if T > A_CHUNK else 4)
            hg, c, n, m = _mlstm(seq(q), seq(k), seq(v), seq(o), seq(gates), p["a_head_norm"][l],
                                 c0[l], n0[l], m0[l][..., None], bblk, act_dtype, name=f"mlstm_{tag}_{l}")
            new_c.append(c)
            new_n.append(n)
            new_m.append(m[..., 0])
            mix = (hg.reshape(B * T, -1), p["a_w_out"][l], p["zero_bias"])
        else:
            j = l - n_a
            q = _qproj(xs, p["mix_norm"][l], p["b_w_q"][j], p["b_b_q"][j], cos, sin, act_dtype,
                       name=f"qproj_{tag}_{l}")
            q = q.reshape(B, T, -1)
            kn, vn = k_new.reshape(B, T, -1), v_new.reshape(B, T, -1)
            if cache is None:
                o = _swa_prompt(q, kn, vn, p["b_sinks"][j], act_dtype, name=f"swa_{tag}_{l}")
            else:
                kc, vc = cache
                o = _swa_sample(q, kc.reshape(B, kc.shape[1], -1), kn, vc.reshape(B, vc.shape[1], -1), vn,
                                p["b_sinks"][j], _largest_divisor(B, 8), name=f"swa_{tag}_{l}")
            mix = (o.reshape(B * T, -1), p["b_w_o"][j], p["b_b_o"][j])
        last = l == depth - 1
        xs = _ffn(xs, p["ffn_norm"][l, 1], p["ffn_wg"][l, 1], p["ffn_wu"][l, 1], p["ffn_wd"][l, 1],
                  mix=mix, final_gain=p["final_norm"] if last else None, name=f"ffn_{tag}_{l}b")
        mix = None
    y = xs.reshape(B, T, D)
    kvh = (B, T, B_KV_HEADS, B_HEAD_DIM)
    k_new, v_new = k_new.reshape(kvh), v_new.reshape(kvh)
    if cache is None:
        wb = min(WINDOW, T)
        buf_k, buf_v = k_new[:, -wb:], v_new[:, -wb:]
    else:
        wb = cache[0].shape[1]
        buf_k = jnp.concatenate([cache[0], k_new], axis=1)[:, -wb:]
        buf_v = jnp.concatenate([cache[1], v_new], axis=1)[:, -wb:]
    return y, jnp.stack(new_c), jnp.stack(new_n), jnp.stack(new_m), buf_k, buf_v


def kernel(x_prompt, x_sample, state_mlstm_C, state_mlstm_n, state_mlstm_m, cache_swa_k, cache_swa_v,
           ffn_norm, ffn_w_gate, ffn_w_up, ffn_w_down, mix_norm, a_w_in, a_b_gate, a_head_norm, a_w_out,
           kv_norm, w_kv, b_kv, b_w_q, b_b_q, b_sinks, b_w_o, b_b_o, final_norm):
    n_a, d_model, a_proj = a_w_in.shape
    ng = 2 * A_HEADS
    qk_w = d_model // 2
    v_w = d_model
    if a_proj != 2 * qk_w + 2 * v_w + ng:
        raise ValueError("unexpected mLSTM projection width")
    row = lambda a: a[..., None, :].astype(F32)
    p = {
        "ffn_norm": row(ffn_norm), "mix_norm": row(mix_norm), "kv_norm": row(kv_norm),
        "final_norm": row(final_norm), "a_head_norm": row(a_head_norm),
        "ffn_wg": ffn_w_gate.astype(BF16), "ffn_wu": ffn_w_up.astype(BF16), "ffn_wd": ffn_w_down.astype(BF16),
        "a_wqk": a_w_in[:, :, :2 * qk_w].astype(BF16),
        "a_wv": a_w_in[:, :, 2 * qk_w:2 * qk_w + v_w].astype(BF16),
        "a_wo": a_w_in[:, :, 2 * qk_w + v_w:2 * qk_w + 2 * v_w].astype(BF16),
        "a_wgt": jnp.pad(a_w_in[:, :, 2 * qk_w + 2 * v_w:], ((0, 0), (0, 0), (0, LANES - ng))).astype(BF16),
        "a_bg": jnp.pad(a_b_gate, ((0, 0), (0, LANES - ng)))[:, None, :].astype(F32),
        "a_w_out": a_w_out.astype(BF16),
        "zero_bias": jnp.zeros((1, d_model), F32),
        "w_kv": w_kv.astype(BF16), "b_kv": row(b_kv),
        "b_w_q": b_w_q.astype(BF16), "b_b_q": row(b_b_q), "b_sinks": row(b_sinks),
        "b_w_o": b_w_o.astype(BF16), "b_b_o": row(b_b_o),
    }
    Bp, Sp, _ = x_prompt.shape
    Bs, Ts, _ = x_sample.shape
    zc = jnp.zeros((n_a, Bp) + state_mlstm_C.shape[2:], F32)
    zn = jnp.zeros((n_a, Bp) + state_mlstm_n.shape[2:], F32)
    zm = jnp.zeros((n_a, Bp) + state_mlstm_m.shape[2:], F32)

    p_rows = math.lcm(Sp, _token_tile(Bp * Sp))
    p_tables = _rope_tables(jnp.arange(p_rows, dtype=jnp.int32) % Sp)
    s_rows = math.lcm(Ts, _token_tile(Bs * Ts))
    s_tables = _rope_tables(PAST_LEN + jnp.arange(s_rows, dtype=jnp.int32) % Ts)

    y_p, p_c, p_n, p_m, p_k, p_v = _trunk(x_prompt, p_tables, (zc, zn, zm), None, p, "p", BF16)
    y_s, s_c, s_n, s_m, s_k, s_v = _trunk(x_sample, s_tables, (state_mlstm_C, state_mlstm_n, state_mlstm_m),
                                          (cache_swa_k, cache_swa_v), p, "s", F32)
    return (y_p, y_s, p_c, p_n, p_m, p_k, p_v, s_c, s_n, s_m, s_k, s_v)
```

```python
import functools
import math

import jax
import jax.numpy as jnp
from jax import lax
from jax.experimental import pallas as pl
from jax.experimental.pallas import tpu as pltpu

F32 = jnp.float32
BF16 = jnp.bfloat16

EPS = 1e-6
GATE_CAP = 15.0
A_HEADS = 4
A_CHUNK = 64
B_HEAD_DIM = 64
B_KV_HEADS = 4
WINDOW = 128
ROT_DIM = B_HEAD_DIM // 4
ROPE_THETA = 500000.0
PAST_LEN = 8192

LANES = 128
V7X_VMEM_LIMIT_BYTES = 56 * 1024 * 1024
TOKEN_TILE = 512
FF_TILE = 256


def _rms(x, g):
    return (x * lax.rsqrt(jnp.mean(x * x, axis=-1, keepdims=True) + EPS)) * g


def _dot(a, b):
    return jnp.dot(a, b, preferred_element_type=F32)


def _dot_nt(a, b):
    return lax.dot_general(a, b, (((1,), (1,)), ((), ())), preferred_element_type=F32)


def _dot_tn(a, b):
    return lax.dot_general(a, b, (((0,), (0,)), ((), ())), preferred_element_type=F32)


def _dot_f32(a, b):
    return jnp.dot(a, b, preferred_element_type=F32, precision=lax.Precision.HIGHEST)


def _at(arr, *idx):
    return (arr, idx)


def _split(param):
    return param if isinstance(param, tuple) else (param, ())


def _shape(param):
    arr, idx = _split(param)
    return arr.shape[len(idx):]


def _resident(param):
    arr, idx = _split(param)
    block = (None,) * len(idx) + arr.shape[len(idx):]
    index = idx + (0,) * (arr.ndim - len(idx))
    return arr, pl.BlockSpec(block, lambda *_: index, pipeline_mode=pl.Buffered(1))


def _operands(tiled, residents):
    pairs = list(tiled) + [_resident(r) for r in residents]
    return [a for a, _ in pairs], [sp for _, sp in pairs]


def _rows(tm, width):
    return pl.BlockSpec((tm, width), lambda i: (i, 0))


def _params(*sem):
    return pltpu.CompilerParams(dimension_semantics=sem, vmem_limit_bytes=V7X_VMEM_LIMIT_BYTES)


def _token_tile(n):
    tm = TOKEN_TILE
    while n % tm:
        tm //= 2
    return tm


def _ffn_kernel(*refs, premix, final, tf):
    refs = list(refs)
    x_ref = refs.pop(0)
    if premix:
        hg_ref, wmix_ref, bmix_ref = refs.pop(0), refs.pop(0), refs.pop(0)
    gn_ref, wg_ref, wu_ref, wd_ref = refs.pop(0), refs.pop(0), refs.pop(0), refs.pop(0)
    if final:
        fg_ref = refs.pop(0)
    o_ref, a_ref = refs

    x = x_ref[...]
    if premix:
        x = x + (_dot(hg_ref[...].astype(BF16), wmix_ref[...]) + bmix_ref[...])
    h = _rms(x, gn_ref[...]).astype(BF16)
    for j in range(wg_ref.shape[1] // tf):
        sl = slice(j * tf, (j + 1) * tf)
        g = _dot(h, wg_ref[:, sl])
        u = _dot(h, wu_ref[:, sl])
        a_ref[:, sl] = ((g * jax.nn.sigmoid(g)) * u).astype(BF16)
    out = x + 0.5 * _dot(a_ref[...], wd_ref[...])
    if final:
        out = _rms(out, fg_ref[...])
    o_ref[...] = out


def _ffn(x, gn, wg, wu, wd, mix=None, final_gain=None, name="ffn"):
    n, d = x.shape
    dff = _shape(wg)[1]
    tm = _token_tile(n)
    tf = FF_TILE if dff % FF_TILE == 0 else dff
    tiled, residents = [(x, _rows(tm, d))], []
    if mix is not None:
        hg, wmix, bmix = mix
        tiled.append((hg, _rows(tm, hg.shape[1])))
        residents += [wmix, bmix]
    residents += [gn, wg, wu, wd]
    if final_gain is not None:
        residents.append(final_gain)
    args, specs = _operands(tiled, residents)
    kern = functools.partial(_ffn_kernel, premix=mix is not None, final=final_gain is not None, tf=tf)
    return pl.pallas_call(
        kern,
        grid=(n // tm,),
        in_specs=specs,
        out_specs=_rows(tm, d),
        out_shape=jax.ShapeDtypeStruct((n, d), F32),
        scratch_shapes=[pltpu.VMEM((tm, dff), BF16)],
        compiler_params=_params("parallel"),
        name=name,
    )(*args)


def _inproj_kernel(x_ref, gn_ref, wqk_ref, wv_ref, wo_ref, wgt_ref, bg_ref,
                   q_ref, k_ref, v_ref, o_ref, g_ref, *, k_scale):
    h = _rms(x_ref[...], gn_ref[...]).astype(BF16)
    dqk = q_ref.shape[1]
    qk = _dot(h, wqk_ref[...])
    q_ref[...] = qk[:, :dqk].astype(q_ref.dtype)
    k_ref[...] = (qk[:, dqk:] * k_scale).astype(k_ref.dtype)
    v_ref[...] = _dot(h, wv_ref[...]).astype(v_ref.dtype)
    o_ref[...] = _dot(h, wo_ref[...])
    gates = _dot(h, wgt_ref[...]) + bg_ref[...]
    gates = GATE_CAP * jnp.tanh(gates / GATE_CAP)
    logsig = jnp.minimum(gates, 0.0) - jnp.log1p(jnp.exp(-jnp.abs(gates)))
    lane = lax.broadcasted_iota(jnp.int32, gates.shape, 1)
    g_ref[...] = jnp.where(lane < A_HEADS, gates, logsig)


def _inproj(x, gn, wqk, wv, wo, wgt, bg, act_dtype, name):
    n, d = x.shape
    tm = _token_tile(n)
    dqk = _shape(wqk)[1] // 2
    dv = _shape(wv)[1]
    kern = functools.partial(_inproj_kernel, k_scale=(dqk // A_HEADS) ** -0.5)
    args, specs = _operands([(x, _rows(tm, d))], [gn, wqk, wv, wo, wgt, bg])
    return pl.pallas_call(
        kern,
        grid=(n // tm,),
        in_specs=specs,
        out_specs=[_rows(tm, dqk), _rows(tm, dqk), _rows(tm, dv), _rows(tm, dv), _rows(tm, LANES)],
        out_shape=[jax.ShapeDtypeStruct((n, dqk), act_dtype), jax.ShapeDtypeStruct((n, dqk), act_dtype),
                   jax.ShapeDtypeStruct((n, dv), act_dtype), jax.ShapeDtypeStruct((n, dv), F32),
                   jax.ShapeDtypeStruct((n, LANES), F32)],
        compiler_params=_params("parallel"),
        name=name,
    )(*args)


def _rope(x, cos, sin):
    width = x.shape[1]
    reps = width // LANES
    cosw = jnp.concatenate([cos] * reps, axis=1)
    sinw = jnp.concatenate([sin] * reps, axis=1)
    half = ROT_DIM // 2
    lane = lax.broadcasted_iota(jnp.int32, x.shape, 1)
    first_half = (lane & (B_HEAD_DIM - 1)) < half
    partner = jnp.where(first_half, pltpu.roll(x, width - half, 1), pltpu.roll(x, half, 1))
    return x * cosw + partner * sinw


def _kv_kernel(x_ref, cos_ref, sin_ref, gn_ref, w_ref, b_ref, k_ref, v_ref):
    h = _rms(x_ref[...], gn_ref[...]).astype(BF16)
    kv = _dot(h, w_ref[...]) + b_ref[...]
    kw = k_ref.shape[1]
    k_ref[...] = _rope(kv[:, :kw], cos_ref[...], sin_ref[...])
    v_ref[...] = kv[:, kw:]


def _pos_spec(tm, table):
    nblk = table.shape[0] // tm
    return pl.BlockSpec((tm, LANES), lambda i: (i % nblk, 0))


def _kv(x, gn, w, b, cos, sin, name):
    n, d = x.shape
    tm = _token_tile(n)
    kw = _shape(w)[1] // 2
    args, specs = _operands([(x, _rows(tm, d)), (cos, _pos_spec(tm, cos)), (sin, _pos_spec(tm, sin))],
                            [gn, w, b])
    return pl.pallas_call(
        _kv_kernel,
        grid=(n // tm,),
        in_specs=specs,
        out_specs=[_rows(tm, kw), _rows(tm, kw)],
        out_shape=[jax.ShapeDtypeStruct((n, kw), F32), jax.ShapeDtypeStruct((n, kw), F32)],
        compiler_params=_params("parallel"),
        name=name,
    )(*args)


def _qproj_kernel(x_ref, cos_ref, sin_ref, gn_ref, w_ref, b_ref, q_ref, *, q_scale):
    h = _rms(x_ref[...], gn_ref[...]).astype(BF16)
    q = _dot(h, w_ref[...]) + b_ref[...]
    q_ref[...] = (_rope(q, cos_ref[...], sin_ref[...]) * q_scale).astype(q_ref.dtype)


def _qproj(x, gn, w, b, cos, sin, act_dtype, name):
    n, d = x.shape
    tm = _token_tile(n)
    kern = functools.partial(_qproj_kernel, q_scale=B_HEAD_DIM ** -0.5)
    qw = _shape(w)[1]
    args, specs = _operands([(x, _rows(tm, d)), (cos, _pos_spec(tm, cos)), (sin, _pos_spec(tm, sin))],
                            [gn, w, b])
    return pl.pallas_call(
        kern,
        grid=(n // tm,),
        in_specs=specs,
        out_specs=_rows(tm, qw),
        out_shape=jax.ShapeDtypeStruct((n, qw), act_dtype),
        compiler_params=_params("parallel"),
        name=name,
    )(*args)


def _mlstm_kernel(q_ref, k_ref, v_ref, o_ref, gtm_ref, gt_ref, hn_ref, c0_ref, n0_ref, m0_ref, *rest,
                  bblk, chunk):
    out_ref, c_ref, n_ref, m_ref = rest[-4:]
    L = chunk
    dqk = q_ref.shape[2] // A_HEADS
    dv = v_ref.shape[2] // A_HEADS

    @pl.when(pl.program_id(1) == 0)
    def _():
        c_ref[...] = c0_ref[...]
        n_ref[...] = n0_ref[...]
        m_ref[...] = m0_ref[...]

    row = lax.broadcasted_iota(jnp.int32, (L, L), 0)
    col = lax.broadcasted_iota(jnp.int32, (L, L), 1)
    tril = col <= row
    lower_ones = tril.astype(F32)
    upper_ones = (row <= col).astype(F32)

    for b in range(bblk):
        gt = gt_ref[b, 0]
        gtm = gtm_ref[b]
        csum_rows = _dot_f32(gt, upper_ones)
        csum_cols = _dot_f32(lower_ones, gtm)
        for h in range(A_HEADS):
            i_row = gt[h:h + 1, :]
            b_row = csum_rows[A_HEADS + h:A_HEADS + h + 1, :]
            i_col = gtm[:, h:h + 1]
            b_col = csum_cols[:, A_HEADS + h:A_HEADS + h + 1]
            b_last = b_col[L - 1:L, :]
            m_prev = m_ref[b, h:h + 1, :]
            n_prev = n_ref[b, h:h + 1, :]
            c_prev = c_ref[b, h]

            qh = q_ref[b, :, h * dqk:(h + 1) * dqk].astype(BF16)
            kh = k_ref[b, :, h * dqk:(h + 1) * dqk].astype(BF16)
            vh = v_ref[b, :, h * dv:(h + 1) * dv].astype(BF16)

            dmat = jnp.where(tril, (b_col - b_row) + i_row, -jnp.inf)
            g = b_col + m_prev
            mt = jnp.maximum(g, jnp.max(dmat, axis=-1, keepdims=True))
            w_inter = jnp.exp(g - mt)
            s = _dot_nt(qh, kh) * jnp.exp(dmat - mt)
            num = _dot(s.astype(BF16), vh) + w_inter * _dot(qh, c_prev.astype(BF16))
            qn = jnp.sum(qh.astype(F32) * n_prev, axis=-1, keepdims=True)
            den = jnp.sum(s, axis=-1, keepdims=True) + w_inter * qn
            hc = num / jnp.maximum(jnp.abs(den), jnp.exp(-mt))

            m_new = mt[L - 1:L, :]
            a_col = jnp.exp(((b_last - b_col) + i_col) - m_new)
            decay = jnp.exp((b_last + m_prev) - m_new)
            ka = kh.astype(F32) * a_col
            c_ref[b, h] = decay * c_prev + _dot_tn(ka.astype(BF16), vh)
            n_ref[b, h:h + 1, :] = decay * n_prev + jnp.sum(ka, axis=0, keepdims=True)
            m_ref[b, h:h + 1, :] = m_new

            hn = _rms(hc, hn_ref[:, h * dv:(h + 1) * dv])
            gate = jax.nn.sigmoid(o_ref[b, :, h * dv:(h + 1) * dv])
            out_ref[b, :, h * dv:(h + 1) * dv] = (hn * gate).astype(out_ref.dtype)


def _mlstm(q, k, v, o, gates, head_norm, state0, layer0, layer, n_layers, prev_final, bblk, out_dtype, name):
    B, S, qw = q.shape
    vw = v.shape[2]
    L = A_CHUNK if S % A_CHUNK == 0 else S
    nc = S // L
    ng = 2 * A_HEADS
    gt = jnp.swapaxes(gates[:, :, :ng].reshape(B, nc, L, ng), 2, 3)
    kern = functools.partial(_mlstm_kernel, bblk=bblk, chunk=L)
    seq = lambda w: pl.BlockSpec((bblk, L, w), lambda i, c: (i, c, 0))

    def state(a, row):
        return pl.BlockSpec((None, bblk) + a.shape[2:], lambda i, c: (row, i) + (0,) * (a.ndim - 2))

    hn_arr, hn_spec = _resident(head_norm)
    args = [q, k, v, o, gates, gt, hn_arr, *state0]
    specs = [seq(qw), seq(qw), seq(vw), seq(vw), seq(LANES),
             pl.BlockSpec((bblk, 1, ng, L), lambda i, c: (i, c, 0, 0)), hn_spec,
             *[state(a, layer0) for a in state0]]
    aliases = {}
    if prev_final is not None:
        aliases = {len(args) + t: 1 + t for t in range(len(prev_final))}
        args += list(prev_final)
        specs += [pl.BlockSpec(memory_space=pl.ANY)] * len(prev_final)
    return pl.pallas_call(
        kern,
        grid=(B // bblk, nc),
        in_specs=specs,
        out_specs=[seq(vw)] + [state(a, layer) for a in state0],
        out_shape=[jax.ShapeDtypeStruct((B, S, vw), out_dtype)]
                  + [jax.ShapeDtypeStruct((n_layers,) + a.shape[1:], F32) for a in state0],
        input_output_aliases=aliases,
        compiler_params=_params("parallel", "arbitrary"),
        name=name,
    )(*args)


def _softmax_pv(scores, masks, vals, sink_col):
    scores = [jnp.where(m, s, -jnp.inf) for s, m in zip(scores, masks)]
    mx = sink_col
    for s in scores:
        mx = jnp.maximum(mx, jnp.max(s, axis=-1, keepdims=True))
    denom = jnp.exp(sink_col - mx)
    acc = None
    for s, v in zip(scores, vals):
        p = jnp.exp(s - mx)
        denom = denom + jnp.sum(p, axis=-1, keepdims=True)
        pv = _dot(p.astype(BF16), v)
        acc = pv if acc is None else acc + pv
    return acc / denom


def _sink_column(sinks_ref, kv_head, group, rows):
    ridx = lax.broadcasted_iota(jnp.int32, (group * rows, 1), 0)
    col = jnp.full((group * rows, 1), sinks_ref[0, kv_head * group], F32)
    for g in range(1, group):
        col = jnp.where(ridx >= g * rows, sinks_ref[0, kv_head * group + g], col)
    return col


def _pair_block_diag(x):
    zero = jnp.zeros_like(x)
    return jnp.concatenate([jnp.concatenate([x, zero], axis=1), jnp.concatenate([zero, x], axis=1)], axis=0)


def _swa_prompt_kernel(sinks_ref, q_ref, kp_ref, kc_ref, vp_ref, vc_ref, o_ref, *, group):
    W = q_ref.shape[1]
    hd = B_HEAD_DIM
    pairs = group // 2
    R = pairs * W
    has_prev = pl.program_id(1) > 0
    t = lax.broadcasted_iota(jnp.int32, (R, 2 * W), 0) & (W - 1)
    s = lax.broadcasted_iota(jnp.int32, (R, 2 * W), 1) & (W - 1)
    use_cur = s <= t
    visible = jnp.logical_or(use_cur, has_prev)
    row = lax.broadcasted_iota(jnp.int32, (R, 1), 0)
    first_head = lax.broadcasted_iota(jnp.int32, (R, 2 * hd), 1) < hd
    ones = jnp.ones((W, hd), BF16)
    for kh in range(B_KV_HEADS):
        ksl = slice(kh * hd, (kh + 1) * hd)
        q2 = jnp.concatenate(
            [q_ref[0, :, (kh * pairs + pr) * 2 * hd:(kh * pairs + pr + 1) * 2 * hd] for pr in range(pairs)],
            axis=0).astype(BF16)
        sc_cur = _dot_nt(q2, _pair_block_diag(kc_ref[0, :, ksl].astype(BF16)))
        sc_prev = _dot_nt(q2, _pair_block_diag(kp_ref[0, :, ksl].astype(BF16)))
        sc = jnp.where(visible, jnp.where(use_cur, sc_cur, sc_prev), -jnp.inf)
        probs, sink_terms = [], []
        for g in range(2):
            sink = jnp.full((R, 1), sinks_ref[0, kh * group + g], F32)
            for pr in range(1, pairs):
                sink = jnp.where(row >= pr * W, sinks_ref[0, kh * group + 2 * pr + g], sink)
            sg = sc[:, g * W:(g + 1) * W]
            mx = jnp.maximum(jnp.max(sg, axis=-1, keepdims=True), sink)
            probs.append(jnp.exp(sg - mx))
            sink_terms.append(jnp.exp(sink - mx))
        p = jnp.concatenate(probs, axis=1)
        p_cur = jnp.where(use_cur, p, 0.0).astype(BF16)
        p_prev = jnp.where(use_cur, 0.0, p).astype(BF16)
        v_cur = jnp.concatenate([_pair_block_diag(vc_ref[0, :, ksl].astype(BF16)), _pair_block_diag(ones)], axis=1)
        v_prev = jnp.concatenate([_pair_block_diag(vp_ref[0, :, ksl].astype(BF16)), _pair_block_diag(ones)], axis=1)
        acc = _dot(p_cur, v_cur) + _dot(p_prev, v_prev)
        denom = acc[:, 2 * hd:] + jnp.where(first_head, sink_terms[0], sink_terms[1])
        out = acc[:, :2 * hd] / denom
        for pr in range(pairs):
            lo = (kh * pairs + pr) * 2 * hd
            o_ref[0, :, lo:lo + 2 * hd] = out[pr * W:(pr + 1) * W, :].astype(o_ref.dtype)


def _swa_prompt(q, k, v, sinks, out_dtype, name):
    B, S, qw = q.shape
    kw = k.shape[2]
    nb = S // WINDOW
    group = qw // kw
    if group % 2 or 2 * B_HEAD_DIM != LANES:
        raise ValueError("the prompt attention kernel pairs query heads into LANES-wide slices")
    cur = lambda w: pl.BlockSpec((1, WINDOW, w), lambda b, j: (b, j, 0))
    prev = lambda w: pl.BlockSpec((1, WINDOW, w), lambda b, j: (b, jnp.maximum(j - 1, 0), 0))
    kern = functools.partial(_swa_prompt_kernel, group=group)
    return pl.pallas_call(
        kern,
        grid=(B, nb),
        in_specs=[pl.BlockSpec(memory_space=pltpu.SMEM), cur(qw), prev(kw), cur(kw), prev(kw), cur(kw)],
        out_specs=cur(qw),
        out_shape=jax.ShapeDtypeStruct((B, S, qw), out_dtype),
        compiler_params=_params("parallel", "parallel"),
        name=name,
    )(sinks, q, k, k, v, v)


def _swa_sample_kernel(sinks_ref, q_ref, kc_ref, kn_ref, vc_ref, vn_ref, o_ref, *, group, bblk):
    T = q_ref.shape[1]
    wb = kc_ref.shape[1]
    hd = B_HEAD_DIM
    t = lax.broadcasted_iota(jnp.int32, (group * T, wb), 0) % T
    i = lax.broadcasted_iota(jnp.int32, (group * T, wb), 1)
    mask_buf = jnp.logical_and((t + wb) - i < WINDOW, (PAST_LEN - wb) + i >= 0)
    tn = lax.broadcasted_iota(jnp.int32, (group * T, T), 0) % T
    u = lax.broadcasted_iota(jnp.int32, (group * T, T), 1)
    mask_new = u <= tn
    for b in range(bblk):
        for kh in range(B_KV_HEADS):
            ksl = slice(kh * hd, (kh + 1) * hd)
            q = jnp.concatenate(
                [q_ref[b, :, (kh * group + g) * hd:(kh * group + g + 1) * hd] for g in range(group)], axis=0)
            q = q.astype(BF16)
            sc_buf = _dot_nt(q, kc_ref[b, :, ksl].astype(BF16))
            sc_new = _dot_nt(q, kn_ref[b, :, ksl].astype(BF16))
            out = _softmax_pv([sc_buf, sc_new], [mask_buf, mask_new],
                              [vc_ref[b, :, ksl].astype(BF16), vn_ref[b, :, ksl].astype(BF16)],
                              _sink_column(sinks_ref, kh, group, T))
            for g in range(group):
                lo = (kh * group + g) * hd
                o_ref[b, :, lo:lo + hd] = out[g * T:(g + 1) * T, :].astype(o_ref.dtype)


def _swa_sample(q, k_buf, k_new, v_buf, v_new, sinks, bblk, name):
    B, T, qw = q.shape
    wb, kw = k_buf.shape[1:]
    group = qw // kw
    blk = lambda r, w: pl.BlockSpec((bblk, r, w), lambda i: (i, 0, 0))
    kern = functools.partial(_swa_sample_kernel, group=group, bblk=bblk)
    return pl.pallas_call(
        kern,
        grid=(B // bblk,),
        in_specs=[pl.BlockSpec(memory_space=pltpu.SMEM), blk(T, qw), blk(wb, kw), blk(T, kw),
                  blk(wb, kw), blk(T, kw)],
        out_specs=blk(T, qw),
        out_shape=jax.ShapeDtypeStruct((B, T, qw), F32),
        compiler_params=_params("parallel"),
        name=name,
    )(sinks, q, k_buf, k_new, v_buf, v_new)


def _rope_tables(pos):
    half = ROT_DIM // 2
    inv_freq = jnp.exp(-math.log(ROPE_THETA) * jnp.arange(0, ROT_DIM, 2, dtype=F32) / ROT_DIM)
    ang = pos.astype(F32)[:, None] * inv_freq[None, :]
    d = jnp.arange(LANES, dtype=jnp.int32) % B_HEAD_DIM
    ang_l = jnp.take(ang, d % half, axis=1)
    rot = (d < ROT_DIM)[None, :]
    sign = jnp.where(d < half, -1.0, 1.0)[None, :]
    cos = jnp.where(rot, jnp.cos(ang_l), 1.0)
    sin = jnp.where(rot, jnp.sin(ang_l) * sign, 0.0)
    return cos.astype(F32), sin.astype(F32)


def _largest_divisor(n, cap):
    d = min(n, cap)
    while n % d:
        d -= 1
    return d


def _trunk(x, pos_tables, state, cache, p, tag, act_dtype):
    B, T, D = x.shape
    n_a = p["a_wqk"].shape[0]
    depth = p["ffn_norm"].shape[0]
    cos, sin = pos_tables
    xs = x.reshape(B * T, D)
    final_state = None
    k_new = v_new = None
    mix = None
    for l in range(depth):
        if l == n_a:
            if mix is not None:
                raise AssertionError("pending mixer output")
            k_new, v_new = _kv(xs, p["kv_norm"], p["w_kv"], p["b_kv"], cos, sin, name=f"kv_{tag}")
        xs = _ffn(xs, _at(p["ffn_norm"], l, 0), _at(p["ffn_wg"], l, 0), _at(p["ffn_wu"], l, 0),
                  _at(p["ffn_wd"], l, 0), name=f"ffn_{tag}_{l}a")
        if l < n_a:
            q, k, v, o, gates = _inproj(xs, _at(p["mix_norm"], l), _at(p["a_wqk"], l), _at(p["a_wv"], l),
                                        _at(p["a_wo"], l), _at(p["a_wgt"], l), _at(p["a_bg"], l), act_dtype,
                                        name=f"inproj_{tag}_{l}")
            seq = lambda a: a.reshape(B, T, a.shape[1])
            bblk = _largest_divisor(B, 2 if T > A_CHUNK else 4)
            layer0 = l if state[0].shape[0] == n_a else 0
            hg, *final_state = _mlstm(seq(q), seq(k), seq(v), seq(o), seq(gates), _at(p["a_head_norm"], l),
                                      state, layer0, l, n_a, final_state, bblk, act_dtype,
                                      name=f"mlstm_{tag}_{l}")
            mix = (hg.reshape(B * T, -1), _at(p["a_w_out"], l), p["zero_bias"])
        else:
            j = l - n_a
            q = _qproj(xs, _at(p["mix_norm"], l), _at(p["b_w_q"], j), _at(p["b_b_q"], j), cos, sin, act_dtype,
                       name=f"qproj_{tag}_{l}")
            q = q.reshape(B, T, -1)
            kn, vn = k_new.reshape(B, T, -1), v_new.reshape(B, T, -1)
            if cache is None:
                o = _swa_prompt(q, kn, vn, p["b_sinks"][j], act_dtype, name=f"swa_{tag}_{l}")
            else:
                kc, vc = cache
                o = _swa_sample(q, kc.reshape(B, kc.shape[1], -1), kn, vc.reshape(B, vc.shape[1], -1), vn,
                                p["b_sinks"][j], _largest_divisor(B, 8), name=f"swa_{tag}_{l}")
            mix = (o.reshape(B * T, -1), _at(p["b_w_o"], j), _at(p["b_b_o"], j))
        last = l == depth - 1
        xs = _ffn(xs, _at(p["ffn_norm"], l, 1), _at(p["ffn_wg"], l, 1), _at(p["ffn_wu"], l, 1),
                  _at(p["ffn_wd"], l, 1), mix=mix, final_gain=p["final_norm"] if last else None,
                  name=f"ffn_{tag}_{l}b")
        mix = None
    y = xs.reshape(B, T, D)
    new_c, new_n, new_m = final_state
    kvh = (B, T, B_KV_HEADS, B_HEAD_DIM)
    k_new, v_new = k_new.reshape(kvh), v_new.reshape(kvh)
    if cache is None:
        wb = min(WINDOW, T)
        buf_k, buf_v = k_new[:, -wb:], v_new[:, -wb:]
    else:
        wb = cache[0].shape[1]
        buf_k = jnp.concatenate([cache[0], k_new], axis=1)[:, -wb:]
        buf_v = jnp.concatenate([cache[1], v_new], axis=1)[:, -wb:]
    return y, new_c, new_n, new_m[..., 0], buf_k, buf_v


def kernel(x_prompt, x_sample, state_mlstm_C, state_mlstm_n, state_mlstm_m, cache_swa_k, cache_swa_v,
           ffn_norm, ffn_w_gate, ffn_w_up, ffn_w_down, mix_norm, a_w_in, a_b_gate, a_head_norm, a_w_out,
           kv_norm, w_kv, b_kv, b_w_q, b_b_q, b_sinks, b_w_o, b_b_o, final_norm):
    n_a, d_model, a_proj = a_w_in.shape
    ng = 2 * A_HEADS
    qk_w = d_model // 2
    v_w = d_model
    if a_proj != 2 * qk_w + 2 * v_w + ng:
        raise ValueError("unexpected mLSTM projection width")
    row = lambda a: a[..., None, :].astype(F32)
    p = {
        "ffn_norm": row(ffn_norm), "mix_norm": row(mix_norm), "kv_norm": row(kv_norm),
        "final_norm": row(final_norm), "a_head_norm": row(a_head_norm),
        "ffn_wg": ffn_w_gate.astype(BF16), "ffn_wu": ffn_w_up.astype(BF16), "ffn_wd": ffn_w_down.astype(BF16),
        "a_wqk": a_w_in[:, :, :2 * qk_w].astype(BF16),
        "a_wv": a_w_in[:, :, 2 * qk_w:2 * qk_w + v_w].astype(BF16),
        "a_wo": a_w_in[:, :, 2 * qk_w + v_w:2 * qk_w + 2 * v_w].astype(BF16),
        "a_wgt": jnp.pad(a_w_in[:, :, 2 * qk_w + 2 * v_w:], ((0, 0), (0, 0), (0, LANES - ng))).astype(BF16),
        "a_bg": jnp.pad(a_b_gate, ((0, 0), (0, LANES - ng)))[:, None, :].astype(F32),
        "a_w_out": a_w_out.astype(BF16),
        "zero_bias": jnp.zeros((1, d_model), F32),
        "w_kv": w_kv.astype(BF16), "b_kv": row(b_kv),
        "b_w_q": b_w_q.astype(BF16), "b_b_q": row(b_b_q), "b_sinks": row(b_sinks),
        "b_w_o": b_w_o.astype(BF16), "b_b_o": row(b_b_o),
    }
    Bp, Sp, _ = x_prompt.shape
    Bs, Ts, _ = x_sample.shape
    zc = jnp.zeros((1, Bp) + state_mlstm_C.shape[2:], F32)
    zn = jnp.zeros((1, Bp) + state_mlstm_n.shape[2:], F32)
    zm = jnp.zeros((1, Bp) + state_mlstm_m.shape[2:] + (1,), F32)
    s_state = (state_mlstm_C, state_mlstm_n, state_mlstm_m[..., None])

    p_rows = math.lcm(Sp, _token_tile(Bp * Sp))
    p_tables = _rope_tables(jnp.arange(p_rows, dtype=jnp.int32) % Sp)
    s_rows = math.lcm(Ts, _token_tile(Bs * Ts))
    s_tables = _rope_tables(PAST_LEN + jnp.arange(s_rows, dtype=jnp.int32) % Ts)

    y_p, p_c, p_n, p_m, p_k, p_v = _trunk(x_prompt, p_tables, (zc, zn, zm), None, p, "p", BF16)
    y_s, s_c, s_n, s_m, s_k, s_v = _trunk(x_sample, s_tables, s_state, (cache_swa_k, cache_swa_v), p, "s", F32)
    return (y_p, y_s, p_c, p_n, p_m, p_k, p_v, s_c, s_n, s_m, s_k, s_v)
```

```python
import functools
import math

import jax
import jax.numpy as jnp
from jax import lax
from jax.experimental import pallas as pl
from jax.experimental.pallas import tpu as pltpu

F32 = jnp.float32
BF16 = jnp.bfloat16

EPS = 1e-6
GATE_CAP = 15.0
A_HEADS = 4
MLSTM_CHUNK = 256
B_HEAD_DIM = 64
B_KV_HEADS = 4
WINDOW = 128
ROT_DIM = B_HEAD_DIM // 4
ROPE_THETA = 500000.0
PAST_LEN = 8192

LANES = 128
V7X_VMEM_LIMIT_BYTES = 56 * 1024 * 1024
TOKEN_TILE = 512
FF_TILE = 256


def _rms(x, g):
    return (x * lax.rsqrt(jnp.mean(x * x, axis=-1, keepdims=True) + EPS)) * g


def _dot(a, b):
    return jnp.dot(a, b, preferred_element_type=F32)


def _dot_nt(a, b):
    return lax.dot_general(a, b, (((1,), (1,)), ((), ())), preferred_element_type=F32)


def _dot_f32(a, b):
    return jnp.dot(a, b, preferred_element_type=F32, precision=lax.Precision.HIGHEST)


def _at(arr, *idx):
    return (arr, idx)


def _split(param):
    return param if isinstance(param, tuple) else (param, ())


def _shape(param):
    arr, idx = _split(param)
    return arr.shape[len(idx):]


def _resident(param):
    arr, idx = _split(param)
    block = (None,) * len(idx) + arr.shape[len(idx):]
    index = idx + (0,) * (arr.ndim - len(idx))
    return arr, pl.BlockSpec(block, lambda *_: index, pipeline_mode=pl.Buffered(1))


def _operands(tiled, residents):
    pairs = list(tiled) + [_resident(r) for r in residents]
    return [a for a, _ in pairs], [sp for _, sp in pairs]


def _rows(tm, width):
    return pl.BlockSpec((tm, width), lambda i: (i, 0))


def _params(*sem):
    return pltpu.CompilerParams(dimension_semantics=sem, vmem_limit_bytes=V7X_VMEM_LIMIT_BYTES)


def _token_tile(n):
    tm = TOKEN_TILE
    while n % tm:
        tm //= 2
    return tm


def _ffn_kernel(*refs, premix, final, tf):
    refs = list(refs)
    x_ref = refs.pop(0)
    if premix:
        hg_ref, wmix_ref, bmix_ref = refs.pop(0), refs.pop(0), refs.pop(0)
    gn_ref, wg_ref, wu_ref, wd_ref = refs.pop(0), refs.pop(0), refs.pop(0), refs.pop(0)
    if final:
        fg_ref = refs.pop(0)
    o_ref, a_ref = refs

    x = x_ref[...]
    if premix:
        x = x + (_dot(hg_ref[...].astype(BF16), wmix_ref[...]) + bmix_ref[...])
    h = _rms(x, gn_ref[...]).astype(BF16)
    for j in range(wg_ref.shape[1] // tf):
        sl = slice(j * tf, (j + 1) * tf)
        g = _dot(h, wg_ref[:, sl])
        u = _dot(h, wu_ref[:, sl])
        a_ref[:, sl] = ((g * jax.nn.sigmoid(g)) * u).astype(BF16)
    out = x + 0.5 * _dot(a_ref[...], wd_ref[...])
    if final:
        out = _rms(out, fg_ref[...])
    o_ref[...] = out


def _ffn(x, gn, wg, wu, wd, mix=None, final_gain=None, name="ffn"):
    n, d = x.shape
    dff = _shape(wg)[1]
    tm = _token_tile(n)
    tf = FF_TILE if dff % FF_TILE == 0 else dff
    tiled, residents = [(x, _rows(tm, d))], []
    if mix is not None:
        hg, wmix, bmix = mix
        tiled.append((hg, _rows(tm, hg.shape[1])))
        residents += [wmix, bmix]
    residents += [gn, wg, wu, wd]
    if final_gain is not None:
        residents.append(final_gain)
    args, specs = _operands(tiled, residents)
    kern = functools.partial(_ffn_kernel, premix=mix is not None, final=final_gain is not None, tf=tf)
    return pl.pallas_call(
        kern,
        grid=(n // tm,),
        in_specs=specs,
        out_specs=_rows(tm, d),
        out_shape=jax.ShapeDtypeStruct((n, d), F32),
        scratch_shapes=[pltpu.VMEM((tm, dff), BF16)],
        compiler_params=_params("parallel"),
        name=name,
    )(*args)


def _inproj_kernel(x_ref, gn_ref, wqk_ref, wv_ref, wo_ref, wgt_ref, bg_ref,
                   q_ref, k_ref, v_ref, o_ref, g_ref, *, k_scale):
    h = _rms(x_ref[...], gn_ref[...]).astype(BF16)
    dqk = q_ref.shape[1]
    qk = _dot(h, wqk_ref[...])
    q_ref[...] = qk[:, :dqk].astype(q_ref.dtype)
    k_ref[...] = (qk[:, dqk:] * k_scale).astype(k_ref.dtype)
    v_ref[...] = _dot(h, wv_ref[...]).astype(v_ref.dtype)
    o_ref[...] = _dot(h, wo_ref[...])
    gates = _dot(h, wgt_ref[...]) + bg_ref[...]
    gates = GATE_CAP * jnp.tanh(gates / GATE_CAP)
    logsig = jnp.minimum(gates, 0.0) - jnp.log1p(jnp.exp(-jnp.abs(gates)))
    lane = lax.broadcasted_iota(jnp.int32, gates.shape, 1)
    g_ref[...] = jnp.where(lane < A_HEADS, gates, logsig)


def _inproj(x, gn, wqk, wv, wo, wgt, bg, act_dtype, name):
    n, d = x.shape
    tm = _token_tile(n)
    dqk = _shape(wqk)[1] // 2
    dv = _shape(wv)[1]
    kern = functools.partial(_inproj_kernel, k_scale=(dqk // A_HEADS) ** -0.5)
    args, specs = _operands([(x, _rows(tm, d))], [gn, wqk, wv, wo, wgt, bg])
    return pl.pallas_call(
        kern,
        grid=(n // tm,),
        in_specs=specs,
        out_specs=[_rows(tm, dqk), _rows(tm, dqk), _rows(tm, dv), _rows(tm, dv), _rows(tm, LANES)],
        out_shape=[jax.ShapeDtypeStruct((n, dqk), act_dtype), jax.ShapeDtypeStruct((n, dqk), act_dtype),
                   jax.ShapeDtypeStruct((n, dv), act_dtype), jax.ShapeDtypeStruct((n, dv), F32),
                   jax.ShapeDtypeStruct((n, LANES), F32)],
        compiler_params=_params("parallel"),
        name=name,
    )(*args)


def _rope(x, cos, sin):
    width = x.shape[1]
    reps = width // LANES
    cosw = jnp.concatenate([cos] * reps, axis=1)
    sinw = jnp.concatenate([sin] * reps, axis=1)
    half = ROT_DIM // 2
    lane = lax.broadcasted_iota(jnp.int32, x.shape, 1)
    first_half = (lane & (B_HEAD_DIM - 1)) < half
    partner = jnp.where(first_half, pltpu.roll(x, width - half, 1), pltpu.roll(x, half, 1))
    return x * cosw + partner * sinw


def _kv_kernel(x_ref, cos_ref, sin_ref, gn_ref, w_ref, b_ref, k_ref, v_ref):
    h = _rms(x_ref[...], gn_ref[...]).astype(BF16)
    kv = _dot(h, w_ref[...]) + b_ref[...]
    kw = k_ref.shape[1]
    k_ref[...] = _rope(kv[:, :kw], cos_ref[...], sin_ref[...])
    v_ref[...] = kv[:, kw:]


def _pos_spec(tm, table):
    nblk = table.shape[0] // tm
    return pl.BlockSpec((tm, LANES), lambda i: (i % nblk, 0))


def _kv(x, gn, w, b, cos, sin, name):
    n, d = x.shape
    tm = _token_tile(n)
    kw = _shape(w)[1] // 2
    args, specs = _operands([(x, _rows(tm, d)), (cos, _pos_spec(tm, cos)), (sin, _pos_spec(tm, sin))],
                            [gn, w, b])
    return pl.pallas_call(
        _kv_kernel,
        grid=(n // tm,),
        in_specs=specs,
        out_specs=[_rows(tm, kw), _rows(tm, kw)],
        out_shape=[jax.ShapeDtypeStruct((n, kw), F32), jax.ShapeDtypeStruct((n, kw), F32)],
        compiler_params=_params("parallel"),
        name=name,
    )(*args)


def _qproj_kernel(x_ref, cos_ref, sin_ref, gn_ref, w_ref, b_ref, q_ref, *, q_scale):
    h = _rms(x_ref[...], gn_ref[...]).astype(BF16)
    q = _dot(h, w_ref[...]) + b_ref[...]
    q_ref[...] = (_rope(q, cos_ref[...], sin_ref[...]) * q_scale).astype(q_ref.dtype)


def _qproj(x, gn, w, b, cos, sin, act_dtype, name):
    n, d = x.shape
    tm = _token_tile(n)
    kern = functools.partial(_qproj_kernel, q_scale=B_HEAD_DIM ** -0.5)
    qw = _shape(w)[1]
    args, specs = _operands([(x, _rows(tm, d)), (cos, _pos_spec(tm, cos)), (sin, _pos_spec(tm, sin))],
                            [gn, w, b])
    return pl.pallas_call(
        kern,
        grid=(n // tm,),
        in_specs=specs,
        out_specs=_rows(tm, qw),
        out_shape=jax.ShapeDtypeStruct((n, qw), act_dtype),
        compiler_params=_params("parallel"),
        name=name,
    )(*args)


def _mlstm_kernel(q_ref, k_ref, v_ref, o_ref, gtm_ref, gt_ref, hn_ref, c0_ref, n0_ref, m0_ref, *rest,
                  bblk, chunk):
    out_ref, c_ref, n_ref, m_ref = rest[-4:]
    L = chunk
    dqk = q_ref.shape[2] // A_HEADS
    dv = v_ref.shape[2] // A_HEADS

    @pl.when(pl.program_id(1) == 0)
    def _():
        c_ref[...] = c0_ref[...]
        n_ref[...] = n0_ref[...]
        m_ref[...] = m0_ref[...]

    row = lax.broadcasted_iota(jnp.int32, (L, L), 0)
    col = lax.broadcasted_iota(jnp.int32, (L, L), 1)
    tril = col <= row
    lower_ones = tril.astype(F32)
    upper_ones = (row <= col).astype(F32)

    chains = [(b, h) for b in range(bblk) for h in range(A_HEADS)]
    per_chain = lambda f: jnp.stack([f(b, h) for b, h in chains])

    gts = [gt_ref[b, 0] for b in range(bblk)]
    gtms = [gtm_ref[b] for b in range(bblk)]
    csum_rows = [_dot_f32(gt, upper_ones) for gt in gts]
    csum_cols = [_dot_f32(lower_ones, gtm) for gtm in gtms]
    i_row = per_chain(lambda b, h: gts[b][h:h + 1, :])
    b_row = per_chain(lambda b, h: csum_rows[b][A_HEADS + h:A_HEADS + h + 1, :])
    i_col = per_chain(lambda b, h: gtms[b][:, h:h + 1])
    b_col = per_chain(lambda b, h: csum_cols[b][:, A_HEADS + h:A_HEADS + h + 1])
    b_last = b_col[:, L - 1:L, :]
    m_prev = per_chain(lambda b, h: m_ref[b, h:h + 1, :])
    n_prev = per_chain(lambda b, h: n_ref[b, h:h + 1, :])
    c_prev = per_chain(lambda b, h: c_ref[b, h])

    q = per_chain(lambda b, h: q_ref[b, :, h * dqk:(h + 1) * dqk]).astype(BF16)
    k = per_chain(lambda b, h: k_ref[b, :, h * dqk:(h + 1) * dqk]).astype(BF16)
    v = per_chain(lambda b, h: v_ref[b, :, h * dv:(h + 1) * dv]).astype(BF16)

    dmat = jnp.where(tril[None], (b_col - b_row) + i_row, -jnp.inf)
    g = b_col + m_prev
    mt = jnp.maximum(g, jnp.max(dmat, axis=-1, keepdims=True))
    w_inter = jnp.exp(g - mt)
    s = jnp.einsum("gtd,gsd->gts", q, k, preferred_element_type=F32) * jnp.exp(dmat - mt)
    num = (jnp.einsum("gts,gsv->gtv", s.astype(BF16), v, preferred_element_type=F32)
           + w_inter * jnp.einsum("gtd,gdv->gtv", q, c_prev.astype(BF16), preferred_element_type=F32))
    qn = jnp.sum(q.astype(F32) * n_prev, axis=-1, keepdims=True)
    den = jnp.sum(s, axis=-1, keepdims=True) + w_inter * qn
    hc = num / jnp.maximum(jnp.abs(den), jnp.exp(-mt))

    m_new = mt[:, L - 1:L, :]
    a_col = jnp.exp(((b_last - b_col) + i_col) - m_new)
    decay = jnp.exp((b_last + m_prev) - m_new)
    ka = k.astype(F32) * a_col
    kat = jnp.swapaxes(ka, 1, 2).astype(BF16)
    c_new = decay * c_prev + jnp.einsum("gds,gsv->gdv", kat, v, preferred_element_type=F32)
    n_new = decay * n_prev + jnp.sum(ka, axis=1, keepdims=True)

    gain = per_chain(lambda b, h: hn_ref[:, h * dv:(h + 1) * dv])
    gate = jax.nn.sigmoid(per_chain(lambda b, h: o_ref[b, :, h * dv:(h + 1) * dv]))
    res = (_rms(hc, gain) * gate).astype(out_ref.dtype)
    for gi, (b, h) in enumerate(chains):
        c_ref[b, h] = c_new[gi]
        n_ref[b, h:h + 1, :] = n_new[gi]
        m_ref[b, h:h + 1, :] = m_new[gi]
        out_ref[b, :, h * dv:(h + 1) * dv] = res[gi]


def _mlstm(q, k, v, o, gates, head_norm, state0, layer0, layer, n_layers, prev_final, bblk, out_dtype, name):
    B, S, qw = q.shape
    vw = v.shape[2]
    L = MLSTM_CHUNK if S % MLSTM_CHUNK == 0 else S
    nc = S // L
    ng = 2 * A_HEADS
    gt = jnp.swapaxes(gates[:, :, :ng].reshape(B, nc, L, ng), 2, 3)
    kern = functools.partial(_mlstm_kernel, bblk=bblk, chunk=L)
    seq = lambda w: pl.BlockSpec((bblk, L, w), lambda i, c: (i, c, 0))

    def state(a, row):
        return pl.BlockSpec((None, bblk) + a.shape[2:], lambda i, c: (row, i) + (0,) * (a.ndim - 2))

    hn_arr, hn_spec = _resident(head_norm)
    args = [q, k, v, o, gates, gt, hn_arr, *state0]
    specs = [seq(qw), seq(qw), seq(vw), seq(vw), seq(LANES),
             pl.BlockSpec((bblk, 1, ng, L), lambda i, c: (i, c, 0, 0)), hn_spec,
             *[state(a, layer0) for a in state0]]
    aliases = {}
    if prev_final is not None:
        aliases = {len(args) + t: 1 + t for t in range(len(prev_final))}
        args += list(prev_final)
        specs += [pl.BlockSpec(memory_space=pl.ANY)] * len(prev_final)
    return pl.pallas_call(
        kern,
        grid=(B // bblk, nc),
        in_specs=specs,
        out_specs=[seq(vw)] + [state(a, layer) for a in state0],
        out_shape=[jax.ShapeDtypeStruct((B, S, vw), out_dtype)]
                  + [jax.ShapeDtypeStruct((n_layers,) + a.shape[1:], F32) for a in state0],
        input_output_aliases=aliases,
        compiler_params=_params("parallel", "arbitrary"),
        name=name,
    )(*args)


def _pair_block_diag(x):
    zero = jnp.zeros_like(x)
    return jnp.concatenate([jnp.concatenate([x, zero], axis=1), jnp.concatenate([zero, x], axis=1)], axis=0)


def _swa_prompt_kernel(sinks_ref, q_ref, kp_ref, kc_ref, vp_ref, vc_ref, o_ref, *, group):
    W = q_ref.shape[1]
    hd = B_HEAD_DIM
    pairs = group // 2
    R = pairs * W
    has_prev = pl.program_id(1) > 0
    t = lax.broadcasted_iota(jnp.int32, (R, 2 * W), 0) & (W - 1)
    s = lax.broadcasted_iota(jnp.int32, (R, 2 * W), 1) & (W - 1)
    use_cur = s <= t
    visible = jnp.logical_or(use_cur, has_prev)
    row = lax.broadcasted_iota(jnp.int32, (R, 1), 0)
    first_head = lax.broadcasted_iota(jnp.int32, (R, 2 * hd), 1) < hd
    ones = jnp.ones((W, hd), BF16)
    for kh in range(B_KV_HEADS):
        ksl = slice(kh * hd, (kh + 1) * hd)
        q2 = jnp.concatenate(
            [q_ref[0, :, (kh * pairs + pr) * 2 * hd:(kh * pairs + pr + 1) * 2 * hd] for pr in range(pairs)],
            axis=0).astype(BF16)
        sc_cur = _dot_nt(q2, _pair_block_diag(kc_ref[0, :, ksl].astype(BF16)))
        sc_prev = _dot_nt(q2, _pair_block_diag(kp_ref[0, :, ksl].astype(BF16)))
        sc = jnp.where(visible, jnp.where(use_cur, sc_cur, sc_prev), -jnp.inf)
        probs, sink_terms = [], []
        for g in range(2):
            sink = jnp.full((R, 1), sinks_ref[0, kh * group + g], F32)
            for pr in range(1, pairs):
                sink = jnp.where(row >= pr * W, sinks_ref[0, kh * group + 2 * pr + g], sink)
            sg = sc[:, g * W:(g + 1) * W]
            mx = jnp.maximum(jnp.max(sg, axis=-1, keepdims=True), sink)
            probs.append(jnp.exp(sg - mx))
            sink_terms.append(jnp.exp(sink - mx))
        p = jnp.concatenate(probs, axis=1)
        p_cur = jnp.where(use_cur, p, 0.0).astype(BF16)
        p_prev = jnp.where(use_cur, 0.0, p).astype(BF16)
        v_cur = jnp.concatenate([_pair_block_diag(vc_ref[0, :, ksl].astype(BF16)), _pair_block_diag(ones)], axis=1)
        v_prev = jnp.concatenate([_pair_block_diag(vp_ref[0, :, ksl].astype(BF16)), _pair_block_diag(ones)], axis=1)
        acc = _dot(p_cur, v_cur) + _dot(p_prev, v_prev)
        denom = acc[:, 2 * hd:] + jnp.where(first_head, sink_terms[0], sink_terms[1])
        out = acc[:, :2 * hd] / denom
        for pr in range(pairs):
            lo = (kh * pairs + pr) * 2 * hd
            o_ref[0, :, lo:lo + 2 * hd] = out[pr * W:(pr + 1) * W, :].astype(o_ref.dtype)


def _swa_prompt(q, k, v, sinks, out_dtype, name):
    B, S, qw = q.shape
    kw = k.shape[2]
    nb = S // WINDOW
    group = qw // kw
    if group % 2 or 2 * B_HEAD_DIM != LANES:
        raise ValueError("the prompt attention kernel pairs query heads into LANES-wide slices")
    cur = lambda w: pl.BlockSpec((1, WINDOW, w), lambda b, j: (b, j, 0))
    prev = lambda w: pl.BlockSpec((1, WINDOW, w), lambda b, j: (b, jnp.maximum(j - 1, 0), 0))
    kern = functools.partial(_swa_prompt_kernel, group=group)
    return pl.pallas_call(
        kern,
        grid=(B, nb),
        in_specs=[pl.BlockSpec(memory_space=pltpu.SMEM), cur(qw), prev(kw), cur(kw), prev(kw), cur(kw)],
        out_specs=cur(qw),
        out_shape=jax.ShapeDtypeStruct((B, S, qw), out_dtype),
        compiler_params=_params("parallel", "parallel"),
        name=name,
    )(sinks, q, k, k, v, v)


def _swa_sample_kernel(sinks_ref, q_ref, kc_ref, kn_ref, vc_ref, vn_ref, o_ref, *, group, bblk):
    T = q_ref.shape[1]
    wb, kw = kc_ref.shape[1:]
    hd = B_HEAD_DIM
    nq = B_KV_HEADS * group
    R = nq * T
    qb = q_ref[...]
    blocks = []
    for kh in range(B_KV_HEADS):
        for g in range(group):
            lo = (kh * group + g) * hd
            parts = [qb[:, :, lo:lo + hd]]
            if kh:
                parts.insert(0, jnp.zeros((bblk, T, kh * hd), qb.dtype))
            if kh < B_KV_HEADS - 1:
                parts.append(jnp.zeros((bblk, T, (B_KV_HEADS - 1 - kh) * hd), qb.dtype))
            blocks.append(jnp.concatenate(parts, axis=-1))
    q_all = jnp.concatenate(blocks, axis=1).astype(BF16)

    t = lax.broadcasted_iota(jnp.int32, (R, wb), 0) % T
    i = lax.broadcasted_iota(jnp.int32, (R, wb), 1)
    mask_buf = jnp.logical_and((t + wb) - i < WINDOW, (PAST_LEN - wb) + i >= 0)
    tn = lax.broadcasted_iota(jnp.int32, (R, T), 0) % T
    u = lax.broadcasted_iota(jnp.int32, (R, T), 1)
    mask_new = u <= tn
    ridx = lax.broadcasted_iota(jnp.int32, (R, 1), 0)
    sink = jnp.full((R, 1), sinks_ref[0, 0], F32)
    for hq in range(1, nq):
        sink = jnp.where(ridx >= hq * T, sinks_ref[0, hq], sink)

    sc_buf = jnp.einsum("bqd,bkd->bqk", q_all, kc_ref[...].astype(BF16), preferred_element_type=F32)
    sc_new = jnp.einsum("bqd,bkd->bqk", q_all, kn_ref[...].astype(BF16), preferred_element_type=F32)
    sc_buf = jnp.where(mask_buf[None], sc_buf, -jnp.inf)
    sc_new = jnp.where(mask_new[None], sc_new, -jnp.inf)
    mx = jnp.maximum(jnp.maximum(jnp.max(sc_buf, axis=-1, keepdims=True),
                                 jnp.max(sc_new, axis=-1, keepdims=True)), sink[None])
    p_buf = jnp.exp(sc_buf - mx)
    p_new = jnp.exp(sc_new - mx)
    denom = (jnp.sum(p_buf, axis=-1, keepdims=True) + jnp.sum(p_new, axis=-1, keepdims=True)
             + jnp.exp(sink[None] - mx))
    acc = (jnp.einsum("bqk,bkd->bqd", p_buf.astype(BF16), vc_ref[...].astype(BF16), preferred_element_type=F32)
           + jnp.einsum("bqk,bkd->bqd", p_new.astype(BF16), vn_ref[...].astype(BF16), preferred_element_type=F32))
    out = (acc / denom).astype(o_ref.dtype)
    for kh in range(B_KV_HEADS):
        for g in range(group):
            hq = kh * group + g
            o_ref[:, :, hq * hd:(hq + 1) * hd] = out[:, hq * T:(hq + 1) * T, kh * hd:(kh + 1) * hd]


def _swa_sample(q, k_buf, k_new, v_buf, v_new, sinks, bblk, name):
    B, T, qw = q.shape
    wb, kw = k_buf.shape[1:]
    group = qw // kw
    blk = lambda r, w: pl.BlockSpec((bblk, r, w), lambda i: (i, 0, 0))
    kern = functools.partial(_swa_sample_kernel, group=group, bblk=bblk)
    return pl.pallas_call(
        kern,
        grid=(B // bblk,),
        in_specs=[pl.BlockSpec(memory_space=pltpu.SMEM), blk(T, qw), blk(wb, kw), blk(T, kw),
                  blk(wb, kw), blk(T, kw)],
        out_specs=blk(T, qw),
        out_shape=jax.ShapeDtypeStruct((B, T, qw), F32),
        compiler_params=_params("parallel"),
        name=name,
    )(sinks, q, k_buf, k_new, v_buf, v_new)


def _rope_tables(pos):
    half = ROT_DIM // 2
    inv_freq = jnp.exp(-math.log(ROPE_THETA) * jnp.arange(0, ROT_DIM, 2, dtype=F32) / ROT_DIM)
    ang = pos.astype(F32)[:, None] * inv_freq[None, :]
    d = jnp.arange(LANES, dtype=jnp.int32) % B_HEAD_DIM
    ang_l = jnp.take(ang, d % half, axis=1)
    rot = (d < ROT_DIM)[None, :]
    sign = jnp.where(d < half, -1.0, 1.0)[None, :]
    cos = jnp.where(rot, jnp.cos(ang_l), 1.0)
    sin = jnp.where(rot, jnp.sin(ang_l) * sign, 0.0)
    return cos.astype(F32), sin.astype(F32)


def _largest_divisor(n, cap):
    d = min(n, cap)
    while n % d:
        d -= 1
    return d


def _trunk(x, pos_tables, state, cache, p, tag, act_dtype):
    B, T, D = x.shape
    n_a = p["a_wqk"].shape[0]
    depth = p["ffn_norm"].shape[0]
    cos, sin = pos_tables
    xs = x.reshape(B * T, D)
    final_state = None
    k_new = v_new = None
    mix = None
    for l in range(depth):
        if l == n_a:
            if mix is not None:
                raise AssertionError("pending mixer output")
            k_new, v_new = _kv(xs, p["kv_norm"], p["w_kv"], p["b_kv"], cos, sin, name=f"kv_{tag}")
        xs = _ffn(xs, _at(p["ffn_norm"], l, 0), _at(p["ffn_wg"], l, 0), _at(p["ffn_wu"], l, 0),
                  _at(p["ffn_wd"], l, 0), name=f"ffn_{tag}_{l}a")
        if l < n_a:
            q, k, v, o, gates = _inproj(xs, _at(p["mix_norm"], l), _at(p["a_wqk"], l), _at(p["a_wv"], l),
                                        _at(p["a_wo"], l), _at(p["a_wgt"], l), _at(p["a_bg"], l), act_dtype,
                                        name=f"inproj_{tag}_{l}")
            seq = lambda a: a.reshape(B, T, a.shape[1])
            bblk = _largest_divisor(B, 2 if T >= MLSTM_CHUNK else 4)
            layer0 = l if state[0].shape[0] == n_a else 0
            hg, *final_state = _mlstm(seq(q), seq(k), seq(v), seq(o), seq(gates), _at(p["a_head_norm"], l),
                                      state, layer0, l, n_a, final_state, bblk, act_dtype,
                                      name=f"mlstm_{tag}_{l}")
            mix = (hg.reshape(B * T, -1), _at(p["a_w_out"], l), p["zero_bias"])
        else:
            j = l - n_a
            q = _qproj(xs, _at(p["mix_norm"], l), _at(p["b_w_q"], j), _at(p["b_b_q"], j), cos, sin, act_dtype,
                       name=f"qproj_{tag}_{l}")
            q = q.reshape(B, T, -1)
            kn, vn = k_new.reshape(B, T, -1), v_new.reshape(B, T, -1)
            if cache is None:
                o = _swa_prompt(q, kn, vn, p["b_sinks"][j], act_dtype, name=f"swa_{tag}_{l}")
            else:
                kc, vc = cache
                o = _swa_sample(q, kc.reshape(B, kc.shape[1], -1), kn, vc.reshape(B, vc.shape[1], -1), vn,
                                p["b_sinks"][j], _largest_divisor(B, 8), name=f"swa_{tag}_{l}")
            mix = (o.reshape(B * T, -1), _at(p["b_w_o"], j), _at(p["b_b_o"], j))
        last = l == depth - 1
        xs = _ffn(xs, _at(p["ffn_norm"], l, 1), _at(p["ffn_wg"], l, 1), _at(p["ffn_wu"], l, 1),
                  _at(p["ffn_wd"], l, 1), mix=mix, final_gain=p["final_norm"] if last else None,
                  name=f"ffn_{tag}_{l}b")
        mix = None
    y = xs.reshape(B, T, D)
    new_c, new_n, new_m = final_state
    kvh = (B, T, B_KV_HEADS, B_HEAD_DIM)
    k_new, v_new = k_new.reshape(kvh), v_new.reshape(kvh)
    if cache is None:
        wb = min(WINDOW, T)
        buf_k, buf_v = k_new[:, -wb:], v_new[:, -wb:]
    else:
        wb = cache[0].shape[1]
        buf_k = jnp.concatenate([cache[0], k_new], axis=1)[:, -wb:]
        buf_v = jnp.concatenate([cache[1], v_new], axis=1)[:, -wb:]
    return y, new_c, new_n, new_m[..., 0], buf_k, buf_v


def kernel(x_prompt, x_sample, state_mlstm_C, state_mlstm_n, state_mlstm_m, cache_swa_k, cache_swa_v,
           ffn_norm, ffn_w_gate, ffn_w_up, ffn_w_down, mix_norm, a_w_in, a_b_gate, a_head_norm, a_w_out,
           kv_norm, w_kv, b_kv, b_w_q, b_b_q, b_sinks, b_w_o, b_b_o, final_norm):
    n_a, d_model, a_proj = a_w_in.shape
    ng = 2 * A_HEADS
    qk_w = d_model // 2
    v_w = d_model
    if a_proj != 2 * qk_w + 2 * v_w + ng:
        raise ValueError("unexpected mLSTM projection width")
    row = lambda a: a[..., None, :].astype(F32)
    p = {
        "ffn_norm": row(ffn_norm), "mix_norm": row(mix_norm), "kv_norm": row(kv_norm),
        "final_norm": row(final_norm), "a_head_norm": row(a_head_norm),
        "ffn_wg": ffn_w_gate.astype(BF16), "ffn_wu": ffn_w_up.astype(BF16), "ffn_wd": ffn_w_down.astype(BF16),
        "a_wqk": a_w_in[:, :, :2 * qk_w].astype(BF16),
        "a_wv": a_w_in[:, :, 2 * qk_w:2 * qk_w + v_w].astype(BF16),
        "a_wo": a_w_in[:, :, 2 * qk_w + v_w:2 * qk_w + 2 * v_w].astype(BF16),
        "a_wgt": jnp.pad(a_w_in[:, :, 2 * qk_w + 2 * v_w:], ((0, 0), (0, 0), (0, LANES - ng))).astype(BF16),
        "a_bg": jnp.pad(a_b_gate, ((0, 0), (0, LANES - ng)))[:, None, :].astype(F32),
        "a_w_out": a_w_out.astype(BF16),
        "zero_bias": jnp.zeros((1, d_model), F32),
        "w_kv": w_kv.astype(BF16), "b_kv": row(b_kv),
        "b_w_q": b_w_q.astype(BF16), "b_b_q": row(b_b_q), "b_sinks": row(b_sinks),
        "b_w_o": b_w_o.astype(BF16), "b_b_o": row(b_b_o),
    }
    Bp, Sp, _ = x_prompt.shape
    Bs, Ts, _ = x_sample.shape
    zc = jnp.zeros((1, Bp) + state_mlstm_C.shape[2:], F32)
    zn = jnp.zeros((1, Bp) + state_mlstm_n.shape[2:], F32)
    zm = jnp.zeros((1, Bp) + state_mlstm_m.shape[2:] + (1,), F32)
    s_state = (state_mlstm_C, state_mlstm_n, state_mlstm_m[..., None])

    p_rows = math.lcm(Sp, _token_tile(Bp * Sp))
    p_tables = _rope_tables(jnp.arange(p_rows, dtype=jnp.int32) % Sp)
    s_rows = math.lcm(Ts, _token_tile(Bs * Ts))
    s_tables = _rope_tables(PAST_LEN + jnp.arange(s_rows, dtype=jnp.int32) % Ts)

    y_p, p_c, p_n, p_m, p_k, p_v = _trunk(x_prompt, p_tables, (zc, zn, zm), None, p, "p", BF16)
    y_s, s_c, s_n, s_m, s_k, s_v = _trunk(x_sample, s_tables, s_state, (cache_swa_k, cache_swa_v), p, "s", F32)
    return (y_p, y_s, p_c, p_n, p_m, p_k, p_v, s_c, s_n, s_m, s_k, s_v)
```

```python
import functools
import math

import jax
import jax.numpy as jnp
from jax import lax
from jax.experimental import pallas as pl
from jax.experimental.pallas import tpu as pltpu

F32 = jnp.float32
BF16 = jnp.bfloat16

EPS = 1e-6
GATE_CAP = 15.0
A_HEADS = 4
MLSTM_CHUNK = 256
B_HEAD_DIM = 64
B_KV_HEADS = 4
WINDOW = 128
ROT_DIM = B_HEAD_DIM // 4
ROPE_THETA = 500000.0
PAST_LEN = 8192

LANES = 128
V7X_VMEM_LIMIT_BYTES = 56 * 1024 * 1024
TOKEN_TILE = 512
FF_TILE = 256


def _rms(x, g):
    return (x * lax.rsqrt(jnp.mean(x * x, axis=-1, keepdims=True) + EPS)) * g


def _dot(a, b):
    return jnp.dot(a, b, preferred_element_type=F32)


def _dot_nt(a, b):
    return lax.dot_general(a, b, (((1,), (1,)), ((), ())), preferred_element_type=F32)


def _dot_f32(a, b):
    return jnp.dot(a, b, preferred_element_type=F32, precision=lax.Precision.HIGHEST)


def _at(arr, *idx):
    return (arr, idx)


def _split(param):
    return param if isinstance(param, tuple) else (param, ())


def _shape(param):
    arr, idx = _split(param)
    return arr.shape[len(idx):]


def _resident(param):
    arr, idx = _split(param)
    block = (None,) * len(idx) + arr.shape[len(idx):]
    index = idx + (0,) * (arr.ndim - len(idx))
    return arr, pl.BlockSpec(block, lambda *_: index, pipeline_mode=pl.Buffered(1))


def _operands(tiled, residents):
    pairs = list(tiled) + [_resident(r) for r in residents]
    return [a for a, _ in pairs], [sp for _, sp in pairs]


def _rows(tm, width):
    return pl.BlockSpec((tm, width), lambda i: (i, 0))


def _params(*sem):
    return pltpu.CompilerParams(dimension_semantics=sem, vmem_limit_bytes=V7X_VMEM_LIMIT_BYTES)


def _token_tile(n):
    tm = TOKEN_TILE
    while n % tm:
        tm //= 2
    return tm


def _ffn_kernel(*refs, n_x, n_hg, final, n_out, n_first, tf):
    refs = list(refs)
    x_refs = [refs.pop(0) for _ in range(n_x)]
    hg_refs = [refs.pop(0) for _ in range(n_hg)]
    if n_hg:
        wmix_ref, bmix_ref = refs.pop(0), refs.pop(0)
    gn_ref, wg_ref, wu_ref, wd_ref = refs.pop(0), refs.pop(0), refs.pop(0), refs.pop(0)
    if final:
        fg_ref = refs.pop(0)
    o_refs = [refs.pop(0) for _ in range(n_out)]
    (a_ref,) = refs
    first = pl.program_id(0) < n_first

    def current(rs, dtype):
        vals = [r[...].astype(dtype) for r in rs]
        return vals[0] if len(vals) == 1 else jnp.where(first, vals[0], vals[1])

    x = current(x_refs, F32)
    if n_hg:
        x = x + (_dot(current(hg_refs, BF16), wmix_ref[...]) + bmix_ref[...])
    h = _rms(x, gn_ref[...]).astype(BF16)
    for j in range(wg_ref.shape[1] // tf):
        sl = slice(j * tf, (j + 1) * tf)
        g = _dot(h, wg_ref[:, sl])
        u = _dot(h, wu_ref[:, sl])
        a_ref[:, sl] = ((g * jax.nn.sigmoid(g)) * u).astype(BF16)
    out = x + 0.5 * _dot(a_ref[...], wd_ref[...])
    if final:
        out = _rms(out, fg_ref[...])
    if n_out == 1:
        o_refs[0][...] = out
    else:
        @pl.when(first)
        def _():
            o_refs[0][...] = out

        @pl.when(jnp.logical_not(first))
        def _():
            o_refs[1][...] = out


def _stream_rows(tm, width, n_first):
    return (pl.BlockSpec((tm, width), lambda i: (jnp.minimum(i, n_first - 1), 0)),
            pl.BlockSpec((tm, width), lambda i: (jnp.maximum(i - n_first, 0), 0)))


def _ffn(x, rows, gn, wg, wu, wd, mix=None, final_gain=None, split_out=False, name="ffn"):
    n = sum(rows)
    d = _shape(gn)[1]
    dff = _shape(wg)[1]
    tm = _token_tile(math.gcd(*rows))
    n_first = rows[0] // tm
    tf = FF_TILE if dff % FF_TILE == 0 else dff

    def tiles(a):
        if isinstance(a, tuple):
            return list(zip(a, _stream_rows(tm, a[0].shape[1], n_first)))
        return [(a, _rows(tm, a.shape[1]))]

    tiled, residents = tiles(x), []
    n_x = len(tiled)
    n_hg = 0
    if mix is not None:
        hg, wmix, bmix = mix
        tiled += tiles(hg)
        n_hg = len(tiled) - n_x
        residents += [wmix, bmix]
    residents += [gn, wg, wu, wd]
    if final_gain is not None:
        residents.append(final_gain)
    args, specs = _operands(tiled, residents)
    if split_out:
        out_specs = list(_stream_rows(tm, d, n_first))
        out_shape = [jax.ShapeDtypeStruct((r, d), F32) for r in rows]
    else:
        out_specs, out_shape = _rows(tm, d), jax.ShapeDtypeStruct((n, d), F32)
    kern = functools.partial(_ffn_kernel, n_x=n_x, n_hg=n_hg, final=final_gain is not None,
                             n_out=2 if split_out else 1, n_first=n_first, tf=tf)
    return pl.pallas_call(
        kern,
        grid=(n // tm,),
        in_specs=specs,
        out_specs=out_specs,
        out_shape=out_shape,
        scratch_shapes=[pltpu.VMEM((tm, dff), BF16)],
        compiler_params=_params("arbitrary"),
        name=name,
    )(*args)


def _inproj_kernel(x_ref, gn_ref, wqk_ref, wv_ref, wo_ref, wgt_ref, bg_ref,
                   q_ref, k_ref, v_ref, o_ref, g_ref, *, k_scale):
    h = _rms(x_ref[...], gn_ref[...]).astype(BF16)
    dqk = q_ref.shape[1]
    qk = _dot(h, wqk_ref[...])
    q_ref[...] = qk[:, :dqk].astype(q_ref.dtype)
    k_ref[...] = (qk[:, dqk:] * k_scale).astype(k_ref.dtype)
    v_ref[...] = _dot(h, wv_ref[...]).astype(v_ref.dtype)
    o_ref[...] = _dot(h, wo_ref[...])
    gates = _dot(h, wgt_ref[...]) + bg_ref[...]
    gates = GATE_CAP * jnp.tanh(gates / GATE_CAP)
    logsig = jnp.minimum(gates, 0.0) - jnp.log1p(jnp.exp(-jnp.abs(gates)))
    lane = lax.broadcasted_iota(jnp.int32, gates.shape, 1)
    g_ref[...] = jnp.where(lane < A_HEADS, gates, logsig)


def _window_tile(row0, n):
    return _token_tile(math.gcd(row0, n))


def _window(xw):
    arr, row0, n = xw
    tm = _window_tile(row0, n)
    blk0 = row0 // tm
    return n, tm, (arr, pl.BlockSpec((tm, arr.shape[1]), lambda i: (i + blk0, 0)))


def _inproj(xw, gn, wqk, wv, wo, wgt, bg, act_dtype, name):
    n, tm, x_op = _window(xw)
    dqk = _shape(wqk)[1] // 2
    dv = _shape(wv)[1]
    kern = functools.partial(_inproj_kernel, k_scale=(dqk // A_HEADS) ** -0.5)
    args, specs = _operands([x_op], [gn, wqk, wv, wo, wgt, bg])
    return pl.pallas_call(
        kern,
        grid=(n // tm,),
        in_specs=specs,
        out_specs=[_rows(tm, dqk), _rows(tm, dqk), _rows(tm, dv), _rows(tm, dv), _rows(tm, LANES)],
        out_shape=[jax.ShapeDtypeStruct((n, dqk), act_dtype), jax.ShapeDtypeStruct((n, dqk), act_dtype),
                   jax.ShapeDtypeStruct((n, dv), act_dtype), jax.ShapeDtypeStruct((n, dv), F32),
                   jax.ShapeDtypeStruct((n, LANES), F32)],
        compiler_params=_params("parallel"),
        name=name,
    )(*args)


def _rope(x, cos, sin):
    width = x.shape[1]
    reps = width // LANES
    cosw = jnp.concatenate([cos] * reps, axis=1)
    sinw = jnp.concatenate([sin] * reps, axis=1)
    half = ROT_DIM // 2
    lane = lax.broadcasted_iota(jnp.int32, x.shape, 1)
    first_half = (lane & (B_HEAD_DIM - 1)) < half
    partner = jnp.where(first_half, pltpu.roll(x, width - half, 1), pltpu.roll(x, half, 1))
    return x * cosw + partner * sinw


def _kv_kernel(x_ref, cos_ref, sin_ref, gn_ref, w_ref, b_ref, k_ref, v_ref):
    h = _rms(x_ref[...], gn_ref[...]).astype(BF16)
    kv = _dot(h, w_ref[...]) + b_ref[...]
    kw = k_ref.shape[1]
    k_ref[...] = _rope(kv[:, :kw], cos_ref[...], sin_ref[...])
    v_ref[...] = kv[:, kw:]


def _pos_spec(tm, table):
    nblk = table.shape[0] // tm
    return pl.BlockSpec((tm, LANES), lambda i: (i % nblk, 0))


def _kv(xw, gn, w, b, cos, sin, name):
    n, tm, x_op = _window(xw)
    kw = _shape(w)[1] // 2
    args, specs = _operands([x_op, (cos, _pos_spec(tm, cos)), (sin, _pos_spec(tm, sin))], [gn, w, b])
    return pl.pallas_call(
        _kv_kernel,
        grid=(n // tm,),
        in_specs=specs,
        out_specs=[_rows(tm, kw), _rows(tm, kw)],
        out_shape=[jax.ShapeDtypeStruct((n, kw), F32), jax.ShapeDtypeStruct((n, kw), F32)],
        compiler_params=_params("parallel"),
        name=name,
    )(*args)


def _qproj_kernel(x_ref, cos_ref, sin_ref, gn_ref, w_ref, b_ref, q_ref, *, q_scale):
    h = _rms(x_ref[...], gn_ref[...]).astype(BF16)
    q = _dot(h, w_ref[...]) + b_ref[...]
    q_ref[...] = (_rope(q, cos_ref[...], sin_ref[...]) * q_scale).astype(q_ref.dtype)


def _qproj(xw, gn, w, b, cos, sin, act_dtype, name):
    n, tm, x_op = _window(xw)
    kern = functools.partial(_qproj_kernel, q_scale=B_HEAD_DIM ** -0.5)
    qw = _shape(w)[1]
    args, specs = _operands([x_op, (cos, _pos_spec(tm, cos)), (sin, _pos_spec(tm, sin))], [gn, w, b])
    return pl.pallas_call(
        kern,
        grid=(n // tm,),
        in_specs=specs,
        out_specs=_rows(tm, qw),
        out_shape=jax.ShapeDtypeStruct((n, qw), act_dtype),
        compiler_params=_params("parallel"),
        name=name,
    )(*args)


def _mlstm_kernel(q_ref, k_ref, v_ref, o_ref, gtm_ref, gt_ref, hn_ref, c0_ref, n0_ref, m0_ref, *rest,
                  bblk, chunk):
    out_ref, c_ref, n_ref, m_ref = rest[-4:]
    L = chunk
    dqk = q_ref.shape[2] // A_HEADS
    dv = v_ref.shape[2] // A_HEADS

    @pl.when(pl.program_id(1) == 0)
    def _():
        c_ref[...] = c0_ref[...]
        n_ref[...] = n0_ref[...]
        m_ref[...] = m0_ref[...]

    row = lax.broadcasted_iota(jnp.int32, (L, L), 0)
    col = lax.broadcasted_iota(jnp.int32, (L, L), 1)
    tril = col <= row
    lower_ones = tril.astype(F32)
    upper_ones = (row <= col).astype(F32)

    chains = [(b, h) for b in range(bblk) for h in range(A_HEADS)]
    per_chain = lambda f: jnp.stack([f(b, h) for b, h in chains])

    gts = [gt_ref[b, 0] for b in range(bblk)]
    gtms = [gtm_ref[b] for b in range(bblk)]
    csum_rows = [_dot_f32(gt, upper_ones) for gt in gts]
    csum_cols = [_dot_f32(lower_ones, gtm) for gtm in gtms]
    i_row = per_chain(lambda b, h: gts[b][h:h + 1, :])
    b_row = per_chain(lambda b, h: csum_rows[b][A_HEADS + h:A_HEADS + h + 1, :])
    i_col = per_chain(lambda b, h: gtms[b][:, h:h + 1])
    b_col = per_chain(lambda b, h: csum_cols[b][:, A_HEADS + h:A_HEADS + h + 1])
    b_last = b_col[:, L - 1:L, :]
    m_prev = per_chain(lambda b, h: m_ref[b, h:h + 1, :])
    n_prev = per_chain(lambda b, h: n_ref[b, h:h + 1, :])
    c_prev = per_chain(lambda b, h: c_ref[b, h])

    q = per_chain(lambda b, h: q_ref[b, :, h * dqk:(h + 1) * dqk]).astype(BF16)
    k = per_chain(lambda b, h: k_ref[b, :, h * dqk:(h + 1) * dqk]).astype(BF16)
    v = per_chain(lambda b, h: v_ref[b, :, h * dv:(h + 1) * dv]).astype(BF16)

    dmat = jnp.where(tril[None], (b_col - b_row) + i_row, -jnp.inf)
    g = b_col + m_prev
    mt = jnp.maximum(g, jnp.max(dmat, axis=-1, keepdims=True))
    w_inter = jnp.exp(g - mt)
    s = jnp.einsum("gtd,gsd->gts", q, k, preferred_element_type=F32) * jnp.exp(dmat - mt)
    num = (jnp.einsum("gts,gsv->gtv", s.astype(BF16), v, preferred_element_type=F32)
           + w_inter * jnp.einsum("gtd,gdv->gtv", q, c_prev.astype(BF16), preferred_element_type=F32))
    qn = jnp.sum(q.astype(F32) * n_prev, axis=-1, keepdims=True)
    den = jnp.sum(s, axis=-1, keepdims=True) + w_inter * qn
    hc = num / jnp.maximum(jnp.abs(den), jnp.exp(-mt))

    m_new = mt[:, L - 1:L, :]
    a_col = jnp.exp(((b_last - b_col) + i_col) - m_new)
    decay = jnp.exp((b_last + m_prev) - m_new)
    ka = k.astype(F32) * a_col
    kat = jnp.swapaxes(ka, 1, 2).astype(BF16)
    c_new = decay * c_prev + jnp.einsum("gds,gsv->gdv", kat, v, preferred_element_type=F32)
    n_new = decay * n_prev + jnp.sum(ka, axis=1, keepdims=True)

    gain = per_chain(lambda b, h: hn_ref[:, h * dv:(h + 1) * dv])
    gate = jax.nn.sigmoid(per_chain(lambda b, h: o_ref[b, :, h * dv:(h + 1) * dv]))
    res = (_rms(hc, gain) * gate).astype(out_ref.dtype)
    for gi, (b, h) in enumerate(chains):
        c_ref[b, h] = c_new[gi]
        n_ref[b, h:h + 1, :] = n_new[gi]
        m_ref[b, h:h + 1, :] = m_new[gi]
        out_ref[b, :, h * dv:(h + 1) * dv] = res[gi]


def _mlstm(q, k, v, o, gates, head_norm, state0, layer0, layer, n_layers, prev_final, bblk, out_dtype, name):
    B, S, qw = q.shape
    vw = v.shape[2]
    L = MLSTM_CHUNK if S % MLSTM_CHUNK == 0 else S
    nc = S // L
    ng = 2 * A_HEADS
    gt = jnp.swapaxes(gates[:, :, :ng].reshape(B, nc, L, ng), 2, 3)
    kern = functools.partial(_mlstm_kernel, bblk=bblk, chunk=L)
    seq = lambda w: pl.BlockSpec((bblk, L, w), lambda i, c: (i, c, 0))

    def state(a, row):
        return pl.BlockSpec((None, bblk) + a.shape[2:], lambda i, c: (row, i) + (0,) * (a.ndim - 2))

    hn_arr, hn_spec = _resident(head_norm)
    args = [q, k, v, o, gates, gt, hn_arr, *state0]
    specs = [seq(qw), seq(qw), seq(vw), seq(vw), seq(LANES),
             pl.BlockSpec((bblk, 1, ng, L), lambda i, c: (i, c, 0, 0)), hn_spec,
             *[state(a, layer0) for a in state0]]
    aliases = {}
    if prev_final is not None:
        aliases = {len(args) + t: 1 + t for t in range(len(prev_final))}
        args += list(prev_final)
        specs += [pl.BlockSpec(memory_space=pl.ANY)] * len(prev_final)
    return pl.pallas_call(
        kern,
        grid=(B // bblk, nc),
        in_specs=specs,
        out_specs=[seq(vw)] + [state(a, layer) for a in state0],
        out_shape=[jax.ShapeDtypeStruct((B, S, vw), out_dtype)]
                  + [jax.ShapeDtypeStruct((n_layers,) + a.shape[1:], F32) for a in state0],
        input_output_aliases=aliases,
        compiler_params=_params("parallel", "arbitrary"),
        name=name,
    )(*args)


def _pair_block_diag(x):
    zero = jnp.zeros_like(x)
    return jnp.concatenate([jnp.concatenate([x, zero], axis=1), jnp.concatenate([zero, x], axis=1)], axis=0)


def _swa_prompt_kernel(sinks_ref, q_ref, kp_ref, kc_ref, vp_ref, vc_ref, o_ref, *, group):
    W = q_ref.shape[1]
    hd = B_HEAD_DIM
    pairs = group // 2
    R = pairs * W
    has_prev = pl.program_id(1) > 0
    t = lax.broadcasted_iota(jnp.int32, (R, 2 * W), 0) & (W - 1)
    s = lax.broadcasted_iota(jnp.int32, (R, 2 * W), 1) & (W - 1)
    use_cur = s <= t
    visible = jnp.logical_or(use_cur, has_prev)
    row = lax.broadcasted_iota(jnp.int32, (R, 1), 0)
    first_head = lax.broadcasted_iota(jnp.int32, (R, 2 * hd), 1) < hd
    ones = jnp.ones((W, hd), BF16)
    def scores(kh):
        ksl = slice(kh * hd, (kh + 1) * hd)
        q2 = jnp.concatenate(
            [q_ref[0, :, (kh * pairs + pr) * 2 * hd:(kh * pairs + pr + 1) * 2 * hd] for pr in range(pairs)],
            axis=0).astype(BF16)
        sc_cur = _dot_nt(q2, _pair_block_diag(kc_ref[0, :, ksl].astype(BF16)))
        sc_prev = _dot_nt(q2, _pair_block_diag(kp_ref[0, :, ksl].astype(BF16)))
        return jnp.where(visible, jnp.where(use_cur, sc_cur, sc_prev), -jnp.inf)

    sc_next = scores(0)
    for kh in range(B_KV_HEADS):
        ksl = slice(kh * hd, (kh + 1) * hd)
        sc = sc_next
        if kh + 1 < B_KV_HEADS:
            sc_next = scores(kh + 1)
        probs, sink_terms = [], []
        for g in range(2):
            sink = jnp.full((R, 1), sinks_ref[0, kh * group + g], F32)
            for pr in range(1, pairs):
                sink = jnp.where(row >= pr * W, sinks_ref[0, kh * group + 2 * pr + g], sink)
            sg = sc[:, g * W:(g + 1) * W]
            mx = jnp.maximum(jnp.max(sg, axis=-1, keepdims=True), sink)
            probs.append(jnp.exp(sg - mx))
            sink_terms.append(jnp.exp(sink - mx))
        p = jnp.concatenate(probs, axis=1)
        p_cur = jnp.where(use_cur, p, 0.0).astype(BF16)
        p_prev = jnp.where(use_cur, 0.0, p).astype(BF16)
        v_cur = jnp.concatenate([_pair_block_diag(vc_ref[0, :, ksl].astype(BF16)), _pair_block_diag(ones)], axis=1)
        v_prev = jnp.concatenate([_pair_block_diag(vp_ref[0, :, ksl].astype(BF16)), _pair_block_diag(ones)], axis=1)
        acc = _dot(p_cur, v_cur) + _dot(p_prev, v_prev)
        denom = acc[:, 2 * hd:] + jnp.where(first_head, sink_terms[0], sink_terms[1])
        out = acc[:, :2 * hd] / denom
        for pr in range(pairs):
            lo = (kh * pairs + pr) * 2 * hd
            o_ref[0, :, lo:lo + 2 * hd] = out[pr * W:(pr + 1) * W, :].astype(o_ref.dtype)


def _swa_prompt(q, k, v, sinks, out_dtype, name):
    B, S, qw = q.shape
    kw = k.shape[2]
    nb = S // WINDOW
    group = qw // kw
    if group % 2 or 2 * B_HEAD_DIM != LANES:
        raise ValueError("the prompt attention kernel pairs query heads into LANES-wide slices")
    cur = lambda w: pl.BlockSpec((1, WINDOW, w), lambda b, j: (b, j, 0))
    prev = lambda w: pl.BlockSpec((1, WINDOW, w), lambda b, j: (b, jnp.maximum(j - 1, 0), 0))
    kern = functools.partial(_swa_prompt_kernel, group=group)
    return pl.pallas_call(
        kern,
        grid=(B, nb),
        in_specs=[pl.BlockSpec(memory_space=pltpu.SMEM), cur(qw), prev(kw), cur(kw), prev(kw), cur(kw)],
        out_specs=cur(qw),
        out_shape=jax.ShapeDtypeStruct((B, S, qw), out_dtype),
        compiler_params=_params("parallel", "parallel"),
        name=name,
    )(sinks, q, k, k, v, v)


def _swa_sample_kernel(sinks_ref, q_ref, kc_ref, kn_ref, vc_ref, vn_ref, o_ref, *, group, bblk):
    T = q_ref.shape[1]
    wb, kw = kc_ref.shape[1:]
    hd = B_HEAD_DIM
    nq = B_KV_HEADS * group
    R = nq * T
    qb = q_ref[...]
    blocks = []
    for kh in range(B_KV_HEADS):
        for g in range(group):
            lo = (kh * group + g) * hd
            parts = [qb[:, :, lo:lo + hd]]
            if kh:
                parts.insert(0, jnp.zeros((bblk, T, kh * hd), qb.dtype))
            if kh < B_KV_HEADS - 1:
                parts.append(jnp.zeros((bblk, T, (B_KV_HEADS - 1 - kh) * hd), qb.dtype))
            blocks.append(jnp.concatenate(parts, axis=-1))
    q_all = jnp.concatenate(blocks, axis=1).astype(BF16)

    t = lax.broadcasted_iota(jnp.int32, (R, wb), 0) % T
    i = lax.broadcasted_iota(jnp.int32, (R, wb), 1)
    mask_buf = jnp.logical_and((t + wb) - i < WINDOW, (PAST_LEN - wb) + i >= 0)
    tn = lax.broadcasted_iota(jnp.int32, (R, T), 0) % T
    u = lax.broadcasted_iota(jnp.int32, (R, T), 1)
    mask_new = u <= tn
    ridx = lax.broadcasted_iota(jnp.int32, (R, 1), 0)
    sink = jnp.full((R, 1), sinks_ref[0, 0], F32)
    for hq in range(1, nq):
        sink = jnp.where(ridx >= hq * T, sinks_ref[0, hq], sink)

    sc_buf = jnp.einsum("bqd,bkd->bqk", q_all, kc_ref[...].astype(BF16), preferred_element_type=F32)
    sc_new = jnp.einsum("bqd,bkd->bqk", q_all, kn_ref[...].astype(BF16), preferred_element_type=F32)
    sc_buf = jnp.where(mask_buf[None], sc_buf, -jnp.inf)
    sc_new = jnp.where(mask_new[None], sc_new, -jnp.inf)
    mx = jnp.maximum(jnp.maximum(jnp.max(sc_buf, axis=-1, keepdims=True),
                                 jnp.max(sc_new, axis=-1, keepdims=True)), sink[None])
    p_buf = jnp.exp(sc_buf - mx)
    p_new = jnp.exp(sc_new - mx)
    denom = (jnp.sum(p_buf, axis=-1, keepdims=True) + jnp.sum(p_new, axis=-1, keepdims=True)
             + jnp.exp(sink[None] - mx))
    acc = (jnp.einsum("bqk,bkd->bqd", p_buf.astype(BF16), vc_ref[...].astype(BF16), preferred_element_type=F32)
           + jnp.einsum("bqk,bkd->bqd", p_new.astype(BF16), vn_ref[...].astype(BF16), preferred_element_type=F32))
    out = (acc / denom).astype(o_ref.dtype)
    for kh in range(B_KV_HEADS):
        for g in range(group):
            hq = kh * group + g
            o_ref[:, :, hq * hd:(hq + 1) * hd] = out[:, hq * T:(hq + 1) * T, kh * hd:(kh + 1) * hd]


def _swa_sample(q, k_buf, k_new, v_buf, v_new, sinks, bblk, name):
    B, T, qw = q.shape
    wb, kw = k_buf.shape[1:]
    group = qw // kw
    blk = lambda r, w: pl.BlockSpec((bblk, r, w), lambda i: (i, 0, 0))
    kern = functools.partial(_swa_sample_kernel, group=group, bblk=bblk)
    return pl.pallas_call(
        kern,
        grid=(B // bblk,),
        in_specs=[pl.BlockSpec(memory_space=pltpu.SMEM), blk(T, qw), blk(wb, kw), blk(T, kw),
                  blk(wb, kw), blk(T, kw)],
        out_specs=blk(T, qw),
        out_shape=jax.ShapeDtypeStruct((B, T, qw), F32),
        compiler_params=_params("parallel"),
        name=name,
    )(sinks, q, k_buf, k_new, v_buf, v_new)


def _rope_tables(pos):
    half = ROT_DIM // 2
    inv_freq = jnp.exp(-math.log(ROPE_THETA) * jnp.arange(0, ROT_DIM, 2, dtype=F32) / ROT_DIM)
    ang = pos.astype(F32)[:, None] * inv_freq[None, :]
    d = jnp.arange(LANES, dtype=jnp.int32) % B_HEAD_DIM
    ang_l = jnp.take(ang, d % half, axis=1)
    rot = (d < ROT_DIM)[None, :]
    sign = jnp.where(d < half, -1.0, 1.0)[None, :]
    cos = jnp.where(rot, jnp.cos(ang_l), 1.0)
    sin = jnp.where(rot, jnp.sin(ang_l) * sign, 0.0)
    return cos.astype(F32), sin.astype(F32)


def _largest_divisor(n, cap):
    d = min(n, cap)
    while n % d:
        d -= 1
    return d


def _forward(streams, p):
    n_a = p["a_wqk"].shape[0]
    depth = p["ffn_norm"].shape[0]
    rows = tuple(st["x"].shape[0] * st["x"].shape[1] for st in streams)
    x = tuple(st["x"].reshape(n, -1) for st, n in zip(streams, rows))

    def window(i):
        if isinstance(x, tuple):
            return (x[i], 0, rows[i])
        return (x, sum(rows[:i]), rows[i])

    final_state = [None for _ in streams]
    kv_new = [None for _ in streams]
    for l in range(depth):
        if l == n_a:
            for i, st in enumerate(streams):
                kv_new[i] = _kv(window(i), p["kv_norm"], p["w_kv"], p["b_kv"], *st["tables"],
                                name=f"kv_{st['tag']}")
        x = _ffn(x, rows, _at(p["ffn_norm"], l, 0), _at(p["ffn_wg"], l, 0), _at(p["ffn_wu"], l, 0),
                 _at(p["ffn_wd"], l, 0), name=f"ffn_{l}a")
        mixed = []
        for i, st in enumerate(streams):
            B, T, _ = st["x"].shape
            tag, act = st["tag"], st["dtype"]
            seq = lambda a: a.reshape(B, T, a.shape[-1])
            if l < n_a:
                q, k, v, o, gates = _inproj(window(i), _at(p["mix_norm"], l), _at(p["a_wqk"], l),
                                            _at(p["a_wv"], l), _at(p["a_wo"], l), _at(p["a_wgt"], l),
                                            _at(p["a_bg"], l), act, name=f"inproj_{tag}_{l}")
                bblk = _largest_divisor(B, 2 if T >= MLSTM_CHUNK else 4)
                layer0 = l if st["state"][0].shape[0] == n_a else 0
                hg, *final_state[i] = _mlstm(seq(q), seq(k), seq(v), seq(o), seq(gates),
                                             _at(p["a_head_norm"], l), st["state"], layer0, l, n_a,
                                             final_state[i], bblk, act, name=f"mlstm_{tag}_{l}")
                mixed.append(hg.reshape(B * T, -1))
            else:
                j = l - n_a
                q = _qproj(window(i), _at(p["mix_norm"], l), _at(p["b_w_q"], j), _at(p["b_b_q"], j),
                           *st["tables"], act, name=f"qproj_{tag}_{l}")
                kn, vn = (seq(a) for a in kv_new[i])
                if st["cache"] is None:
                    o = _swa_prompt(seq(q), kn, vn, p["b_sinks"][j], act, name=f"swa_{tag}_{l}")
                else:
                    kc, vc = st["cache"]
                    o = _swa_sample(seq(q), kc.reshape(B, kc.shape[1], -1), kn, vc.reshape(B, vc.shape[1], -1),
                                    vn, p["b_sinks"][j], _largest_divisor(B, 8), name=f"swa_{tag}_{l}")
                mixed.append(o.reshape(B * T, -1))
        if l < n_a:
            mix = (tuple(mixed), _at(p["a_w_out"], l), p["zero_bias"])
        else:
            mix = (tuple(mixed), _at(p["b_w_o"], l - n_a), _at(p["b_b_o"], l - n_a))
        last = l == depth - 1
        x = _ffn(x, rows, _at(p["ffn_norm"], l, 1), _at(p["ffn_wg"], l, 1), _at(p["ffn_wu"], l, 1),
                 _at(p["ffn_wd"], l, 1), mix=mix, final_gain=p["final_norm"] if last else None,
                 split_out=last, name=f"ffn_{l}b")

    results = []
    for i, st in enumerate(streams):
        B, T, D = st["x"].shape
        new_c, new_n, new_m = final_state[i]
        k_new, v_new = (a.reshape(B, T, -1) for a in kv_new[i])
        if st["cache"] is None:
            wb = min(WINDOW, T)
            buf_k, buf_v = k_new[:, -wb:], v_new[:, -wb:]
        else:
            wb = st["cache"][0].shape[1]
            flat = lambda a: a.reshape(B, a.shape[1], -1)
            buf_k = jnp.concatenate([flat(st["cache"][0]), k_new], axis=1)[:, -wb:]
            buf_v = jnp.concatenate([flat(st["cache"][1]), v_new], axis=1)[:, -wb:]
        heads = lambda a: a.reshape(B, wb, B_KV_HEADS, B_HEAD_DIM)
        results.append((x[i].reshape(B, T, D), new_c, new_n, new_m[..., 0], heads(buf_k), heads(buf_v)))
    return results


def kernel(x_prompt, x_sample, state_mlstm_C, state_mlstm_n, state_mlstm_m, cache_swa_k, cache_swa_v,
           ffn_norm, ffn_w_gate, ffn_w_up, ffn_w_down, mix_norm, a_w_in, a_b_gate, a_head_norm, a_w_out,
           kv_norm, w_kv, b_kv, b_w_q, b_b_q, b_sinks, b_w_o, b_b_o, final_norm):
    n_a, d_model, a_proj = a_w_in.shape
    ng = 2 * A_HEADS
    qk_w = d_model // 2
    v_w = d_model
    if a_proj != 2 * qk_w + 2 * v_w + ng:
        raise ValueError("unexpected mLSTM projection width")
    row = lambda a: a[..., None, :].astype(F32)
    p = {
        "ffn_norm": row(ffn_norm), "mix_norm": row(mix_norm), "kv_norm": row(kv_norm),
        "final_norm": row(final_norm), "a_head_norm": row(a_head_norm),
        "ffn_wg": ffn_w_gate.astype(BF16), "ffn_wu": ffn_w_up.astype(BF16), "ffn_wd": ffn_w_down.astype(BF16),
        "a_wqk": a_w_in[:, :, :2 * qk_w].astype(BF16),
        "a_wv": a_w_in[:, :, 2 * qk_w:2 * qk_w + v_w].astype(BF16),
        "a_wo": a_w_in[:, :, 2 * qk_w + v_w:2 * qk_w + 2 * v_w].astype(BF16),
        "a_wgt": jnp.pad(a_w_in[:, :, 2 * qk_w + 2 * v_w:], ((0, 0), (0, 0), (0, LANES - ng))).astype(BF16),
        "a_bg": jnp.pad(a_b_gate, ((0, 0), (0, LANES - ng)))[:, None, :].astype(F32),
        "a_w_out": a_w_out.astype(BF16),
        "zero_bias": jnp.zeros((1, d_model), F32),
        "w_kv": w_kv.astype(BF16), "b_kv": row(b_kv),
        "b_w_q": b_w_q.astype(BF16), "b_b_q": row(b_b_q), "b_sinks": row(b_sinks),
        "b_w_o": b_w_o.astype(BF16), "b_b_o": row(b_b_o),
    }
    Bp, Sp, _ = x_prompt.shape
    Bs, Ts, _ = x_sample.shape
    zc = jnp.zeros((1, Bp) + state_mlstm_C.shape[2:], F32)
    zn = jnp.zeros((1, Bp) + state_mlstm_n.shape[2:], F32)
    zm = jnp.zeros((1, Bp) + state_mlstm_m.shape[2:] + (1,), F32)
    s_state = (state_mlstm_C, state_mlstm_n, state_mlstm_m[..., None])

    p_rows = math.lcm(Sp, _window_tile(0, Bp * Sp))
    p_tables = _rope_tables(jnp.arange(p_rows, dtype=jnp.int32) % Sp)
    s_rows = math.lcm(Ts, _window_tile(Bp * Sp, Bs * Ts))
    s_tables = _rope_tables(PAST_LEN + jnp.arange(s_rows, dtype=jnp.int32) % Ts)

    prompt = dict(x=x_prompt, tables=p_tables, state=(zc, zn, zm), cache=None, tag="p", dtype=BF16)
    sample = dict(x=x_sample, tables=s_tables, state=s_state, cache=(cache_swa_k, cache_swa_v), tag="s",
                  dtype=F32)
    (y_p, p_c, p_n, p_m, p_k, p_v), (y_s, s_c, s_n, s_m, s_k, s_v) = _forward([prompt, sample], p)
    return (y_p, y_s, p_c, p_n, p_m, p_k, p_v, s_c, s_n, s_m, s_k, s_v)
```

```python
import functools
import math

import jax
import jax.numpy as jnp
from jax import lax
from jax.experimental import pallas as pl
from jax.experimental.pallas import tpu as pltpu

F32 = jnp.float32
BF16 = jnp.bfloat16

EPS = 1e-6
GATE_CAP = 15.0
A_HEADS = 4
MLSTM_CHUNK = 256
B_HEAD_DIM = 64
B_KV_HEADS = 4
WINDOW = 128
ROT_DIM = B_HEAD_DIM // 4
ROPE_THETA = 500000.0
PAST_LEN = 8192

LANES = 128
V7X_VMEM_LIMIT_BYTES = 56 * 1024 * 1024
TOKEN_TILE = 512
FF_TILE = 256


def _rms(x, g):
    return (x * lax.rsqrt(jnp.mean(x * x, axis=-1, keepdims=True) + EPS)) * g


def _dot(a, b):
    return jnp.dot(a, b, preferred_element_type=F32)


def _dot_nt(a, b):
    return lax.dot_general(a, b, (((1,), (1,)), ((), ())), preferred_element_type=F32)


def _dot_f32(a, b):
    return jnp.dot(a, b, preferred_element_type=F32, precision=lax.Precision.HIGHEST)


def _at(arr, *idx):
    return (arr, idx)


def _split(param):
    return param if isinstance(param, tuple) else (param, ())


def _shape(param):
    arr, idx = _split(param)
    return arr.shape[len(idx):]


def _resident(param):
    arr, idx = _split(param)
    block = (None,) * len(idx) + arr.shape[len(idx):]
    index = idx + (0,) * (arr.ndim - len(idx))
    return arr, pl.BlockSpec(block, lambda *_: index, pipeline_mode=pl.Buffered(1))


def _operands(tiled, residents):
    pairs = list(tiled) + [_resident(r) for r in residents]
    return [a for a, _ in pairs], [sp for _, sp in pairs]


def _rows(tm, width):
    return pl.BlockSpec((tm, width), lambda i: (i, 0))


def _params(*sem):
    return pltpu.CompilerParams(dimension_semantics=sem, vmem_limit_bytes=V7X_VMEM_LIMIT_BYTES)


def _token_tile(n):
    tm = TOKEN_TILE
    while n % tm:
        tm //= 2
    return tm


def _ffn_kernel(*refs, n_x, n_hg, final, n_out, n_first, tf):
    refs = list(refs)
    x_refs = [refs.pop(0) for _ in range(n_x)]
    hg_refs = [refs.pop(0) for _ in range(n_hg)]
    if n_hg:
        wmix_ref, bmix_ref = refs.pop(0), refs.pop(0)
    gn_ref, wg_ref, wu_ref, wd_ref = refs.pop(0), refs.pop(0), refs.pop(0), refs.pop(0)
    if final:
        fg_ref = refs.pop(0)
    o_refs = [refs.pop(0) for _ in range(n_out)]
    (a_ref,) = refs
    first = pl.program_id(0) < n_first

    def current(rs, dtype):
        vals = [r[...].astype(dtype) for r in rs]
        return vals[0] if len(vals) == 1 else jnp.where(first, vals[0], vals[1])

    x = current(x_refs, F32)
    if n_hg:
        x = x + (_dot(current(hg_refs, BF16), wmix_ref[...]) + bmix_ref[...])
    h = _rms(x, gn_ref[...]).astype(BF16)
    for j in range(wg_ref.shape[1] // tf):
        sl = slice(j * tf, (j + 1) * tf)
        g = _dot(h, wg_ref[:, sl])
        u = _dot(h, wu_ref[:, sl])
        a_ref[:, sl] = ((g * jax.nn.sigmoid(g)) * u).astype(BF16)
    out = x + 0.5 * _dot(a_ref[...], wd_ref[...])
    if final:
        out = _rms(out, fg_ref[...])
    if n_out == 1:
        o_refs[0][...] = out
    else:
        @pl.when(first)
        def _():
            o_refs[0][...] = out

        @pl.when(jnp.logical_not(first))
        def _():
            o_refs[1][...] = out


def _stream_rows(tm, width, n_first):
    return (pl.BlockSpec((tm, width), lambda i: (jnp.minimum(i, n_first - 1), 0)),
            pl.BlockSpec((tm, width), lambda i: (jnp.maximum(i - n_first, 0), 0)))


def _ffn(x, rows, gn, wg, wu, wd, mix=None, final_gain=None, split_out=False, name="ffn"):
    n = sum(rows)
    d = _shape(gn)[1]
    dff = _shape(wg)[1]
    tf = FF_TILE if dff % FF_TILE == 0 else dff
    residents = ([mix[1], mix[2]] if mix is not None else []) + [gn, wg, wu, wd]
    if final_gain is not None:
        residents.append(final_gain)
    streamed = [x] + ([mix[0]] if mix is not None else [])
    streamed = [a for s in streamed for a in (s if isinstance(s, tuple) else (s,))]

    nbytes = lambda shape, dtype: math.prod(shape) * jnp.dtype(dtype).itemsize
    fixed = sum(nbytes(_shape(r), _split(r)[0].dtype) for r in residents)
    per_row = (2 * sum(nbytes(a.shape[1:], a.dtype) for a in streamed)
               + 2 * (2 if split_out else 1) * d * 4 + dff * 2 + d * 4)
    tm = 2 * TOKEN_TILE
    while math.gcd(*rows) % tm or fixed + tm * per_row > V7X_VMEM_LIMIT_BYTES:
        tm //= 2
    n_first = rows[0] // tm

    def tiles(a):
        if isinstance(a, tuple):
            return list(zip(a, _stream_rows(tm, a[0].shape[1], n_first)))
        return [(a, _rows(tm, a.shape[1]))]

    tiled = tiles(x)
    n_x = len(tiled)
    n_hg = 0
    if mix is not None:
        tiled += tiles(mix[0])
        n_hg = len(tiled) - n_x
    args, specs = _operands(tiled, residents)
    if split_out:
        out_specs = list(_stream_rows(tm, d, n_first))
        out_shape = [jax.ShapeDtypeStruct((r, d), F32) for r in rows]
    else:
        out_specs, out_shape = _rows(tm, d), jax.ShapeDtypeStruct((n, d), F32)
    kern = functools.partial(_ffn_kernel, n_x=n_x, n_hg=n_hg, final=final_gain is not None,
                             n_out=2 if split_out else 1, n_first=n_first, tf=tf)
    return pl.pallas_call(
        kern,
        grid=(n // tm,),
        in_specs=specs,
        out_specs=out_specs,
        out_shape=out_shape,
        scratch_shapes=[pltpu.VMEM((tm, dff), BF16)],
        compiler_params=_params("arbitrary"),
        name=name,
    )(*args)


def _inproj_kernel(x_ref, gn_ref, wqk_ref, wv_ref, wo_ref, wgt_ref, bg_ref,
                   q_ref, k_ref, v_ref, o_ref, g_ref, *, k_scale):
    h = _rms(x_ref[...], gn_ref[...]).astype(BF16)
    dqk = q_ref.shape[1]
    qk = _dot(h, wqk_ref[...])
    q_ref[...] = qk[:, :dqk].astype(q_ref.dtype)
    k_ref[...] = (qk[:, dqk:] * k_scale).astype(k_ref.dtype)
    v_ref[...] = _dot(h, wv_ref[...]).astype(v_ref.dtype)
    o_ref[...] = _dot(h, wo_ref[...])
    gates = _dot(h, wgt_ref[...]) + bg_ref[...]
    gates = GATE_CAP * jnp.tanh(gates / GATE_CAP)
    logsig = jnp.minimum(gates, 0.0) - jnp.log1p(jnp.exp(-jnp.abs(gates)))
    lane = lax.broadcasted_iota(jnp.int32, gates.shape, 1)
    g_ref[...] = jnp.where(lane < A_HEADS, gates, logsig)


def _window_tile(row0, n):
    return _token_tile(math.gcd(row0, n))


def _window(xw):
    arr, row0, n = xw
    tm = _window_tile(row0, n)
    blk0 = row0 // tm
    return n, tm, (arr, pl.BlockSpec((tm, arr.shape[1]), lambda i: (i + blk0, 0)))


def _inproj(xw, gn, wqk, wv, wo, wgt, bg, act_dtype, name):
    n, tm, x_op = _window(xw)
    dqk = _shape(wqk)[1] // 2
    dv = _shape(wv)[1]
    kern = functools.partial(_inproj_kernel, k_scale=(dqk // A_HEADS) ** -0.5)
    args, specs = _operands([x_op], [gn, wqk, wv, wo, wgt, bg])
    return pl.pallas_call(
        kern,
        grid=(n // tm,),
        in_specs=specs,
        out_specs=[_rows(tm, dqk), _rows(tm, dqk), _rows(tm, dv), _rows(tm, dv), _rows(tm, LANES)],
        out_shape=[jax.ShapeDtypeStruct((n, dqk), act_dtype), jax.ShapeDtypeStruct((n, dqk), act_dtype),
                   jax.ShapeDtypeStruct((n, dv), act_dtype), jax.ShapeDtypeStruct((n, dv), F32),
                   jax.ShapeDtypeStruct((n, LANES), F32)],
        compiler_params=_params("parallel"),
        name=name,
    )(*args)


def _rope(x, cos, sin):
    width = x.shape[1]
    reps = width // LANES
    cosw = jnp.concatenate([cos] * reps, axis=1)
    sinw = jnp.concatenate([sin] * reps, axis=1)
    half = ROT_DIM // 2
    lane = lax.broadcasted_iota(jnp.int32, x.shape, 1)
    first_half = (lane & (B_HEAD_DIM - 1)) < half
    partner = jnp.where(first_half, pltpu.roll(x, width - half, 1), pltpu.roll(x, half, 1))
    return x * cosw + partner * sinw


def _kv_kernel(x_ref, cos_ref, sin_ref, gn_ref, w_ref, b_ref, k_ref, v_ref):
    h = _rms(x_ref[...], gn_ref[...]).astype(BF16)
    kv = _dot(h, w_ref[...]) + b_ref[...]
    kw = k_ref.shape[1]
    k_ref[...] = _rope(kv[:, :kw], cos_ref[...], sin_ref[...])
    v_ref[...] = kv[:, kw:]


def _pos_spec(tm, table):
    nblk = table.shape[0] // tm
    return pl.BlockSpec((tm, LANES), lambda i: (i % nblk, 0))


def _kv(xw, gn, w, b, cos, sin, name):
    n, tm, x_op = _window(xw)
    kw = _shape(w)[1] // 2
    args, specs = _operands([x_op, (cos, _pos_spec(tm, cos)), (sin, _pos_spec(tm, sin))], [gn, w, b])
    return pl.pallas_call(
        _kv_kernel,
        grid=(n // tm,),
        in_specs=specs,
        out_specs=[_rows(tm, kw), _rows(tm, kw)],
        out_shape=[jax.ShapeDtypeStruct((n, kw), F32), jax.ShapeDtypeStruct((n, kw), F32)],
        compiler_params=_params("parallel"),
        name=name,
    )(*args)


def _qproj_kernel(x_ref, cos_ref, sin_ref, gn_ref, w_ref, b_ref, q_ref, *, q_scale):
    h = _rms(x_ref[...], gn_ref[...]).astype(BF16)
    q = _dot(h, w_ref[...]) + b_ref[...]
    q_ref[...] = (_rope(q, cos_ref[...], sin_ref[...]) * q_scale).astype(q_ref.dtype)


def _qproj(xw, gn, w, b, cos, sin, act_dtype, name):
    n, tm, x_op = _window(xw)
    kern = functools.partial(_qproj_kernel, q_scale=B_HEAD_DIM ** -0.5)
    qw = _shape(w)[1]
    args, specs = _operands([x_op, (cos, _pos_spec(tm, cos)), (sin, _pos_spec(tm, sin))], [gn, w, b])
    return pl.pallas_call(
        kern,
        grid=(n // tm,),
        in_specs=specs,
        out_specs=_rows(tm, qw),
        out_shape=jax.ShapeDtypeStruct((n, qw), act_dtype),
        compiler_params=_params("parallel"),
        name=name,
    )(*args)


def _mlstm_kernel(q_ref, k_ref, v_ref, o_ref, gtm_ref, gt_ref, hn_ref, c0_ref, n0_ref, m0_ref, *rest,
                  bblk, chunk):
    out_ref, c_ref, n_ref, m_ref = rest[-4:]
    L = chunk
    dqk = q_ref.shape[2] // A_HEADS
    dv = v_ref.shape[2] // A_HEADS

    @pl.when(pl.program_id(1) == 0)
    def _():
        c_ref[...] = c0_ref[...]
        n_ref[...] = n0_ref[...]
        m_ref[...] = m0_ref[...]

    row = lax.broadcasted_iota(jnp.int32, (L, L), 0)
    col = lax.broadcasted_iota(jnp.int32, (L, L), 1)
    tril = col <= row
    lower_ones = tril.astype(F32)
    upper_ones = (row <= col).astype(F32)

    chains = [(b, h) for b in range(bblk) for h in range(A_HEADS)]
    per_chain = lambda f: jnp.stack([f(b, h) for b, h in chains])

    gts = [gt_ref[b, 0] for b in range(bblk)]
    gtms = [gtm_ref[b] for b in range(bblk)]
    csum_rows = [_dot_f32(gt, upper_ones) for gt in gts]
    csum_cols = [_dot_f32(lower_ones, gtm) for gtm in gtms]
    i_row = per_chain(lambda b, h: gts[b][h:h + 1, :])
    b_row = per_chain(lambda b, h: csum_rows[b][A_HEADS + h:A_HEADS + h + 1, :])
    i_col = per_chain(lambda b, h: gtms[b][:, h:h + 1])
    b_col = per_chain(lambda b, h: csum_cols[b][:, A_HEADS + h:A_HEADS + h + 1])
    b_last = b_col[:, L - 1:L, :]
    m_prev = per_chain(lambda b, h: m_ref[b, h:h + 1, :])
    n_prev = per_chain(lambda b, h: n_ref[b, h:h + 1, :])
    c_prev = per_chain(lambda b, h: c_ref[b, h])

    q = per_chain(lambda b, h: q_ref[b, :, h * dqk:(h + 1) * dqk]).astype(BF16)
    k = per_chain(lambda b, h: k_ref[b, :, h * dqk:(h + 1) * dqk]).astype(BF16)
    v = per_chain(lambda b, h: v_ref[b, :, h * dv:(h + 1) * dv]).astype(BF16)

    dmat = jnp.where(tril[None], (b_col - b_row) + i_row, -jnp.inf)
    g = b_col + m_prev
    mt = jnp.maximum(g, jnp.max(dmat, axis=-1, keepdims=True))
    w_inter = jnp.exp(g - mt)
    s = jnp.einsum("gtd,gsd->gts", q, k, preferred_element_type=F32) * jnp.exp(dmat - mt)
    G = len(chains)
    v_ones = jnp.concatenate([v, jnp.ones((G, L, LANES), BF16)], axis=-1)
    n_cols = jnp.broadcast_to(jnp.swapaxes(n_prev, 1, 2), (G, dqk, LANES))
    c_n = jnp.concatenate([c_prev, n_cols], axis=-1).astype(BF16)
    intra = jnp.einsum("gts,gsv->gtv", s.astype(BF16), v_ones, preferred_element_type=F32)
    inter = w_inter * jnp.einsum("gtd,gdv->gtv", q, c_n, preferred_element_type=F32)
    both = intra + inter
    den = jnp.maximum(jnp.abs(both[:, :, dv:]), jnp.exp(-mt))
    hc = both[:, :, :dv] / jnp.concatenate([den] * (dv // LANES), axis=-1)

    m_new = mt[:, L - 1:L, :]
    a_col = jnp.exp(((b_last - b_col) + i_col) - m_new)
    decay = jnp.exp((b_last + m_prev) - m_new)
    ka = k.astype(F32) * a_col
    kat = jnp.swapaxes(ka, 1, 2).astype(BF16)
    c_new = decay * c_prev + jnp.einsum("gds,gsv->gdv", kat, v, preferred_element_type=F32)
    n_new = decay * n_prev + jnp.sum(ka, axis=1, keepdims=True)

    gain = per_chain(lambda b, h: hn_ref[:, h * dv:(h + 1) * dv])
    gate = jax.nn.sigmoid(per_chain(lambda b, h: o_ref[b, :, h * dv:(h + 1) * dv]))
    res = (_rms(hc, gain) * gate).astype(out_ref.dtype)
    for gi, (b, h) in enumerate(chains):
        c_ref[b, h] = c_new[gi]
        n_ref[b, h:h + 1, :] = n_new[gi]
        m_ref[b, h:h + 1, :] = m_new[gi]
        out_ref[b, :, h * dv:(h + 1) * dv] = res[gi]


def _mlstm(q, k, v, o, gates, head_norm, state0, layer0, layer, n_layers, prev_final, bblk, out_dtype, name):
    B, S, qw = q.shape
    vw = v.shape[2]
    L = MLSTM_CHUNK if S % MLSTM_CHUNK == 0 else S
    nc = S // L
    ng = 2 * A_HEADS
    gt = jnp.swapaxes(gates[:, :, :ng].reshape(B, nc, L, ng), 2, 3)
    kern = functools.partial(_mlstm_kernel, bblk=bblk, chunk=L)
    seq = lambda w: pl.BlockSpec((bblk, L, w), lambda i, c: (i, c, 0))

    def state(a, row):
        return pl.BlockSpec((None, bblk) + a.shape[2:], lambda i, c: (row, i) + (0,) * (a.ndim - 2))

    hn_arr, hn_spec = _resident(head_norm)
    args = [q, k, v, o, gates, gt, hn_arr, *state0]
    specs = [seq(qw), seq(qw), seq(vw), seq(vw), seq(LANES),
             pl.BlockSpec((bblk, 1, ng, L), lambda i, c: (i, c, 0, 0)), hn_spec,
             *[state(a, layer0) for a in state0]]
    aliases = {}
    if prev_final is not None:
        aliases = {len(args) + t: 1 + t for t in range(len(prev_final))}
        args += list(prev_final)
        specs += [pl.BlockSpec(memory_space=pl.ANY)] * len(prev_final)
    return pl.pallas_call(
        kern,
        grid=(B // bblk, nc),
        in_specs=specs,
        out_specs=[seq(vw)] + [state(a, layer) for a in state0],
        out_shape=[jax.ShapeDtypeStruct((B, S, vw), out_dtype)]
                  + [jax.ShapeDtypeStruct((n_layers,) + a.shape[1:], F32) for a in state0],
        input_output_aliases=aliases,
        compiler_params=_params("parallel", "arbitrary"),
        name=name,
    )(*args)


def _pair_block_diag(x):
    zero = jnp.zeros_like(x)
    return jnp.concatenate([jnp.concatenate([x, zero], axis=1), jnp.concatenate([zero, x], axis=1)], axis=0)


def _swa_prompt_kernel(sinks_ref, q_ref, kp_ref, kc_ref, vp_ref, vc_ref, o_ref, *, group):
    W = q_ref.shape[1]
    hd = B_HEAD_DIM
    pairs = group // 2
    R = pairs * W
    has_prev = pl.program_id(1) > 0
    t = lax.broadcasted_iota(jnp.int32, (R, 2 * W), 0) & (W - 1)
    s = lax.broadcasted_iota(jnp.int32, (R, 2 * W), 1) & (W - 1)
    use_cur = s <= t
    visible = jnp.logical_or(use_cur, has_prev)
    row = lax.broadcasted_iota(jnp.int32, (R, 1), 0)
    first_head = lax.broadcasted_iota(jnp.int32, (R, 2 * hd), 1) < hd
    ones = jnp.ones((W, hd), BF16)
    def scores(kh):
        ksl = slice(kh * hd, (kh + 1) * hd)
        q2 = jnp.concatenate(
            [q_ref[0, :, (kh * pairs + pr) * 2 * hd:(kh * pairs + pr + 1) * 2 * hd] for pr in range(pairs)],
            axis=0).astype(BF16)
        sc_cur = _dot_nt(q2, _pair_block_diag(kc_ref[0, :, ksl].astype(BF16)))
        sc_prev = _dot_nt(q2, _pair_block_diag(kp_ref[0, :, ksl].astype(BF16)))
        return jnp.where(visible, jnp.where(use_cur, sc_cur, sc_prev), -jnp.inf)

    sc_next = scores(0)
    for kh in range(B_KV_HEADS):
        ksl = slice(kh * hd, (kh + 1) * hd)
        sc = sc_next
        if kh + 1 < B_KV_HEADS:
            sc_next = scores(kh + 1)
        probs, sink_terms = [], []
        for g in range(2):
            sink = jnp.full((R, 1), sinks_ref[0, kh * group + g], F32)
            for pr in range(1, pairs):
                sink = jnp.where(row >= pr * W, sinks_ref[0, kh * group + 2 * pr + g], sink)
            sg = sc[:, g * W:(g + 1) * W]
            mx = jnp.maximum(jnp.max(sg, axis=-1, keepdims=True), sink)
            probs.append(jnp.exp(sg - mx))
            sink_terms.append(jnp.exp(sink - mx))
        p = jnp.concatenate(probs, axis=1)
        p_cur = jnp.where(use_cur, p, 0.0).astype(BF16)
        p_prev = jnp.where(use_cur, 0.0, p).astype(BF16)
        v_cur = jnp.concatenate([_pair_block_diag(vc_ref[0, :, ksl].astype(BF16)), _pair_block_diag(ones)], axis=1)
        v_prev = jnp.concatenate([_pair_block_diag(vp_ref[0, :, ksl].astype(BF16)), _pair_block_diag(ones)], axis=1)
        acc = _dot(p_cur, v_cur) + _dot(p_prev, v_prev)
        denom = acc[:, 2 * hd:] + jnp.where(first_head, sink_terms[0], sink_terms[1])
        out = acc[:, :2 * hd] / denom
        for pr in range(pairs):
            lo = (kh * pairs + pr) * 2 * hd
            o_ref[0, :, lo:lo + 2 * hd] = out[pr * W:(pr + 1) * W, :].astype(o_ref.dtype)


def _swa_prompt(q, k, v, sinks, out_dtype, name):
    B, S, qw = q.shape
    kw = k.shape[2]
    nb = S // WINDOW
    group = qw // kw
    if group % 2 or 2 * B_HEAD_DIM != LANES:
        raise ValueError("the prompt attention kernel pairs query heads into LANES-wide slices")
    cur = lambda w: pl.BlockSpec((1, WINDOW, w), lambda b, j: (b, j, 0))
    prev = lambda w: pl.BlockSpec((1, WINDOW, w), lambda b, j: (b, jnp.maximum(j - 1, 0), 0))
    kern = functools.partial(_swa_prompt_kernel, group=group)
    return pl.pallas_call(
        kern,
        grid=(B, nb),
        in_specs=[pl.BlockSpec(memory_space=pltpu.SMEM), cur(qw), prev(kw), cur(kw), prev(kw), cur(kw)],
        out_specs=cur(qw),
        out_shape=jax.ShapeDtypeStruct((B, S, qw), out_dtype),
        compiler_params=_params("parallel", "parallel"),
        name=name,
    )(sinks, q, k, k, v, v)


def _swa_sample_kernel(sinks_ref, q_ref, kc_ref, kn_ref, vc_ref, vn_ref, o_ref, *, group, bblk):
    T = q_ref.shape[1]
    wb, kw = kc_ref.shape[1:]
    hd = B_HEAD_DIM
    nq = B_KV_HEADS * group
    R = nq * T
    qb = q_ref[...]
    blocks = []
    for kh in range(B_KV_HEADS):
        for g in range(group):
            lo = (kh * group + g) * hd
            parts = [qb[:, :, lo:lo + hd]]
            if kh:
                parts.insert(0, jnp.zeros((bblk, T, kh * hd), qb.dtype))
            if kh < B_KV_HEADS - 1:
                parts.append(jnp.zeros((bblk, T, (B_KV_HEADS - 1 - kh) * hd), qb.dtype))
            blocks.append(jnp.concatenate(parts, axis=-1))
    q_all = jnp.concatenate(blocks, axis=1).astype(BF16)

    t = lax.broadcasted_iota(jnp.int32, (R, wb), 0) % T
    i = lax.broadcasted_iota(jnp.int32, (R, wb), 1)
    mask_buf = jnp.logical_and((t + wb) - i < WINDOW, (PAST_LEN - wb) + i >= 0)
    tn = lax.broadcasted_iota(jnp.int32, (R, T), 0) % T
    u = lax.broadcasted_iota(jnp.int32, (R, T), 1)
    mask_new = u <= tn
    ridx = lax.broadcasted_iota(jnp.int32, (R, 1), 0)
    sink = jnp.full((R, 1), sinks_ref[0, 0], F32)
    for hq in range(1, nq):
        sink = jnp.where(ridx >= hq * T, sinks_ref[0, hq], sink)

    sc_buf = jnp.einsum("bqd,bkd->bqk", q_all, kc_ref[...].astype(BF16), preferred_element_type=F32)
    sc_new = jnp.einsum("bqd,bkd->bqk", q_all, kn_ref[...].astype(BF16), preferred_element_type=F32)
    sc_buf = jnp.where(mask_buf[None], sc_buf, -jnp.inf)
    sc_new = jnp.where(mask_new[None], sc_new, -jnp.inf)
    mx = jnp.maximum(jnp.maximum(jnp.max(sc_buf, axis=-1, keepdims=True),
                                 jnp.max(sc_new, axis=-1, keepdims=True)), sink[None])
    p_buf = jnp.exp(sc_buf - mx)
    p_new = jnp.exp(sc_new - mx)
    denom = (jnp.sum(p_buf, axis=-1, keepdims=True) + jnp.sum(p_new, axis=-1, keepdims=True)
             + jnp.exp(sink[None] - mx))
    acc = (jnp.einsum("bqk,bkd->bqd", p_buf.astype(BF16), vc_ref[...].astype(BF16), preferred_element_type=F32)
           + jnp.einsum("bqk,bkd->bqd", p_new.astype(BF16), vn_ref[...].astype(BF16), preferred_element_type=F32))
    out = (acc / denom).astype(o_ref.dtype)
    for kh in range(B_KV_HEADS):
        for g in range(group):
            hq = kh * group + g
            o_ref[:, :, hq * hd:(hq + 1) * hd] = out[:, hq * T:(hq + 1) * T, kh * hd:(kh + 1) * hd]


def _swa_sample(q, k_buf, k_new, v_buf, v_new, sinks, bblk, name):
    B, T, qw = q.shape
    wb, kw = k_buf.shape[1:]
    group = qw // kw
    blk = lambda r, w: pl.BlockSpec((bblk, r, w), lambda i: (i, 0, 0))
    kern = functools.partial(_swa_sample_kernel, group=group, bblk=bblk)
    return pl.pallas_call(
        kern,
        grid=(B // bblk,),
        in_specs=[pl.BlockSpec(memory_space=pltpu.SMEM), blk(T, qw), blk(wb, kw), blk(T, kw),
                  blk(wb, kw), blk(T, kw)],
        out_specs=blk(T, qw),
        out_shape=jax.ShapeDtypeStruct((B, T, qw), BF16),
        compiler_params=_params("parallel"),
        name=name,
    )(sinks, q, k_buf, k_new, v_buf, v_new)


def _rope_tables(pos):
    half = ROT_DIM // 2
    inv_freq = jnp.exp(-math.log(ROPE_THETA) * jnp.arange(0, ROT_DIM, 2, dtype=F32) / ROT_DIM)
    ang = pos.astype(F32)[:, None] * inv_freq[None, :]
    d = jnp.arange(LANES, dtype=jnp.int32) % B_HEAD_DIM
    ang_l = jnp.take(ang, d % half, axis=1)
    rot = (d < ROT_DIM)[None, :]
    sign = jnp.where(d < half, -1.0, 1.0)[None, :]
    cos = jnp.where(rot, jnp.cos(ang_l), 1.0)
    sin = jnp.where(rot, jnp.sin(ang_l) * sign, 0.0)
    return cos.astype(F32), sin.astype(F32)


def _largest_divisor(n, cap):
    d = min(n, cap)
    while n % d:
        d -= 1
    return d


def _forward(streams, p):
    n_a = p["a_wqk"].shape[0]
    depth = p["ffn_norm"].shape[0]
    rows = tuple(st["x"].shape[0] * st["x"].shape[1] for st in streams)
    x = tuple(st["x"].reshape(n, -1) for st, n in zip(streams, rows))

    def window(i):
        if isinstance(x, tuple):
            return (x[i], 0, rows[i])
        return (x, sum(rows[:i]), rows[i])

    final_state = [None for _ in streams]
    kv_new = [None for _ in streams]
    for l in range(depth):
        if l == n_a:
            for i, st in enumerate(streams):
                kv_new[i] = _kv(window(i), p["kv_norm"], p["w_kv"], p["b_kv"], *st["tables"],
                                name=f"kv_{st['tag']}")
        x = _ffn(x, rows, _at(p["ffn_norm"], l, 0), _at(p["ffn_wg"], l, 0), _at(p["ffn_wu"], l, 0),
                 _at(p["ffn_wd"], l, 0), name=f"ffn_{l}a")
        mixed = []
        for i, st in enumerate(streams):
            B, T, _ = st["x"].shape
            tag, act = st["tag"], st["dtype"]
            seq = lambda a: a.reshape(B, T, a.shape[-1])
            if l < n_a:
                q, k, v, o, gates = _inproj(window(i), _at(p["mix_norm"], l), _at(p["a_wqk"], l),
                                            _at(p["a_wv"], l), _at(p["a_wo"], l), _at(p["a_wgt"], l),
                                            _at(p["a_bg"], l), act, name=f"inproj_{tag}_{l}")
                bblk = _largest_divisor(B, 2 if T >= MLSTM_CHUNK else 4)
                layer0 = l if st["state"][0].shape[0] == n_a else 0
                hg, *final_state[i] = _mlstm(seq(q), seq(k), seq(v), seq(o), seq(gates),
                                             _at(p["a_head_norm"], l), st["state"], layer0, l, n_a,
                                             final_state[i], bblk, BF16, name=f"mlstm_{tag}_{l}")
                mixed.append(hg.reshape(B * T, -1))
            else:
                j = l - n_a
                q = _qproj(window(i), _at(p["mix_norm"], l), _at(p["b_w_q"], j), _at(p["b_b_q"], j),
                           *st["tables"], act, name=f"qproj_{tag}_{l}")
                kn, vn = (seq(a) for a in kv_new[i])
                if st["cache"] is None:
                    o = _swa_prompt(seq(q), kn, vn, p["b_sinks"][j], BF16, name=f"swa_{tag}_{l}")
                else:
                    kc, vc = st["cache"]
                    o = _swa_sample(seq(q), kc.reshape(B, kc.shape[1], -1), kn, vc.reshape(B, vc.shape[1], -1),
                                    vn, p["b_sinks"][j], _largest_divisor(B, 8), name=f"swa_{tag}_{l}")
                mixed.append(o.reshape(B * T, -1))
        if l < n_a:
            mix = (tuple(mixed), _at(p["a_w_out"], l), p["zero_bias"])
        else:
            mix = (tuple(mixed), _at(p["b_w_o"], l - n_a), _at(p["b_b_o"], l - n_a))
        last = l == depth - 1
        x = _ffn(x, rows, _at(p["ffn_norm"], l, 1), _at(p["ffn_wg"], l, 1), _at(p["ffn_wu"], l, 1),
                 _at(p["ffn_wd"], l, 1), mix=mix, final_gain=p["final_norm"] if last else None,
                 split_out=last, name=f"ffn_{l}b")

    results = []
    for i, st in enumerate(streams):
        B, T, D = st["x"].shape
        new_c, new_n, new_m = final_state[i]
        k_new, v_new = (a.reshape(B, T, -1) for a in kv_new[i])
        if st["cache"] is None:
            wb = min(WINDOW, T)
            buf_k, buf_v = k_new[:, -wb:], v_new[:, -wb:]
        else:
            wb = st["cache"][0].shape[1]
            flat = lambda a: a.reshape(B, a.shape[1], -1)
            buf_k = jnp.concatenate([flat(st["cache"][0]), k_new], axis=1)[:, -wb:]
            buf_v = jnp.concatenate([flat(st["cache"][1]), v_new], axis=1)[:, -wb:]
        heads = lambda a: a.reshape(B, wb, B_KV_HEADS, B_HEAD_DIM)
        results.append((x[i].reshape(B, T, D), new_c, new_n, new_m[..., 0], heads(buf_k), heads(buf_v)))
    return results


def kernel(x_prompt, x_sample, state_mlstm_C, state_mlstm_n, state_mlstm_m, cache_swa_k, cache_swa_v,
           ffn_norm, ffn_w_gate, ffn_w_up, ffn_w_down, mix_norm, a_w_in, a_b_gate, a_head_norm, a_w_out,
           kv_norm, w_kv, b_kv, b_w_q, b_b_q, b_sinks, b_w_o, b_b_o, final_norm):
    n_a, d_model, a_proj = a_w_in.shape
    ng = 2 * A_HEADS
    qk_w = d_model // 2
    v_w = d_model
    if a_proj != 2 * qk_w + 2 * v_w + ng:
        raise ValueError("unexpected mLSTM projection width")
    row = lambda a: a[..., None, :].astype(F32)
    p = {
        "ffn_norm": row(ffn_norm), "mix_norm": row(mix_norm), "kv_norm": row(kv_norm),
        "final_norm": row(final_norm), "a_head_norm": row(a_head_norm),
        "ffn_wg": ffn_w_gate.astype(BF16), "ffn_wu": ffn_w_up.astype(BF16), "ffn_wd": ffn_w_down.astype(BF16),
        "a_wqk": a_w_in[:, :, :2 * qk_w].astype(BF16),
        "a_wv": a_w_in[:, :, 2 * qk_w:2 * qk_w + v_w].astype(BF16),
        "a_wo": a_w_in[:, :, 2 * qk_w + v_w:2 * qk_w + 2 * v_w].astype(BF16),
        "a_wgt": jnp.pad(a_w_in[:, :, 2 * qk_w + 2 * v_w:], ((0, 0), (0, 0), (0, LANES - ng))).astype(BF16),
        "a_bg": jnp.pad(a_b_gate, ((0, 0), (0, LANES - ng)))[:, None, :].astype(F32),
        "a_w_out": a_w_out.astype(BF16),
        "zero_bias": jnp.zeros((1, d_model), F32),
        "w_kv": w_kv.astype(BF16), "b_kv": row(b_kv),
        "b_w_q": b_w_q.astype(BF16), "b_b_q": row(b_b_q), "b_sinks": row(b_sinks),
        "b_w_o": b_w_o.astype(BF16), "b_b_o": row(b_b_o),
    }
    Bp, Sp, _ = x_prompt.shape
    Bs, Ts, _ = x_sample.shape
    zc = jnp.zeros((1, Bp) + state_mlstm_C.shape[2:], F32)
    zn = jnp.zeros((1, Bp) + state_mlstm_n.shape[2:], F32)
    zm = jnp.zeros((1, Bp) + state_mlstm_m.shape[2:] + (1,), F32)
    s_state = (state_mlstm_C, state_mlstm_n, state_mlstm_m[..., None])

    p_rows = math.lcm(Sp, _window_tile(0, Bp * Sp))
    p_tables = _rope_tables(jnp.arange(p_rows, dtype=jnp.int32) % Sp)
    s_rows = math.lcm(Ts, _window_tile(Bp * Sp, Bs * Ts))
    s_tables = _rope_tables(PAST_LEN + jnp.arange(s_rows, dtype=jnp.int32) % Ts)

    prompt = dict(x=x_prompt, tables=p_tables, state=(zc, zn, zm), cache=None, tag="p", dtype=BF16)
    sample = dict(x=x_sample, tables=s_tables, state=s_state, cache=(cache_swa_k, cache_swa_v), tag="s",
                  dtype=F32)
    (y_p, p_c, p_n, p_m, p_k, p_v), (y_s, s_c, s_n, s_m, s_k, s_v) = _forward([prompt, sample], p)
    return (y_p, y_s, p_c, p_n, p_m, p_k, p_v, s_c, s_n, s_m, s_k, s_v)
```

```python
import functools
import math

import jax
import jax.numpy as jnp
from jax import lax
from jax.experimental import pallas as pl
from jax.experimental.pallas import tpu as pltpu

F32 = jnp.float32
BF16 = jnp.bfloat16

EPS = 1e-6
GATE_CAP = 15.0
A_HEADS = 4
MLSTM_CHUNK = 256
B_HEAD_DIM = 64
B_KV_HEADS = 4
WINDOW = 128
ROT_DIM = B_HEAD_DIM // 4
ROPE_THETA = 500000.0
PAST_LEN = 8192

LANES = 128
V7X_VMEM_LIMIT_BYTES = 60 * 1024 * 1024
TOKEN_TILE = 512
FF_TILE = 256
CAST_STEPS = 16
SWA_BLOCKS_PER_STEP = 4


def _rms(x, g):
    return (x * lax.rsqrt(jnp.mean(x * x, axis=-1, keepdims=True) + EPS)) * g


def _dot(a, b):
    return jnp.dot(a, b, preferred_element_type=F32)


def _dot_nt(a, b):
    return lax.dot_general(a, b, (((1,), (1,)), ((), ())), preferred_element_type=F32)


def _dot_f32(a, b):
    return jnp.dot(a, b, preferred_element_type=F32, precision=lax.Precision.HIGHEST)


def _at(arr, *idx):
    return (arr, idx)


def _split(param):
    return param if isinstance(param, tuple) else (param, ())


def _shape(param):
    arr, idx = _split(param)
    return arr.shape[len(idx):]


def _resident(param):
    arr, idx = _split(param)
    block = (None,) * len(idx) + arr.shape[len(idx):]
    index = idx + (0,) * (arr.ndim - len(idx))
    return arr, pl.BlockSpec(block, lambda *_: index, pipeline_mode=pl.Buffered(1))


def _operands(tiled, residents):
    pairs = list(tiled) + [_resident(r) for r in residents]
    return [a for a, _ in pairs], [sp for _, sp in pairs]


def _rows(tm, width):
    return pl.BlockSpec((tm, width), lambda i: (i, 0))


def _params(*sem):
    return pltpu.CompilerParams(dimension_semantics=sem, vmem_limit_bytes=V7X_VMEM_LIMIT_BYTES)


def _token_tile(n):
    tm = TOKEN_TILE
    while n % tm:
        tm //= 2
    return tm


def _ffn_kernel(*refs, n_x, n_hg, n_cast, final, n_out, n_first, tf):
    refs = list(refs)
    x_refs = [refs.pop(0) for _ in range(n_x)]
    hg_refs = [refs.pop(0) for _ in range(n_hg)]
    cast_src = [refs.pop(0) for _ in range(n_cast)]
    if n_hg:
        wmix_ref, bmix_ref = refs.pop(0), refs.pop(0)
    gn_ref, wg_ref, wu_ref, wd_ref = refs.pop(0), refs.pop(0), refs.pop(0), refs.pop(0)
    if final:
        fg_ref = refs.pop(0)
    o_refs = [refs.pop(0) for _ in range(n_out)]
    cast_dst = [refs.pop(0) for _ in range(n_cast)]
    (a_ref,) = refs
    first = pl.program_id(0) < n_first

    if n_cast:
        @pl.when(pl.program_id(0) < CAST_STEPS)
        def _():
            for src, dst in zip(cast_src, cast_dst):
                dst[...] = src[...].astype(dst.dtype)

    def current(rs, dtype):
        vals = [r[...].astype(dtype) for r in rs]
        return vals[0] if len(vals) == 1 else jnp.where(first, vals[0], vals[1])

    x = current(x_refs, F32)
    if n_hg:
        x = x + (_dot(current(hg_refs, BF16), wmix_ref[...]) + bmix_ref[...])
    h = _rms(x, gn_ref[...]).astype(BF16)
    for j in range(wg_ref.shape[1] // tf):
        sl = slice(j * tf, (j + 1) * tf)
        g = _dot(h, wg_ref[:, sl])
        u = _dot(h, wu_ref[:, sl])
        a_ref[:, sl] = ((g * jax.nn.sigmoid(g)) * u).astype(BF16)
    out = x + 0.5 * _dot(a_ref[...], wd_ref[...])
    if final:
        out = _rms(out, fg_ref[...])
    if n_out == 1:
        o_refs[0][...] = out
    else:
        @pl.when(first)
        def _():
            o_refs[0][...] = out

        @pl.when(jnp.logical_not(first))
        def _():
            o_refs[1][...] = out


def _stream_rows(tm, width, n_first):
    return (pl.BlockSpec((tm, width), lambda i: (jnp.minimum(i, n_first - 1), 0)),
            pl.BlockSpec((tm, width), lambda i: (jnp.maximum(i - n_first, 0), 0)))


def _ffn(x, rows, gn, wg, wu, wd, mix=None, final_gain=None, split_out=False, cast=(), name="ffn"):
    n = sum(rows)
    d = _shape(gn)[1]
    dff = _shape(wg)[1]
    tf = FF_TILE if dff % FF_TILE == 0 else dff
    residents = ([mix[1], mix[2]] if mix is not None else []) + [gn, wg, wu, wd]
    if final_gain is not None:
        residents.append(final_gain)
    streamed = [x] + ([mix[0]] if mix is not None else [])
    streamed = [a for s in streamed for a in (s if isinstance(s, tuple) else (s,))]
    cast_chunks = [(_shape(c)[0] // CAST_STEPS, _shape(c)[1]) for c in cast]

    nbytes = lambda shape, dtype: math.prod(shape) * jnp.dtype(dtype).itemsize
    fixed = (sum(nbytes(_shape(r), _split(r)[0].dtype) for r in residents)
             + 2 * sum(nbytes(c, F32) + nbytes(c, BF16) for c in cast_chunks))
    per_row = (2 * sum(nbytes(a.shape[1:], a.dtype) for a in streamed)
               + 2 * (2 if split_out else 1) * d * 4 + dff * 2 + d * 4)
    tm = 2 * TOKEN_TILE
    while math.gcd(*rows) % tm or fixed + tm * per_row > V7X_VMEM_LIMIT_BYTES:
        tm //= 2
    n_first = rows[0] // tm
    if cast and n // tm < CAST_STEPS:
        raise ValueError("too few grid steps for the weight rounding side job")

    def tiles(a):
        if isinstance(a, tuple):
            return list(zip(a, _stream_rows(tm, a[0].shape[1], n_first)))
        return [(a, _rows(tm, a.shape[1]))]

    tiled = tiles(x)
    n_x = len(tiled)
    n_hg = 0
    if mix is not None:
        tiled += tiles(mix[0])
        n_hg = len(tiled) - n_x
    chunk_of = lambda i: jnp.minimum(i, CAST_STEPS - 1)
    for c, chunk in zip(cast, cast_chunks):
        arr, idx = _split(c)
        tiled.append((arr, pl.BlockSpec((None,) * len(idx) + chunk, lambda i, idx=idx: idx + (chunk_of(i), 0))))
    args, specs = _operands(tiled, residents)
    if split_out:
        out_specs = list(_stream_rows(tm, d, n_first))
        out_shape = [jax.ShapeDtypeStruct((r, d), F32) for r in rows]
    else:
        out_specs, out_shape = [_rows(tm, d)], [jax.ShapeDtypeStruct((n, d), F32)]
    n_out = len(out_specs)
    out_specs += [pl.BlockSpec(chunk, lambda i: (chunk_of(i), 0)) for chunk in cast_chunks]
    out_shape += [jax.ShapeDtypeStruct(_shape(c), BF16) for c in cast]
    kern = functools.partial(_ffn_kernel, n_x=n_x, n_hg=n_hg, n_cast=len(cast), final=final_gain is not None,
                             n_out=n_out, n_first=n_first, tf=tf)
    outs = pl.pallas_call(
        kern,
        grid=(n // tm,),
        in_specs=specs,
        out_specs=out_specs,
        out_shape=out_shape,
        scratch_shapes=[pltpu.VMEM((tm, dff), BF16)],
        compiler_params=_params("arbitrary"),
        name=name,
    )(*args)
    y = tuple(outs[:n_out]) if split_out else outs[0]
    return y, tuple(outs[n_out:])


def _inproj_kernel(x_ref, gn_ref, wqk_ref, wv_ref, wo_ref, wgt_ref, bg_ref,
                   q_ref, k_ref, v_ref, o_ref, g_ref, *, k_scale):
    h = _rms(x_ref[...], gn_ref[...]).astype(BF16)
    dqk = q_ref.shape[1]
    qk = _dot(h, wqk_ref[...])
    q_ref[...] = qk[:, :dqk].astype(q_ref.dtype)
    k_ref[...] = (qk[:, dqk:] * k_scale).astype(k_ref.dtype)
    v_ref[...] = _dot(h, wv_ref[...]).astype(v_ref.dtype)
    o_ref[...] = _dot(h, wo_ref[...])
    gates = _dot(h, wgt_ref[...]) + bg_ref[...]
    gates = GATE_CAP * jnp.tanh(gates / GATE_CAP)
    logsig = jnp.minimum(gates, 0.0) - jnp.log1p(jnp.exp(-jnp.abs(gates)))
    lane = lax.broadcasted_iota(jnp.int32, gates.shape, 1)
    g_ref[...] = jnp.where(lane < A_HEADS, gates, logsig)


def _window_tile(row0, n):
    return _token_tile(math.gcd(row0, n))


def _window(xw):
    arr, row0, n = xw
    tm = _window_tile(row0, n)
    blk0 = row0 // tm
    return n, tm, (arr, pl.BlockSpec((tm, arr.shape[1]), lambda i: (i + blk0, 0)))


def _inproj(xw, gn, wqk, wv, wo, wgt, bg, act_dtype, name):
    n, tm, x_op = _window(xw)
    dqk = _shape(wqk)[1] // 2
    dv = _shape(wv)[1]
    kern = functools.partial(_inproj_kernel, k_scale=(dqk // A_HEADS) ** -0.5)
    args, specs = _operands([x_op], [gn, wqk, wv, wo, wgt, bg])
    return pl.pallas_call(
        kern,
        grid=(n // tm,),
        in_specs=specs,
        out_specs=[_rows(tm, dqk), _rows(tm, dqk), _rows(tm, dv), _rows(tm, dv), _rows(tm, LANES)],
        out_shape=[jax.ShapeDtypeStruct((n, dqk), act_dtype), jax.ShapeDtypeStruct((n, dqk), act_dtype),
                   jax.ShapeDtypeStruct((n, dv), act_dtype), jax.ShapeDtypeStruct((n, dv), F32),
                   jax.ShapeDtypeStruct((n, LANES), F32)],
        compiler_params=_params("parallel"),
        name=name,
    )(*args)


def _rope(x, cos, sin):
    width = x.shape[1]
    reps = width // LANES
    cosw = jnp.concatenate([cos] * reps, axis=1)
    sinw = jnp.concatenate([sin] * reps, axis=1)
    half = ROT_DIM // 2
    lane = lax.broadcasted_iota(jnp.int32, x.shape, 1)
    first_half = (lane & (B_HEAD_DIM - 1)) < half
    partner = jnp.where(first_half, pltpu.roll(x, width - half, 1), pltpu.roll(x, half, 1))
    return x * cosw + partner * sinw


def _kv_kernel(x_ref, cos_ref, sin_ref, gn_ref, w_ref, b_ref, k_ref, v_ref):
    h = _rms(x_ref[...], gn_ref[...]).astype(BF16)
    kv = _dot(h, w_ref[...]) + b_ref[...]
    kw = k_ref.shape[1]
    k_ref[...] = _rope(kv[:, :kw], cos_ref[...], sin_ref[...])
    v_ref[...] = kv[:, kw:]


def _pos_spec(tm, table):
    nblk = table.shape[0] // tm
    return pl.BlockSpec((tm, LANES), lambda i: (i % nblk, 0))


def _kv(xw, gn, w, b, cos, sin, name):
    n, tm, x_op = _window(xw)
    kw = _shape(w)[1] // 2
    args, specs = _operands([x_op, (cos, _pos_spec(tm, cos)), (sin, _pos_spec(tm, sin))], [gn, w, b])
    return pl.pallas_call(
        _kv_kernel,
        grid=(n // tm,),
        in_specs=specs,
        out_specs=[_rows(tm, kw), _rows(tm, kw)],
        out_shape=[jax.ShapeDtypeStruct((n, kw), F32), jax.ShapeDtypeStruct((n, kw), F32)],
        compiler_params=_params("parallel"),
        name=name,
    )(*args)


def _qproj_kernel(x_ref, cos_ref, sin_ref, gn_ref, w_ref, b_ref, q_ref, *, q_scale):
    h = _rms(x_ref[...], gn_ref[...]).astype(BF16)
    q = _dot(h, w_ref[...]) + b_ref[...]
    q_ref[...] = (_rope(q, cos_ref[...], sin_ref[...]) * q_scale).astype(q_ref.dtype)


def _qproj(xw, gn, w, b, cos, sin, act_dtype, name):
    n, tm, x_op = _window(xw)
    kern = functools.partial(_qproj_kernel, q_scale=B_HEAD_DIM ** -0.5)
    qw = _shape(w)[1]
    args, specs = _operands([x_op, (cos, _pos_spec(tm, cos)), (sin, _pos_spec(tm, sin))], [gn, w, b])
    return pl.pallas_call(
        kern,
        grid=(n // tm,),
        in_specs=specs,
        out_specs=_rows(tm, qw),
        out_shape=jax.ShapeDtypeStruct((n, qw), act_dtype),
        compiler_params=_params("parallel"),
        name=name,
    )(*args)


def _mlstm_kernel(q_ref, k_ref, v_ref, o_ref, gtm_ref, gt_ref, hn_ref, c0_ref, n0_ref, m0_ref, *rest,
                  bblk, chunk):
    out_ref, c_ref, n_ref, m_ref = rest[-4:]
    L = chunk
    dqk = q_ref.shape[2] // A_HEADS
    dv = v_ref.shape[2] // A_HEADS

    @pl.when(pl.program_id(1) == 0)
    def _():
        c_ref[...] = c0_ref[...]
        n_ref[...] = n0_ref[...]
        m_ref[...] = m0_ref[...]

    row = lax.broadcasted_iota(jnp.int32, (L, L), 0)
    col = lax.broadcasted_iota(jnp.int32, (L, L), 1)
    tril = col <= row
    lower_ones = tril.astype(F32)
    upper_ones = (row <= col).astype(F32)

    chains = [(b, h) for b in range(bblk) for h in range(A_HEADS)]
    per_chain = lambda f: jnp.stack([f(b, h) for b, h in chains])

    gts = [gt_ref[b, 0] for b in range(bblk)]
    gtms = [gtm_ref[b] for b in range(bblk)]
    csum_rows = [_dot_f32(gt, upper_ones) for gt in gts]
    csum_cols = [_dot_f32(lower_ones, gtm) for gtm in gtms]
    i_row = per_chain(lambda b, h: gts[b][h:h + 1, :])
    b_row = per_chain(lambda b, h: csum_rows[b][A_HEADS + h:A_HEADS + h + 1, :])
    i_col = per_chain(lambda b, h: gtms[b][:, h:h + 1])
    b_col = per_chain(lambda b, h: csum_cols[b][:, A_HEADS + h:A_HEADS + h + 1])
    b_last = b_col[:, L - 1:L, :]
    m_prev = per_chain(lambda b, h: m_ref[b, h:h + 1, :])
    n_prev = per_chain(lambda b, h: n_ref[b, h:h + 1, :])
    c_prev = per_chain(lambda b, h: c_ref[b, h])

    q = per_chain(lambda b, h: q_ref[b, :, h * dqk:(h + 1) * dqk]).astype(BF16)
    k = per_chain(lambda b, h: k_ref[b, :, h * dqk:(h + 1) * dqk]).astype(BF16)
    v = per_chain(lambda b, h: v_ref[b, :, h * dv:(h + 1) * dv]).astype(BF16)

    dmat = jnp.where(tril[None], (b_col - b_row) + i_row, -jnp.inf)
    g = b_col + m_prev
    mt = jnp.maximum(g, jnp.max(dmat, axis=-1, keepdims=True))
    w_inter = jnp.exp(g - mt)
    s = jnp.einsum("gtd,gsd->gts", q, k, preferred_element_type=F32) * jnp.exp(dmat - mt)
    G = len(chains)
    v_ones = jnp.concatenate([v, jnp.ones((G, L, LANES), BF16)], axis=-1)
    n_cols = jnp.broadcast_to(jnp.swapaxes(n_prev, 1, 2), (G, dqk, LANES))
    c_n = jnp.concatenate([c_prev, n_cols], axis=-1).astype(BF16)
    intra = jnp.einsum("gts,gsv->gtv", s.astype(BF16), v_ones, preferred_element_type=F32)
    inter = w_inter * jnp.einsum("gtd,gdv->gtv", q, c_n, preferred_element_type=F32)
    both = intra + inter
    den = jnp.maximum(jnp.abs(both[:, :, dv:]), jnp.exp(-mt))
    hc = both[:, :, :dv] / jnp.concatenate([den] * (dv // LANES), axis=-1)

    m_new = mt[:, L - 1:L, :]
    a_col = jnp.exp(((b_last - b_col) + i_col) - m_new)
    decay = jnp.exp((b_last + m_prev) - m_new)
    ka = k.astype(F32) * a_col
    kat = jnp.swapaxes(ka, 1, 2).astype(BF16)
    c_new = decay * c_prev + jnp.einsum("gds,gsv->gdv", kat, v, preferred_element_type=F32)
    n_new = decay * n_prev + jnp.sum(ka, axis=1, keepdims=True)

    gain = per_chain(lambda b, h: hn_ref[:, h * dv:(h + 1) * dv])
    gate = jax.nn.sigmoid(per_chain(lambda b, h: o_ref[b, :, h * dv:(h + 1) * dv]))
    res = (_rms(hc, gain) * gate).astype(out_ref.dtype)
    for gi, (b, h) in enumerate(chains):
        c_ref[b, h] = c_new[gi]
        n_ref[b, h:h + 1, :] = n_new[gi]
        m_ref[b, h:h + 1, :] = m_new[gi]
        out_ref[b, :, h * dv:(h + 1) * dv] = res[gi]


def _mlstm(q, k, v, o, gates, head_norm, state0, layer0, layer, n_layers, prev_final, bblk, out_dtype, name):
    B, S, qw = q.shape
    vw = v.shape[2]
    L = MLSTM_CHUNK if S % MLSTM_CHUNK == 0 else S
    nc = S // L
    ng = 2 * A_HEADS
    gt = jnp.swapaxes(gates[:, :, :ng].reshape(B, nc, L, ng), 2, 3)
    kern = functools.partial(_mlstm_kernel, bblk=bblk, chunk=L)
    seq = lambda w: pl.BlockSpec((bblk, L, w), lambda i, c: (i, c, 0))

    def state(a, row):
        return pl.BlockSpec((None, bblk) + a.shape[2:], lambda i, c: (row, i) + (0,) * (a.ndim - 2))

    hn_arr, hn_spec = _resident(head_norm)
    args = [q, k, v, o, gates, gt, hn_arr, *state0]
    specs = [seq(qw), seq(qw), seq(vw), seq(vw), seq(LANES),
             pl.BlockSpec((bblk, 1, ng, L), lambda i, c: (i, c, 0, 0)), hn_spec,
             *[state(a, layer0) for a in state0]]
    aliases = {}
    if prev_final is not None:
        aliases = {len(args) + t: 1 + t for t in range(len(prev_final))}
        args += list(prev_final)
        specs += [pl.BlockSpec(memory_space=pl.ANY)] * len(prev_final)
    return pl.pallas_call(
        kern,
        grid=(B // bblk, nc),
        in_specs=specs,
        out_specs=[seq(vw)] + [state(a, layer) for a in state0],
        out_shape=[jax.ShapeDtypeStruct((B, S, vw), out_dtype)]
                  + [jax.ShapeDtypeStruct((n_layers,) + a.shape[1:], F32) for a in state0],
        input_output_aliases=aliases,
        compiler_params=_params("parallel", "arbitrary"),
        name=name,
    )(*args)


def _pair_block_diag(x):
    zero = jnp.zeros_like(x)
    return jnp.concatenate([jnp.concatenate([x, zero], axis=1), jnp.concatenate([zero, x], axis=1)], axis=0)


def _swa_prompt_kernel(sinks_ref, q_ref, kp_ref, kc_ref, vp_ref, vc_ref, o_ref, *, group):
    W = WINDOW
    hd = B_HEAD_DIM
    pairs = group // 2
    R = pairs * W
    nblk = q_ref.shape[1] // W
    has_prev = pl.program_id(1) > 0
    t = lax.broadcasted_iota(jnp.int32, (R, 2 * W), 0) & (W - 1)
    s = lax.broadcasted_iota(jnp.int32, (R, 2 * W), 1) & (W - 1)
    use_cur = s <= t
    visible = jnp.logical_or(use_cur, has_prev)
    row = lax.broadcasted_iota(jnp.int32, (R, 1), 0)
    first_head = lax.broadcasted_iota(jnp.int32, (R, 2 * hd), 1) < hd
    ones = jnp.ones((W, hd), BF16)

    def key_blocks(ref_prev, ref_cur, blk, ksl):
        cur = ref_cur[0, blk * W:(blk + 1) * W, ksl]
        prev = ref_prev[0, :, ksl] if blk == 0 else ref_cur[0, (blk - 1) * W:blk * W, ksl]
        return prev.astype(BF16), cur.astype(BF16)

    def scores(item):
        blk, kh = item
        ksl = slice(kh * hd, (kh + 1) * hd)
        rows = slice(blk * W, (blk + 1) * W)
        q2 = jnp.concatenate(
            [q_ref[0, rows, (kh * pairs + pr) * 2 * hd:(kh * pairs + pr + 1) * 2 * hd] for pr in range(pairs)],
            axis=0).astype(BF16)
        k_prev, k_cur = key_blocks(kp_ref, kc_ref, blk, ksl)
        sc_cur = _dot_nt(q2, _pair_block_diag(k_cur))
        sc_prev = _dot_nt(q2, _pair_block_diag(k_prev))
        sc = jnp.where(use_cur, sc_cur, sc_prev)
        return jnp.where(visible, sc, -jnp.inf) if blk == 0 else sc

    items = [(blk, kh) for blk in range(nblk) for kh in range(B_KV_HEADS)]
    sc_next = scores(items[0])
    for idx, (blk, kh) in enumerate(items):
        ksl = slice(kh * hd, (kh + 1) * hd)
        sc = sc_next
        if idx + 1 < len(items):
            sc_next = scores(items[idx + 1])
        probs, sink_terms = [], []
        for g in range(2):
            sink = jnp.full((R, 1), sinks_ref[0, kh * group + g], F32)
            for pr in range(1, pairs):
                sink = jnp.where(row >= pr * W, sinks_ref[0, kh * group + 2 * pr + g], sink)
            sg = sc[:, g * W:(g + 1) * W]
            mx = jnp.maximum(jnp.max(sg, axis=-1, keepdims=True), sink)
            probs.append(jnp.exp(sg - mx))
            sink_terms.append(jnp.exp(sink - mx))
        p = jnp.concatenate(probs, axis=1)
        p_cur = jnp.where(use_cur, p, 0.0).astype(BF16)
        p_prev = jnp.where(use_cur, 0.0, p).astype(BF16)
        v_prev, v_cur = key_blocks(vp_ref, vc_ref, blk, ksl)
        v_cur = jnp.concatenate([_pair_block_diag(v_cur), _pair_block_diag(ones)], axis=1)
        v_prev = jnp.concatenate([_pair_block_diag(v_prev), _pair_block_diag(ones)], axis=1)
        acc = _dot(p_cur, v_cur) + _dot(p_prev, v_prev)
        denom = acc[:, 2 * hd:] + jnp.where(first_head, sink_terms[0], sink_terms[1])
        out = acc[:, :2 * hd] / denom
        for pr in range(pairs):
            lo = (kh * pairs + pr) * 2 * hd
            o_ref[0, blk * W:(blk + 1) * W, lo:lo + 2 * hd] = out[pr * W:(pr + 1) * W, :].astype(o_ref.dtype)


def _swa_prompt(q, k, v, sinks, out_dtype, name):
    B, S, qw = q.shape
    kw = k.shape[2]
    nb = S // WINDOW
    group = qw // kw
    if group % 2 or 2 * B_HEAD_DIM != LANES:
        raise ValueError("the prompt attention kernel pairs query heads into LANES-wide slices")
    nblk = SWA_BLOCKS_PER_STEP if nb % SWA_BLOCKS_PER_STEP == 0 else 1
    cur = lambda w: pl.BlockSpec((1, nblk * WINDOW, w), lambda b, j: (b, j, 0))
    prev = lambda w: pl.BlockSpec((1, WINDOW, w), lambda b, j: (b, jnp.maximum(j * nblk - 1, 0), 0))
    kern = functools.partial(_swa_prompt_kernel, group=group)
    return pl.pallas_call(
        kern,
        grid=(B, nb // nblk),
        in_specs=[pl.BlockSpec(memory_space=pltpu.SMEM), cur(qw), prev(kw), cur(kw), prev(kw), cur(kw)],
        out_specs=cur(qw),
        out_shape=jax.ShapeDtypeStruct((B, S, qw), out_dtype),
        compiler_params=_params("parallel", "parallel"),
        name=name,
    )(sinks, q, k, k, v, v)


def _swa_sample_kernel(sinks_ref, q_ref, kc_ref, kn_ref, vc_ref, vn_ref, o_ref, *, group, bblk):
    T = q_ref.shape[1]
    wb, kw = kc_ref.shape[1:]
    hd = B_HEAD_DIM
    nq = B_KV_HEADS * group
    R = nq * T
    qb = q_ref[...]
    blocks = []
    for kh in range(B_KV_HEADS):
        for g in range(group):
            lo = (kh * group + g) * hd
            parts = [qb[:, :, lo:lo + hd]]
            if kh:
                parts.insert(0, jnp.zeros((bblk, T, kh * hd), qb.dtype))
            if kh < B_KV_HEADS - 1:
                parts.append(jnp.zeros((bblk, T, (B_KV_HEADS - 1 - kh) * hd), qb.dtype))
            blocks.append(jnp.concatenate(parts, axis=-1))
    q_all = jnp.concatenate(blocks, axis=1).astype(BF16)

    t = lax.broadcasted_iota(jnp.int32, (R, wb), 0) % T
    i = lax.broadcasted_iota(jnp.int32, (R, wb), 1)
    mask_buf = jnp.logical_and((t + wb) - i < WINDOW, (PAST_LEN - wb) + i >= 0)
    tn = lax.broadcasted_iota(jnp.int32, (R, T), 0) % T
    u = lax.broadcasted_iota(jnp.int32, (R, T), 1)
    mask_new = u <= tn
    ridx = lax.broadcasted_iota(jnp.int32, (R, 1), 0)
    sink = jnp.full((R, 1), sinks_ref[0, 0], F32)
    for hq in range(1, nq):
        sink = jnp.where(ridx >= hq * T, sinks_ref[0, hq], sink)

    sc_buf = jnp.einsum("bqd,bkd->bqk", q_all, kc_ref[...].astype(BF16), preferred_element_type=F32)
    sc_new = jnp.einsum("bqd,bkd->bqk", q_all, kn_ref[...].astype(BF16), preferred_element_type=F32)
    sc_buf = jnp.where(mask_buf[None], sc_buf, -jnp.inf)
    sc_new = jnp.where(mask_new[None], sc_new, -jnp.inf)
    mx = jnp.maximum(jnp.maximum(jnp.max(sc_buf, axis=-1, keepdims=True),
                                 jnp.max(sc_new, axis=-1, keepdims=True)), sink[None])
    p_buf = jnp.exp(sc_buf - mx)
    p_new = jnp.exp(sc_new - mx)
    denom = (jnp.sum(p_buf, axis=-1, keepdims=True) + jnp.sum(p_new, axis=-1, keepdims=True)
             + jnp.exp(sink[None] - mx))
    acc = (jnp.einsum("bqk,bkd->bqd", p_buf.astype(BF16), vc_ref[...].astype(BF16), preferred_element_type=F32)
           + jnp.einsum("bqk,bkd->bqd", p_new.astype(BF16), vn_ref[...].astype(BF16), preferred_element_type=F32))
    out = (acc / denom).astype(o_ref.dtype)
    for kh in range(B_KV_HEADS):
        for g in range(group):
            hq = kh * group + g
            o_ref[:, :, hq * hd:(hq + 1) * hd] = out[:, hq * T:(hq + 1) * T, kh * hd:(kh + 1) * hd]


def _swa_sample(q, k_buf, k_new, v_buf, v_new, sinks, bblk, name):
    B, T, qw = q.shape
    wb, kw = k_buf.shape[1:]
    group = qw // kw
    blk = lambda r, w: pl.BlockSpec((bblk, r, w), lambda i: (i, 0, 0))
    kern = functools.partial(_swa_sample_kernel, group=group, bblk=bblk)
    return pl.pallas_call(
        kern,
        grid=(B // bblk,),
        in_specs=[pl.BlockSpec(memory_space=pltpu.SMEM), blk(T, qw), blk(wb, kw), blk(T, kw),
                  blk(wb, kw), blk(T, kw)],
        out_specs=blk(T, qw),
        out_shape=jax.ShapeDtypeStruct((B, T, qw), BF16),
        compiler_params=_params("parallel"),
        name=name,
    )(sinks, q, k_buf, k_new, v_buf, v_new)


def _rope_tables(pos):
    half = ROT_DIM // 2
    inv_freq = jnp.exp(-math.log(ROPE_THETA) * jnp.arange(0, ROT_DIM, 2, dtype=F32) / ROT_DIM)
    ang = pos.astype(F32)[:, None] * inv_freq[None, :]
    d = jnp.arange(LANES, dtype=jnp.int32) % B_HEAD_DIM
    ang_l = jnp.take(ang, d % half, axis=1)
    rot = (d < ROT_DIM)[None, :]
    sign = jnp.where(d < half, -1.0, 1.0)[None, :]
    cos = jnp.where(rot, jnp.cos(ang_l), 1.0)
    sin = jnp.where(rot, jnp.sin(ang_l) * sign, 0.0)
    return cos.astype(F32), sin.astype(F32)


def _largest_divisor(n, cap):
    d = min(n, cap)
    while n % d:
        d -= 1
    return d


def _forward(streams, p):
    n_a = p["a_wqk"].shape[0]
    depth = p["ffn_norm"].shape[0]
    rows = tuple(st["x"].shape[0] * st["x"].shape[1] for st in streams)
    x = tuple(st["x"].reshape(n, -1) for st, n in zip(streams, rows))

    def window(i):
        if isinstance(x, tuple):
            return (x[i], 0, rows[i])
        return (x, sum(rows[:i]), rows[i])

    ffn_order = [(l, half) for l in range(depth) for half in (0, 1)]
    ffn_w = tuple(w[0, 0].astype(BF16) for w in p["ffn_w32"])

    def ffn_after(l, half):
        k = ffn_order.index((l, half)) + 1
        return tuple(_at(w, *ffn_order[k]) for w in p["ffn_w32"]) if k < len(ffn_order) else ()

    final_state = [None for _ in streams]
    kv_new = [None for _ in streams]
    for l in range(depth):
        if l == n_a:
            for i, st in enumerate(streams):
                kv_new[i] = _kv(window(i), p["kv_norm"], p["w_kv"], p["b_kv"], *st["tables"],
                                name=f"kv_{st['tag']}")
        x, ffn_w = _ffn(x, rows, _at(p["ffn_norm"], l, 0), *ffn_w, cast=ffn_after(l, 0), name=f"ffn_{l}a")
        mixed = []
        for i, st in enumerate(streams):
            B, T, _ = st["x"].shape
            tag, act = st["tag"], st["dtype"]
            seq = lambda a: a.reshape(B, T, a.shape[-1])
            if l < n_a:
                q, k, v, o, gates = _inproj(window(i), _at(p["mix_norm"], l), _at(p["a_wqk"], l),
                                            _at(p["a_wv"], l), _at(p["a_wo"], l), _at(p["a_wgt"], l),
                                            _at(p["a_bg"], l), act, name=f"inproj_{tag}_{l}")
                bblk = _largest_divisor(B, 2 if T >= MLSTM_CHUNK else 8)
                layer0 = l if st["state"][0].shape[0] == n_a else 0
                hg, *final_state[i] = _mlstm(seq(q), seq(k), seq(v), seq(o), seq(gates),
                                             _at(p["a_head_norm"], l), st["state"], layer0, l, n_a,
                                             final_state[i], bblk, BF16, name=f"mlstm_{tag}_{l}")
                mixed.append(hg.reshape(B * T, -1))
            else:
                j = l - n_a
                q = _qproj(window(i), _at(p["mix_norm"], l), _at(p["b_w_q"], j), _at(p["b_b_q"], j),
                           *st["tables"], act, name=f"qproj_{tag}_{l}")
                kn, vn = (seq(a) for a in kv_new[i])
                if st["cache"] is None:
                    o = _swa_prompt(seq(q), kn, vn, p["b_sinks"][j], BF16, name=f"swa_{tag}_{l}")
                else:
                    kc, vc = st["cache"]
                    o = _swa_sample(seq(q), kc.reshape(B, kc.shape[1], -1), kn, vc.reshape(B, vc.shape[1], -1),
                                    vn, p["b_sinks"][j], _largest_divisor(B, 8), name=f"swa_{tag}_{l}")
                mixed.append(o.reshape(B * T, -1))
        if l < n_a:
            mix = (tuple(mixed), _at(p["a_w_out"], l), p["zero_bias"])
        else:
            mix = (tuple(mixed), _at(p["b_w_o"], l - n_a), _at(p["b_b_o"], l - n_a))
        last = l == depth - 1
        x, ffn_w = _ffn(x, rows, _at(p["ffn_norm"], l, 1), *ffn_w, mix=mix, cast=ffn_after(l, 1),
                        final_gain=p["final_norm"] if last else None, split_out=last, name=f"ffn_{l}b")

    results = []
    for i, st in enumerate(streams):
        B, T, D = st["x"].shape
        new_c, new_n, new_m = final_state[i]
        k_new, v_new = (a.reshape(B, T, -1) for a in kv_new[i])
        if st["cache"] is None:
            wb = min(WINDOW, T)
            buf_k, buf_v = k_new[:, -wb:], v_new[:, -wb:]
        else:
            wb = st["cache"][0].shape[1]
            flat = lambda a: a.reshape(B, a.shape[1], -1)
            buf_k = jnp.concatenate([flat(st["cache"][0]), k_new], axis=1)[:, -wb:]
            buf_v = jnp.concatenate([flat(st["cache"][1]), v_new], axis=1)[:, -wb:]
        heads = lambda a: a.reshape(B, wb, B_KV_HEADS, B_HEAD_DIM)
        results.append((x[i].reshape(B, T, D), new_c, new_n, new_m[..., 0], heads(buf_k), heads(buf_v)))
    return results


def kernel(x_prompt, x_sample, state_mlstm_C, state_mlstm_n, state_mlstm_m, cache_swa_k, cache_swa_v,
           ffn_norm, ffn_w_gate, ffn_w_up, ffn_w_down, mix_norm, a_w_in, a_b_gate, a_head_norm, a_w_out,
           kv_norm, w_kv, b_kv, b_w_q, b_b_q, b_sinks, b_w_o, b_b_o, final_norm):
    n_a, d_model, a_proj = a_w_in.shape
    ng = 2 * A_HEADS
    qk_w = d_model // 2
    v_w = d_model
    if a_proj != 2 * qk_w + 2 * v_w + ng:
        raise ValueError("unexpected mLSTM projection width")
    row = lambda a: a[..., None, :].astype(F32)
    p = {
        "ffn_norm": row(ffn_norm), "mix_norm": row(mix_norm), "kv_norm": row(kv_norm),
        "final_norm": row(final_norm), "a_head_norm": row(a_head_norm),
        "ffn_w32": (ffn_w_gate, ffn_w_up, ffn_w_down),
        "a_wqk": a_w_in[:, :, :2 * qk_w].astype(BF16),
        "a_wv": a_w_in[:, :, 2 * qk_w:2 * qk_w + v_w].astype(BF16),
        "a_wo": a_w_in[:, :, 2 * qk_w + v_w:2 * qk_w + 2 * v_w].astype(BF16),
        "a_wgt": jnp.pad(a_w_in[:, :, 2 * qk_w + 2 * v_w:], ((0, 0), (0, 0), (0, LANES - ng))).astype(BF16),
        "a_bg": jnp.pad(a_b_gate, ((0, 0), (0, LANES - ng)))[:, None, :].astype(F32),
        "a_w_out": a_w_out.astype(BF16),
        "zero_bias": jnp.zeros((1, d_model), F32),
        "w_kv": w_kv.astype(BF16), "b_kv": row(b_kv),
        "b_w_q": b_w_q.astype(BF16), "b_b_q": row(b_b_q), "b_sinks": row(b_sinks),
        "b_w_o": b_w_o.astype(BF16), "b_b_o": row(b_b_o),
    }
    Bp, Sp, _ = x_prompt.shape
    Bs, Ts, _ = x_sample.shape
    zc = jnp.zeros((1, Bp) + state_mlstm_C.shape[2:], F32)
    zn = jnp.zeros((1, Bp) + state_mlstm_n.shape[2:], F32)
    zm = jnp.zeros((1, Bp) + state_mlstm_m.shape[2:] + (1,), F32)
    s_state = (state_mlstm_C, state_mlstm_n, state_mlstm_m[..., None])

    p_rows = math.lcm(Sp, _window_tile(0, Bp * Sp))
    p_tables = _rope_tables(jnp.arange(p_rows, dtype=jnp.int32) % Sp)
    s_rows = math.lcm(Ts, _window_tile(Bp * Sp, Bs * Ts))
    s_tables = _rope_tables(PAST_LEN + jnp.arange(s_rows, dtype=jnp.int32) % Ts)

    prompt = dict(x=x_prompt, tables=p_tables, state=(zc, zn, zm), cache=None, tag="p", dtype=BF16)
    sample = dict(x=x_sample, tables=s_tables, state=s_state, cache=(cache_swa_k, cache_swa_v), tag="s",
                  dtype=F32)
    (y_p, p_c, p_n, p_m, p_k, p_v), (y_s, s_c, s_n, s_m, s_k, s_v) = _forward([prompt, sample], p)
    return (y_p, y_s, p_c, p_n, p_m, p_k, p_v, s_c, s_n, s_m, s_k, s_v)
```

```python
import functools
import math

import jax
import jax.numpy as jnp
from jax import lax
from jax.experimental import pallas as pl
from jax.experimental.pallas import tpu as pltpu

F32 = jnp.float32
BF16 = jnp.bfloat16

EPS = 1e-6
GATE_CAP = 15.0
A_HEADS = 4
MLSTM_CHUNK = 256
B_HEAD_DIM = 64
B_KV_HEADS = 4
WINDOW = 128
ROT_DIM = B_HEAD_DIM // 4
ROPE_THETA = 500000.0
PAST_LEN = 8192

LANES = 128
V7X_VMEM_LIMIT_BYTES = 60 * 1024 * 1024
TOKEN_TILE = 512
FFN_SUBTILE = 512
FF_TILE = 256
CAST_STEPS = 16
SWA_BLOCKS_PER_STEP = 4


def _rms(x, g):
    return (x * lax.rsqrt(jnp.mean(x * x, axis=-1, keepdims=True) + EPS)) * g


def _dot(a, b):
    return jnp.dot(a, b, preferred_element_type=F32)


def _dot_nt(a, b):
    return lax.dot_general(a, b, (((1,), (1,)), ((), ())), preferred_element_type=F32)


def _dot_f32(a, b):
    return jnp.dot(a, b, preferred_element_type=F32, precision=lax.Precision.HIGHEST)


def _at(arr, *idx):
    return (arr, idx)


def _split(param):
    return param if isinstance(param, tuple) else (param, ())


def _shape(param):
    arr, idx = _split(param)
    return arr.shape[len(idx):]


def _resident(param):
    arr, idx = _split(param)
    block = (None,) * len(idx) + arr.shape[len(idx):]
    index = idx + (0,) * (arr.ndim - len(idx))
    return arr, pl.BlockSpec(block, lambda *_: index, pipeline_mode=pl.Buffered(1))


def _operands(tiled, residents):
    pairs = list(tiled) + [_resident(r) for r in residents]
    return [a for a, _ in pairs], [sp for _, sp in pairs]


def _rows(tm, width):
    return pl.BlockSpec((tm, width), lambda i: (i, 0))


def _params(*sem):
    return pltpu.CompilerParams(dimension_semantics=sem, vmem_limit_bytes=V7X_VMEM_LIMIT_BYTES)


def _token_tile(n):
    tm = TOKEN_TILE
    while n % tm:
        tm //= 2
    return tm


def _ffn_kernel(*refs, n_x, n_hg, n_cast, final, post, n_out, n_first, tf):
    refs = list(refs)
    x_refs = [refs.pop(0) for _ in range(n_x)]
    hg_refs = [refs.pop(0) for _ in range(n_hg)]
    cast_src = [refs.pop(0) for _ in range(n_cast)]
    if post:
        cos_refs, sin_refs = [refs.pop(0), refs.pop(0)], [refs.pop(0), refs.pop(0)]
    if n_hg:
        wmix_ref, bmix_ref = refs.pop(0), refs.pop(0)
    gn_ref, wg_ref, wu_ref, wd_ref = refs.pop(0), refs.pop(0), refs.pop(0), refs.pop(0)
    if final:
        fg_ref = refs.pop(0)
    if post:
        pgn_ref, pw_ref, pb_ref = refs.pop(0), refs.pop(0), refs.pop(0)
    o_refs = [refs.pop(0) for _ in range(n_out)]
    post_refs = [refs.pop(0) for _ in range({None: 0, "q": 1, "kv": 2}[post])]
    cast_dst = [refs.pop(0) for _ in range(n_cast)]
    (a_ref,) = refs
    first = pl.program_id(0) < n_first

    if n_cast:
        @pl.when(pl.program_id(0) < CAST_STEPS)
        def _():
            for src, dst in zip(cast_src, cast_dst):
                dst[...] = src[...].astype(dst.dtype)

    tm = a_ref.shape[0]
    sub = min(tm, FFN_SUBTILE)
    for r0 in range(0, tm, sub):
        rows = slice(r0, r0 + sub)

        def current(rs, dtype):
            vals = [r[rows, :].astype(dtype) for r in rs]
            return vals[0] if len(vals) == 1 else jnp.where(first, vals[0], vals[1])

        x = current(x_refs, F32)
        if n_hg:
            x = x + (_dot(current(hg_refs, BF16), wmix_ref[...]) + bmix_ref[...])
        h = _rms(x, gn_ref[...]).astype(BF16)
        for j in range(wg_ref.shape[1] // tf):
            sl = slice(j * tf, (j + 1) * tf)
            g = _dot(h, wg_ref[:, sl])
            u = _dot(h, wu_ref[:, sl])
            a_ref[rows, sl] = ((g * jax.nn.sigmoid(g)) * u).astype(BF16)
        out = x + 0.5 * _dot(a_ref[rows, :], wd_ref[...])
        if post:
            proj = _dot(_rms(out, pgn_ref[...]).astype(BF16), pw_ref[...]) + pb_ref[...]
            cos, sin = current(cos_refs, F32), current(sin_refs, F32)
            if post == "q":
                post_refs[0][rows, :] = _rope(proj, cos, sin).astype(post_refs[0].dtype)
            else:
                kw = proj.shape[1] // 2
                post_refs[0][rows, :] = _rope(proj[:, :kw], cos, sin).astype(post_refs[0].dtype)
                post_refs[1][rows, :] = proj[:, kw:].astype(post_refs[1].dtype)
        if final:
            out = _rms(out, fg_ref[...])
        if n_out == 1:
            o_refs[0][rows, :] = out
        else:
            @pl.when(first)
            def _():
                o_refs[0][rows, :] = out

            @pl.when(jnp.logical_not(first))
            def _():
                o_refs[1][rows, :] = out


def _stream_rows(tm, width, n_first):
    return (pl.BlockSpec((tm, width), lambda i: (jnp.minimum(i, n_first - 1), 0)),
            pl.BlockSpec((tm, width), lambda i: (jnp.maximum(i - n_first, 0), 0)))


def _ffn(x, rows, gn, wg, wu, wd, mix=None, final_gain=None, split_out=False, cast=(), post=None, name="ffn"):
    n = sum(rows)
    d = _shape(gn)[1]
    dff = _shape(wg)[1]
    tf = FF_TILE if dff % FF_TILE == 0 else dff
    residents = ([mix[1], mix[2]] if mix is not None else []) + [gn, wg, wu, wd]
    if final_gain is not None:
        residents.append(final_gain)
    streamed = [x] + ([mix[0]] if mix is not None else [])
    post_widths = []
    if post is not None:
        kind, pgn, pw, pb, tables_first, tables_second, post_dtype = post
        residents += [pgn, pw, pb]
        streamed += [(tables_first[0], tables_second[0]), (tables_first[1], tables_second[1])]
        post_widths = [_shape(pw)[1]] if kind == "q" else [_shape(pw)[1] // 2] * 2
    flat_streamed = [a for s in streamed for a in (s if isinstance(s, tuple) else (s,))]
    cast_chunks = [(_shape(c)[0] // CAST_STEPS, _shape(c)[1]) for c in cast]

    nbytes = lambda shape, dtype: math.prod(shape) * jnp.dtype(dtype).itemsize
    fixed = (sum(nbytes(_shape(r), _split(r)[0].dtype) for r in residents)
             + 2 * sum(nbytes(c, F32) + nbytes(c, BF16) for c in cast_chunks))
    per_row = (2 * sum(nbytes(a.shape[1:], a.dtype) for a in flat_streamed)
               + 2 * (2 if split_out else 1) * d * 4 + dff * 2 + d * 4
               + 2 * sum(nbytes((w,), post_dtype) for w in post_widths))
    tm = 2 * TOKEN_TILE
    while math.gcd(*rows) % tm or fixed + tm * per_row > V7X_VMEM_LIMIT_BYTES:
        tm //= 2
    n_first = rows[0] // tm
    if cast and n // tm < CAST_STEPS:
        raise ValueError("too few grid steps for the weight rounding side job")

    def tiles(a):
        if isinstance(a, tuple):
            return list(zip(a, _stream_rows(tm, a[0].shape[1], n_first)))
        return [(a, _rows(tm, a.shape[1]))]

    tiled = tiles(x)
    n_x = len(tiled)
    n_hg = 0
    if mix is not None:
        tiled += tiles(mix[0])
        n_hg = len(tiled) - n_x
    chunk_of = lambda i: jnp.minimum(i, CAST_STEPS - 1)
    for c, chunk in zip(cast, cast_chunks):
        arr, idx = _split(c)
        tiled.append((arr, pl.BlockSpec((None,) * len(idx) + chunk, lambda i, idx=idx: idx + (chunk_of(i), 0))))
    if post is not None:
        blocks = [t[0].shape[0] // tm for t in (tables_first, tables_second)]
        table_specs = (pl.BlockSpec((tm, LANES), lambda i: (jnp.minimum(i, n_first - 1) % blocks[0], 0)),
                       pl.BlockSpec((tm, LANES), lambda i: (jnp.maximum(i - n_first, 0) % blocks[1], 0)))
        for which in (0, 1):
            tiled += list(zip((tables_first[which], tables_second[which]), table_specs))
    args, specs = _operands(tiled, residents)
    if split_out:
        out_specs = list(_stream_rows(tm, d, n_first))
        out_shape = [jax.ShapeDtypeStruct((r, d), F32) for r in rows]
    else:
        out_specs, out_shape = [_rows(tm, d)], [jax.ShapeDtypeStruct((n, d), F32)]
    n_out = len(out_specs)
    for w in post_widths:
        out_specs.append(_rows(tm, w))
        out_shape.append(jax.ShapeDtypeStruct((n, w), post_dtype))
    n_post = len(post_widths)
    out_specs += [pl.BlockSpec(chunk, lambda i: (chunk_of(i), 0)) for chunk in cast_chunks]
    out_shape += [jax.ShapeDtypeStruct(_shape(c), BF16) for c in cast]
    kern = functools.partial(_ffn_kernel, n_x=n_x, n_hg=n_hg, n_cast=len(cast), final=final_gain is not None,
                             post=post[0] if post is not None else None, n_out=n_out, n_first=n_first, tf=tf)
    outs = pl.pallas_call(
        kern,
        grid=(n // tm,),
        in_specs=specs,
        out_specs=out_specs,
        out_shape=out_shape,
        scratch_shapes=[pltpu.VMEM((tm, dff), BF16)],
        compiler_params=_params("arbitrary"),
        name=name,
    )(*args)
    y = tuple(outs[:n_out]) if split_out else outs[0]
    return y, list(outs[n_out:n_out + n_post]), tuple(outs[n_out + n_post:])


def _inproj_kernel(x_ref, gn_ref, wqk_ref, wv_ref, wo_ref, wgt_ref, bg_ref,
                   q_ref, k_ref, v_ref, o_ref, g_ref, *, k_scale):
    h = _rms(x_ref[...], gn_ref[...]).astype(BF16)
    dqk = q_ref.shape[1]
    qk = _dot(h, wqk_ref[...])
    q_ref[...] = qk[:, :dqk].astype(q_ref.dtype)
    k_ref[...] = (qk[:, dqk:] * k_scale).astype(k_ref.dtype)
    v_ref[...] = _dot(h, wv_ref[...]).astype(v_ref.dtype)
    o_ref[...] = _dot(h, wo_ref[...])
    gates = _dot(h, wgt_ref[...]) + bg_ref[...]
    gates = GATE_CAP * jnp.tanh(gates / GATE_CAP)
    logsig = jnp.minimum(gates, 0.0) - jnp.log1p(jnp.exp(-jnp.abs(gates)))
    lane = lax.broadcasted_iota(jnp.int32, gates.shape, 1)
    g_ref[...] = jnp.where(lane < A_HEADS, gates, logsig)


def _window_tile(row0, n):
    return _token_tile(math.gcd(row0, n))


def _window(xw):
    arr, row0, n = xw
    tm = _window_tile(row0, n)
    blk0 = row0 // tm
    return n, tm, (arr, pl.BlockSpec((tm, arr.shape[1]), lambda i: (i + blk0, 0)))


def _inproj(xw, gn, wqk, wv, wo, wgt, bg, act_dtype, name):
    n, tm, x_op = _window(xw)
    dqk = _shape(wqk)[1] // 2
    dv = _shape(wv)[1]
    kern = functools.partial(_inproj_kernel, k_scale=(dqk // A_HEADS) ** -0.5)
    args, specs = _operands([x_op], [gn, wqk, wv, wo, wgt, bg])
    return pl.pallas_call(
        kern,
        grid=(n // tm,),
        in_specs=specs,
        out_specs=[_rows(tm, dqk), _rows(tm, dqk), _rows(tm, dv), _rows(tm, dv), _rows(tm, LANES)],
        out_shape=[jax.ShapeDtypeStruct((n, dqk), act_dtype), jax.ShapeDtypeStruct((n, dqk), act_dtype),
                   jax.ShapeDtypeStruct((n, dv), act_dtype), jax.ShapeDtypeStruct((n, dv), F32),
                   jax.ShapeDtypeStruct((n, LANES), F32)],
        compiler_params=_params("parallel"),
        name=name,
    )(*args)


def _rope(x, cos, sin):
    width = x.shape[1]
    reps = width // LANES
    cosw = jnp.concatenate([cos] * reps, axis=1)
    sinw = jnp.concatenate([sin] * reps, axis=1)
    half = ROT_DIM // 2
    lane = lax.broadcasted_iota(jnp.int32, (1, width), 1)
    first_half = (lane & (B_HEAD_DIM - 1)) < half
    partner = jnp.where(first_half, pltpu.roll(x, width - half, 1), pltpu.roll(x, half, 1))
    return x * cosw + partner * sinw


def _mlstm_kernel(q_ref, k_ref, v_ref, o_ref, gtm_ref, gt_ref, hn_ref, c0_ref, n0_ref, m0_ref, *rest,
                  bblk, chunk):
    out_ref, c_ref, n_ref, m_ref = rest[-4:]
    L = chunk
    dqk = q_ref.shape[2] // A_HEADS
    dv = v_ref.shape[2] // A_HEADS

    @pl.when(pl.program_id(1) == 0)
    def _():
        c_ref[...] = c0_ref[...]
        n_ref[...] = n0_ref[...]
        m_ref[...] = m0_ref[...]

    row = lax.broadcasted_iota(jnp.int32, (L, L), 0)
    col = lax.broadcasted_iota(jnp.int32, (L, L), 1)
    tril = col <= row
    lower_ones = tril.astype(F32)
    upper_ones = (row <= col).astype(F32)

    chains = [(b, h) for b in range(bblk) for h in range(A_HEADS)]
    per_chain = lambda f: jnp.stack([f(b, h) for b, h in chains])

    gts = [gt_ref[b, 0] for b in range(bblk)]
    gtms = [gtm_ref[b] for b in range(bblk)]
    csum_rows = [_dot_f32(gt, upper_ones) for gt in gts]
    csum_cols = [_dot_f32(lower_ones, gtm) for gtm in gtms]
    i_row = per_chain(lambda b, h: gts[b][h:h + 1, :])
    b_row = per_chain(lambda b, h: csum_rows[b][A_HEADS + h:A_HEADS + h + 1, :])
    i_col = per_chain(lambda b, h: gtms[b][:, h:h + 1])
    b_col = per_chain(lambda b, h: csum_cols[b][:, A_HEADS + h:A_HEADS + h + 1])
    b_last = b_col[:, L - 1:L, :]
    m_prev = per_chain(lambda b, h: m_ref[b, h:h + 1, :])
    n_prev = per_chain(lambda b, h: n_ref[b, h:h + 1, :])
    c_prev = per_chain(lambda b, h: c_ref[b, h])

    q = per_chain(lambda b, h: q_ref[b, :, h * dqk:(h + 1) * dqk]).astype(BF16)
    k = per_chain(lambda b, h: k_ref[b, :, h * dqk:(h + 1) * dqk]).astype(BF16)
    v = per_chain(lambda b, h: v_ref[b, :, h * dv:(h + 1) * dv]).astype(BF16)

    dmat = jnp.where(tril[None], (b_col - b_row) + i_row, -jnp.inf)
    g = b_col + m_prev
    mt = jnp.maximum(g, jnp.max(dmat, axis=-1, keepdims=True))
    w_inter = jnp.exp(g - mt)
    s = jnp.einsum("gtd,gsd->gts", q, k, preferred_element_type=F32) * jnp.exp(dmat - mt)
    G = len(chains)
    v_ones = jnp.concatenate([v, jnp.ones((G, L, LANES), BF16)], axis=-1)
    n_cols = jnp.broadcast_to(jnp.swapaxes(n_prev, 1, 2), (G, dqk, LANES))
    c_n = jnp.concatenate([c_prev, n_cols], axis=-1).astype(BF16)
    intra = jnp.einsum("gts,gsv->gtv", s.astype(BF16), v_ones, preferred_element_type=F32)
    inter = w_inter * jnp.einsum("gtd,gdv->gtv", q, c_n, preferred_element_type=F32)
    both = intra + inter
    den = jnp.maximum(jnp.abs(both[:, :, dv:]), jnp.exp(-mt))
    hc = both[:, :, :dv] / jnp.concatenate([den] * (dv // LANES), axis=-1)

    m_new = mt[:, L - 1:L, :]
    a_col = jnp.exp(((b_last - b_col) + i_col) - m_new)
    decay = jnp.exp((b_last + m_prev) - m_new)
    ka = k.astype(F32) * a_col
    kat = jnp.swapaxes(ka, 1, 2).astype(BF16)
    c_new = decay * c_prev + jnp.einsum("gds,gsv->gdv", kat, v, preferred_element_type=F32)
    n_new = decay * n_prev + jnp.sum(ka, axis=1, keepdims=True)

    gain = per_chain(lambda b, h: hn_ref[:, h * dv:(h + 1) * dv])
    gate = jax.nn.sigmoid(per_chain(lambda b, h: o_ref[b, :, h * dv:(h + 1) * dv]))
    res = (_rms(hc, gain) * gate).astype(out_ref.dtype)
    for gi, (b, h) in enumerate(chains):
        c_ref[b, h] = c_new[gi]
        n_ref[b, h:h + 1, :] = n_new[gi]
        m_ref[b, h:h + 1, :] = m_new[gi]
        out_ref[b, :, h * dv:(h + 1) * dv] = res[gi]


def _mlstm(q, k, v, o, gates, head_norm, state0, layer0, layer, n_layers, prev_final, bblk, out_dtype, name):
    B, S, qw = q.shape
    vw = v.shape[2]
    L = MLSTM_CHUNK if S % MLSTM_CHUNK == 0 else S
    nc = S // L
    ng = 2 * A_HEADS
    gt = jnp.swapaxes(gates[:, :, :ng].reshape(B, nc, L, ng), 2, 3)
    kern = functools.partial(_mlstm_kernel, bblk=bblk, chunk=L)
    seq = lambda w: pl.BlockSpec((bblk, L, w), lambda i, c: (i, c, 0))

    def state(a, row):
        return pl.BlockSpec((None, bblk) + a.shape[2:], lambda i, c: (row, i) + (0,) * (a.ndim - 2))

    hn_arr, hn_spec = _resident(head_norm)
    args = [q, k, v, o, gates, gt, hn_arr, *state0]
    specs = [seq(qw), seq(qw), seq(vw), seq(vw), seq(LANES),
             pl.BlockSpec((bblk, 1, ng, L), lambda i, c: (i, c, 0, 0)), hn_spec,
             *[state(a, layer0) for a in state0]]
    aliases = {}
    if prev_final is not None:
        aliases = {len(args) + t: 1 + t for t in range(len(prev_final))}
        args += list(prev_final)
        specs += [pl.BlockSpec(memory_space=pl.ANY)] * len(prev_final)
    return pl.pallas_call(
        kern,
        grid=(B // bblk, nc),
        in_specs=specs,
        out_specs=[seq(vw)] + [state(a, layer) for a in state0],
        out_shape=[jax.ShapeDtypeStruct((B, S, vw), out_dtype)]
                  + [jax.ShapeDtypeStruct((n_layers,) + a.shape[1:], F32) for a in state0],
        input_output_aliases=aliases,
        compiler_params=_params("parallel", "arbitrary"),
        name=name,
    )(*args)


def _pair_block_diag(x):
    zero = jnp.zeros_like(x)
    return jnp.concatenate([jnp.concatenate([x, zero], axis=1), jnp.concatenate([zero, x], axis=1)], axis=0)


def _swa_prompt_kernel(sinks_ref, q_ref, kp_ref, kc_ref, vp_ref, vc_ref, o_ref, *, group):
    W = WINDOW
    hd = B_HEAD_DIM
    pairs = group // 2
    R = pairs * W
    nblk = q_ref.shape[0] // W
    has_prev = pl.program_id(1) > 0
    t = lax.broadcasted_iota(jnp.int32, (R, 2 * W), 0) & (W - 1)
    s = lax.broadcasted_iota(jnp.int32, (R, 2 * W), 1) & (W - 1)
    use_cur = s <= t
    visible = jnp.logical_or(use_cur, has_prev)
    row = lax.broadcasted_iota(jnp.int32, (R, 1), 0)
    first_head = lax.broadcasted_iota(jnp.int32, (R, 2 * hd), 1) < hd
    ones = jnp.ones((W, hd), BF16)

    def key_blocks(ref_prev, ref_cur, blk, ksl):
        cur = ref_cur[blk * W:(blk + 1) * W, ksl]
        prev = ref_prev[:, ksl] if blk == 0 else ref_cur[(blk - 1) * W:blk * W, ksl]
        return prev.astype(BF16), cur.astype(BF16)

    def scores(item):
        blk, kh = item
        ksl = slice(kh * hd, (kh + 1) * hd)
        rows = slice(blk * W, (blk + 1) * W)
        q2 = jnp.concatenate(
            [q_ref[rows, (kh * pairs + pr) * 2 * hd:(kh * pairs + pr + 1) * 2 * hd] for pr in range(pairs)],
            axis=0).astype(BF16)
        k_prev, k_cur = key_blocks(kp_ref, kc_ref, blk, ksl)
        sc_cur = _dot_nt(q2, _pair_block_diag(k_cur))
        sc_prev = _dot_nt(q2, _pair_block_diag(k_prev))
        sc = jnp.where(use_cur, sc_cur, sc_prev)
        return jnp.where(visible, sc, -jnp.inf) if blk == 0 else sc

    items = [(blk, kh) for blk in range(nblk) for kh in range(B_KV_HEADS)]
    sc_next = scores(items[0])
    for idx, (blk, kh) in enumerate(items):
        ksl = slice(kh * hd, (kh + 1) * hd)
        sc = sc_next
        if idx + 1 < len(items):
            sc_next = scores(items[idx + 1])
        probs, sink_terms = [], []
        for g in range(2):
            sink = jnp.full((R, 1), sinks_ref[0, kh * group + g], F32)
            for pr in range(1, pairs):
                sink = jnp.where(row >= pr * W, sinks_ref[0, kh * group + 2 * pr + g], sink)
            sg = sc[:, g * W:(g + 1) * W]
            mx = jnp.maximum(jnp.max(sg, axis=-1, keepdims=True), sink)
            probs.append(jnp.exp(sg - mx))
            sink_terms.append(jnp.exp(sink - mx))
        p = jnp.concatenate(probs, axis=1)
        p_cur = jnp.where(use_cur, p, 0.0).astype(BF16)
        p_prev = jnp.where(use_cur, 0.0, p).astype(BF16)
        v_prev, v_cur = key_blocks(vp_ref, vc_ref, blk, ksl)
        v_cur = jnp.concatenate([_pair_block_diag(v_cur), _pair_block_diag(ones)], axis=1)
        v_prev = jnp.concatenate([_pair_block_diag(v_prev), _pair_block_diag(ones)], axis=1)
        acc = _dot(p_cur, v_cur) + _dot(p_prev, v_prev)
        denom = acc[:, 2 * hd:] + jnp.where(first_head, sink_terms[0], sink_terms[1])
        out = acc[:, :2 * hd] / denom
        for pr in range(pairs):
            lo = (kh * pairs + pr) * 2 * hd
            o_ref[0, blk * W:(blk + 1) * W, lo:lo + 2 * hd] = out[pr * W:(pr + 1) * W, :].astype(o_ref.dtype)


def _swa_prompt(q, k, v, row0, B, S, sinks, out_dtype, name):
    qw, kw = q.shape[1], k.shape[1]
    nb = S // WINDOW
    group = qw // kw
    if group % 2 or 2 * B_HEAD_DIM != LANES:
        raise ValueError("the prompt attention kernel pairs query heads into LANES-wide slices")
    nblk = SWA_BLOCKS_PER_STEP if nb % SWA_BLOCKS_PER_STEP == 0 and row0 % (SWA_BLOCKS_PER_STEP * WINDOW) == 0 else 1
    if row0 % (nblk * WINDOW):
        raise ValueError("the stream must start on a query block boundary")
    steps = nb // nblk
    first = row0 // (nblk * WINDOW)
    cur = lambda w: pl.BlockSpec((nblk * WINDOW, w), lambda b, j: (first + b * steps + j, 0))
    prev = lambda w: pl.BlockSpec((WINDOW, w), lambda b, j: (jnp.maximum((first + b * steps + j) * nblk - 1, 0), 0))
    kern = functools.partial(_swa_prompt_kernel, group=group)
    return pl.pallas_call(
        kern,
        grid=(B, steps),
        in_specs=[pl.BlockSpec(memory_space=pltpu.SMEM), cur(qw), prev(kw), cur(kw), prev(kw), cur(kw)],
        out_specs=pl.BlockSpec((1, nblk * WINDOW, qw), lambda b, j: (b, j, 0)),
        out_shape=jax.ShapeDtypeStruct((B, S, qw), out_dtype),
        compiler_params=_params("parallel", "parallel"),
        name=name,
    )(sinks, q, k, k, v, v)


def _swa_sample_kernel(sinks_ref, q_ref, kc_ref, kn_ref, vc_ref, vn_ref, o_ref, *, group, bblk):
    T = q_ref.shape[0] // bblk
    wb, kw = kc_ref.shape[1:]
    hd = B_HEAD_DIM
    nq = B_KV_HEADS * group
    R = nq * T
    per_seq = lambda ref: ref[...].astype(F32).reshape(bblk, T, ref.shape[1])
    k_new, v_new = per_seq(kn_ref).astype(BF16), per_seq(vn_ref).astype(BF16)
    qb = per_seq(q_ref)
    blocks = []
    for kh in range(B_KV_HEADS):
        for g in range(group):
            lo = (kh * group + g) * hd
            parts = [qb[:, :, lo:lo + hd]]
            if kh:
                parts.insert(0, jnp.zeros((bblk, T, kh * hd), qb.dtype))
            if kh < B_KV_HEADS - 1:
                parts.append(jnp.zeros((bblk, T, (B_KV_HEADS - 1 - kh) * hd), qb.dtype))
            blocks.append(jnp.concatenate(parts, axis=-1))
    q_all = jnp.concatenate(blocks, axis=1).astype(BF16)

    t = lax.broadcasted_iota(jnp.int32, (R, wb), 0) % T
    i = lax.broadcasted_iota(jnp.int32, (R, wb), 1)
    mask_buf = jnp.logical_and((t + wb) - i < WINDOW, (PAST_LEN - wb) + i >= 0)
    tn = lax.broadcasted_iota(jnp.int32, (R, T), 0) % T
    u = lax.broadcasted_iota(jnp.int32, (R, T), 1)
    mask_new = u <= tn
    ridx = lax.broadcasted_iota(jnp.int32, (R, 1), 0)
    sink = jnp.full((R, 1), sinks_ref[0, 0], F32)
    for hq in range(1, nq):
        sink = jnp.where(ridx >= hq * T, sinks_ref[0, hq], sink)

    sc_buf = jnp.einsum("bqd,bkd->bqk", q_all, kc_ref[...].astype(BF16), preferred_element_type=F32)
    sc_new = jnp.einsum("bqd,bkd->bqk", q_all, k_new, preferred_element_type=F32)
    sc_buf = jnp.where(mask_buf[None], sc_buf, -jnp.inf)
    sc_new = jnp.where(mask_new[None], sc_new, -jnp.inf)
    mx = jnp.maximum(jnp.maximum(jnp.max(sc_buf, axis=-1, keepdims=True),
                                 jnp.max(sc_new, axis=-1, keepdims=True)), sink[None])
    p_buf = jnp.exp(sc_buf - mx)
    p_new = jnp.exp(sc_new - mx)
    denom = (jnp.sum(p_buf, axis=-1, keepdims=True) + jnp.sum(p_new, axis=-1, keepdims=True)
             + jnp.exp(sink[None] - mx))
    acc = (jnp.einsum("bqk,bkd->bqd", p_buf.astype(BF16), vc_ref[...].astype(BF16), preferred_element_type=F32)
           + jnp.einsum("bqk,bkd->bqd", p_new.astype(BF16), v_new, preferred_element_type=F32))
    out = (acc / denom).astype(o_ref.dtype)
    for kh in range(B_KV_HEADS):
        for g in range(group):
            hq = kh * group + g
            o_ref[:, :, hq * hd:(hq + 1) * hd] = out[:, hq * T:(hq + 1) * T, kh * hd:(kh + 1) * hd]


def _swa_sample(q, k_buf, k_new, v_buf, v_new, row0, T, sinks, bblk, name):
    qw = q.shape[1]
    B, wb, kw = k_buf.shape
    group = qw // kw
    if row0 % (bblk * T):
        raise ValueError("the stream must start on a sequence block boundary")
    first = row0 // (bblk * T)
    blk = lambda r, w: pl.BlockSpec((bblk, r, w), lambda i: (i, 0, 0))
    new = lambda w: pl.BlockSpec((bblk * T, w), lambda i: (first + i, 0))
    kern = functools.partial(_swa_sample_kernel, group=group, bblk=bblk)
    return pl.pallas_call(
        kern,
        grid=(B // bblk,),
        in_specs=[pl.BlockSpec(memory_space=pltpu.SMEM), new(qw), blk(wb, kw), new(kw), blk(wb, kw), new(kw)],
        out_specs=blk(T, qw),
        out_shape=jax.ShapeDtypeStruct((B, T, qw), BF16),
        compiler_params=_params("parallel"),
        name=name,
    )(sinks, q, k_buf, k_new, v_buf, v_new)


def _rope_tables(pos):
    half = ROT_DIM // 2
    inv_freq = jnp.exp(-math.log(ROPE_THETA) * jnp.arange(0, ROT_DIM, 2, dtype=F32) / ROT_DIM)
    ang = pos.astype(F32)[:, None] * inv_freq[None, :]
    d = jnp.arange(LANES, dtype=jnp.int32) % B_HEAD_DIM
    ang_l = jnp.take(ang, d % half, axis=1)
    rot = (d < ROT_DIM)[None, :]
    sign = jnp.where(d < half, -1.0, 1.0)[None, :]
    cos = jnp.where(rot, jnp.cos(ang_l), 1.0)
    sin = jnp.where(rot, jnp.sin(ang_l) * sign, 0.0)
    return cos.astype(F32), sin.astype(F32)


def _largest_divisor(n, cap):
    d = min(n, cap)
    while n % d:
        d -= 1
    return d


def _forward(streams, p):
    n_a = p["a_wqk"].shape[0]
    depth = p["ffn_norm"].shape[0]
    rows = tuple(st["x"].shape[0] * st["x"].shape[1] for st in streams)
    x = tuple(st["x"].reshape(n, -1) for st, n in zip(streams, rows))

    def window(i):
        if isinstance(x, tuple):
            return (x[i], 0, rows[i])
        return (x, sum(rows[:i]), rows[i])

    ffn_order = [(l, half) for l in range(depth) for half in (0, 1)]
    ffn_w = tuple(w[0, 0].astype(BF16) for w in p["ffn_w32"])

    def ffn_after(l, half):
        k = ffn_order.index((l, half)) + 1
        return tuple(_at(w, *ffn_order[k]) for w in p["ffn_w32"]) if k < len(ffn_order) else ()

    final_state = [None for _ in streams]
    kv_new = None
    if not 0 < n_a < depth:
        raise ValueError("expected mLSTM layers followed by attention layers")
    tables = [st["tables"] for st in streams]
    q_tables = [tuple(t * B_HEAD_DIM ** -0.5 for t in tb) for tb in tables]
    for l in range(depth):
        post = None
        if l >= n_a:
            post = ("q", _at(p["mix_norm"], l), _at(p["b_w_q"], l - n_a), _at(p["b_b_q"], l - n_a),
                    *q_tables, BF16)
        x, posts, ffn_w = _ffn(x, rows, _at(p["ffn_norm"], l, 0), *ffn_w, cast=ffn_after(l, 0), post=post,
                               name=f"ffn_{l}a")
        mixed = []
        for i, st in enumerate(streams):
            B, T, _ = st["x"].shape
            tag, act = st["tag"], st["dtype"]
            seq = lambda a: a.reshape(B, T, a.shape[-1])
            if l < n_a:
                q, k, v, o, gates = _inproj(window(i), _at(p["mix_norm"], l), _at(p["a_wqk"], l),
                                            _at(p["a_wv"], l), _at(p["a_wo"], l), _at(p["a_wgt"], l),
                                            _at(p["a_bg"], l), act, name=f"inproj_{tag}_{l}")
                bblk = _largest_divisor(B, 2 if T >= MLSTM_CHUNK else 8)
                layer0 = l if st["state"][0].shape[0] == n_a else 0
                hg, *final_state[i] = _mlstm(seq(q), seq(k), seq(v), seq(o), seq(gates),
                                             _at(p["a_head_norm"], l), st["state"], layer0, l, n_a,
                                             final_state[i], bblk, BF16, name=f"mlstm_{tag}_{l}")
                mixed.append(hg.reshape(B * T, -1))
            else:
                j = l - n_a
                row0 = sum(rows[:i])
                if st["cache"] is None:
                    o = _swa_prompt(posts[0], *kv_new, row0, B, T, p["b_sinks"][j], BF16, name=f"swa_{tag}_{l}")
                else:
                    kc, vc = st["cache"]
                    o = _swa_sample(posts[0], kc.reshape(B, kc.shape[1], -1), kv_new[0],
                                    vc.reshape(B, vc.shape[1], -1), kv_new[1], row0, T, p["b_sinks"][j],
                                    _largest_divisor(B, 8), name=f"swa_{tag}_{l}")
                mixed.append(o.reshape(B * T, -1))
        if l < n_a:
            mix = (tuple(mixed), _at(p["a_w_out"], l), p["zero_bias"])
        else:
            mix = (tuple(mixed), _at(p["b_w_o"], l - n_a), _at(p["b_b_o"], l - n_a))
        last = l == depth - 1
        post = ("kv", p["kv_norm"], p["w_kv"], p["b_kv"], *tables, F32) if l == n_a - 1 else None
        x, posts, ffn_w = _ffn(x, rows, _at(p["ffn_norm"], l, 1), *ffn_w, mix=mix, cast=ffn_after(l, 1),
                               post=post, final_gain=p["final_norm"] if last else None, split_out=last,
                               name=f"ffn_{l}b")
        if post is not None:
            kv_new = posts

    results = []
    for i, st in enumerate(streams):
        B, T, D = st["x"].shape
        new_c, new_n, new_m = final_state[i]
        row0 = sum(rows[:i])
        k_new, v_new = (a[row0:row0 + rows[i]].reshape(B, T, -1) for a in kv_new)
        if st["cache"] is None:
            wb = min(WINDOW, T)
            buf_k, buf_v = k_new[:, -wb:], v_new[:, -wb:]
        else:
            wb = st["cache"][0].shape[1]
            flat = lambda a: a.reshape(B, a.shape[1], -1)
            buf_k = jnp.concatenate([flat(st["cache"][0]), k_new], axis=1)[:, -wb:]
            buf_v = jnp.concatenate([flat(st["cache"][1]), v_new], axis=1)[:, -wb:]
        heads = lambda a: a.reshape(B, wb, B_KV_HEADS, B_HEAD_DIM)
        results.append((x[i].reshape(B, T, D), new_c, new_n, new_m[..., 0], heads(buf_k), heads(buf_v)))
    return results


def kernel(x_prompt, x_sample, state_mlstm_C, state_mlstm_n, state_mlstm_m, cache_swa_k, cache_swa_v,
           ffn_norm, ffn_w_gate, ffn_w_up, ffn_w_down, mix_norm, a_w_in, a_b_gate, a_head_norm, a_w_out,
           kv_norm, w_kv, b_kv, b_w_q, b_b_q, b_sinks, b_w_o, b_b_o, final_norm):
    n_a, d_model, a_proj = a_w_in.shape
    ng = 2 * A_HEADS
    qk_w = d_model // 2
    v_w = d_model
    if a_proj != 2 * qk_w + 2 * v_w + ng:
        raise ValueError("unexpected mLSTM projection width")
    row = lambda a: a[..., None, :].astype(F32)
    p = {
        "ffn_norm": row(ffn_norm), "mix_norm": row(mix_norm), "kv_norm": row(kv_norm),
        "final_norm": row(final_norm), "a_head_norm": row(a_head_norm),
        "ffn_w32": (ffn_w_gate, ffn_w_up, ffn_w_down),
        "a_wqk": a_w_in[:, :, :2 * qk_w].astype(BF16),
        "a_wv": a_w_in[:, :, 2 * qk_w:2 * qk_w + v_w].astype(BF16),
        "a_wo": a_w_in[:, :, 2 * qk_w + v_w:2 * qk_w + 2 * v_w].astype(BF16),
        "a_wgt": jnp.pad(a_w_in[:, :, 2 * qk_w + 2 * v_w:], ((0, 0), (0, 0), (0, LANES - ng))).astype(BF16),
        "a_bg": jnp.pad(a_b_gate, ((0, 0), (0, LANES - ng)))[:, None, :].astype(F32),
        "a_w_out": a_w_out.astype(BF16),
        "zero_bias": jnp.zeros((1, d_model), F32),
        "w_kv": w_kv.astype(BF16), "b_kv": row(b_kv),
        "b_w_q": b_w_q.astype(BF16), "b_b_q": row(b_b_q), "b_sinks": row(b_sinks),
        "b_w_o": b_w_o.astype(BF16), "b_b_o": row(b_b_o),
    }
    Bp, Sp, _ = x_prompt.shape
    Bs, Ts, _ = x_sample.shape
    zc = jnp.zeros((1, Bp) + state_mlstm_C.shape[2:], F32)
    zn = jnp.zeros((1, Bp) + state_mlstm_n.shape[2:], F32)
    zm = jnp.zeros((1, Bp) + state_mlstm_m.shape[2:] + (1,), F32)
    s_state = (state_mlstm_C, state_mlstm_n, state_mlstm_m[..., None])

    p_rows = math.lcm(Sp, 2 * TOKEN_TILE)
    p_tables = _rope_tables(jnp.arange(p_rows, dtype=jnp.int32) % Sp)
    s_rows = math.lcm(Ts, 2 * TOKEN_TILE)
    s_tables = _rope_tables(PAST_LEN + jnp.arange(s_rows, dtype=jnp.int32) % Ts)

    prompt = dict(x=x_prompt, tables=p_tables, state=(zc, zn, zm), cache=None, tag="p", dtype=BF16)
    sample = dict(x=x_sample, tables=s_tables, state=s_state, cache=(cache_swa_k, cache_swa_v), tag="s",
                  dtype=F32)
    (y_p, p_c, p_n, p_m, p_k, p_v), (y_s, s_c, s_n, s_m, s_k, s_v) = _forward([prompt, sample], p)
    return (y_p, y_s, p_c, p_n, p_m, p_k, p_v, s_c, s_n, s_m, s_k, s_v)
```

```python
import functools
import math

import jax
import jax.numpy as jnp
from jax import lax
from jax.experimental import pallas as pl
from jax.experimental.pallas import tpu as pltpu

F32 = jnp.float32
BF16 = jnp.bfloat16

EPS = 1e-6
GATE_CAP = 15.0
A_HEADS = 4
MLSTM_CHUNK = 256
B_HEAD_DIM = 64
B_KV_HEADS = 4
WINDOW = 128
ROT_DIM = B_HEAD_DIM // 4
ROPE_THETA = 500000.0
PAST_LEN = 8192

LANES = 128
V7X_VMEM_LIMIT_BYTES = 60 * 1024 * 1024
TOKEN_TILE = 512
FFN_SUBTILE = 512
FF_TILE = 256
CAST_STEPS = 16
SWA_BLOCKS_PER_STEP = 4


def _rms(x, g):
    return (x * lax.rsqrt(jnp.mean(x * x, axis=-1, keepdims=True) + EPS)) * g


def _dot(a, b):
    return jnp.dot(a, b, preferred_element_type=F32)


def _dot_nt(a, b):
    return lax.dot_general(a, b, (((1,), (1,)), ((), ())), preferred_element_type=F32)


def _dot_f32(a, b):
    return jnp.dot(a, b, preferred_element_type=F32, precision=lax.Precision.HIGHEST)


def _at(arr, *idx):
    return (arr, idx)


def _split(param):
    return param if isinstance(param, tuple) else (param, ())


def _shape(param):
    arr, idx = _split(param)
    return arr.shape[len(idx):]


def _resident(param):
    arr, idx = _split(param)
    block = (None,) * len(idx) + arr.shape[len(idx):]
    index = idx + (0,) * (arr.ndim - len(idx))
    return arr, pl.BlockSpec(block, lambda *_: index, pipeline_mode=pl.Buffered(1))


def _operands(tiled, residents):
    pairs = list(tiled) + [_resident(r) for r in residents]
    return [a for a, _ in pairs], [sp for _, sp in pairs]


def _rows(tm, width):
    return pl.BlockSpec((tm, width), lambda i: (i, 0))


def _params(*sem):
    return pltpu.CompilerParams(dimension_semantics=sem, vmem_limit_bytes=V7X_VMEM_LIMIT_BYTES)


def _token_tile(n):
    tm = TOKEN_TILE
    while n % tm:
        tm //= 2
    return tm


def _ffn_kernel(*refs, n_x, n_hg, n_cast, final, post, n_out, n_first, tf):
    refs = list(refs)
    x_refs = [refs.pop(0) for _ in range(n_x)]
    hg_refs = [refs.pop(0) for _ in range(n_hg)]
    cast_src = [refs.pop(0) for _ in range(n_cast)]
    if post:
        cos_refs, sin_refs = [refs.pop(0), refs.pop(0)], [refs.pop(0), refs.pop(0)]
    if n_hg:
        wmix_ref, bmix_ref = refs.pop(0), refs.pop(0)
    gn_ref, wg_ref, wu_ref, wd_ref = refs.pop(0), refs.pop(0), refs.pop(0), refs.pop(0)
    if final:
        fg_ref = refs.pop(0)
    if post:
        pgn_ref, pw_ref, pb_ref = refs.pop(0), refs.pop(0), refs.pop(0)
    o_refs = [refs.pop(0) for _ in range(n_out)]
    post_refs = [refs.pop(0) for _ in range({None: 0, "q": 1, "kv": 2}[post])]
    cast_dst = [refs.pop(0) for _ in range(n_cast)]
    (a_ref,) = refs
    first = pl.program_id(0) < n_first

    if n_cast:
        @pl.when(pl.program_id(0) < CAST_STEPS)
        def _():
            for src, dst in zip(cast_src, cast_dst):
                dst[...] = src[...].astype(dst.dtype)

    tm = a_ref.shape[0]
    sub = min(tm, FFN_SUBTILE)
    for r0 in range(0, tm, sub):
        rows = slice(r0, r0 + sub)

        def current(rs, dtype):
            vals = [r[rows, :].astype(dtype) for r in rs]
            return vals[0] if len(vals) == 1 else jnp.where(first, vals[0], vals[1])

        x = current(x_refs, F32)
        if n_hg:
            x = x + (_dot(current(hg_refs, BF16), wmix_ref[...]) + bmix_ref[...])
        h = _rms(x, gn_ref[...]).astype(BF16)
        for j in range(wg_ref.shape[1] // tf):
            sl = slice(j * tf, (j + 1) * tf)
            g = _dot(h, wg_ref[:, sl])
            u = _dot(h, wu_ref[:, sl])
            a_ref[rows, sl] = ((g * jax.nn.sigmoid(g)) * u).astype(BF16)
        out = x + 0.5 * _dot(a_ref[rows, :], wd_ref[...])
        if post:
            proj = _dot(_rms(out, pgn_ref[...]).astype(BF16), pw_ref[...]) + pb_ref[...]
            cos, sin = current(cos_refs, F32), current(sin_refs, F32)
            if post == "q":
                post_refs[0][rows, :] = _rope(proj, cos, sin).astype(post_refs[0].dtype)
            else:
                kw = proj.shape[1] // 2
                post_refs[0][rows, :] = _rope(proj[:, :kw], cos, sin).astype(post_refs[0].dtype)
                post_refs[1][rows, :] = proj[:, kw:].astype(post_refs[1].dtype)
        if final:
            out = _rms(out, fg_ref[...])
        if n_out == 1:
            o_refs[0][rows, :] = out
        else:
            @pl.when(first)
            def _():
                o_refs[0][rows, :] = out

            @pl.when(jnp.logical_not(first))
            def _():
                o_refs[1][rows, :] = out


def _stream_rows(tm, width, n_first):
    return (pl.BlockSpec((tm, width), lambda i: (jnp.minimum(i, n_first - 1), 0)),
            pl.BlockSpec((tm, width), lambda i: (jnp.maximum(i - n_first, 0), 0)))


def _ffn(x, rows, gn, wg, wu, wd, mix=None, final_gain=None, split_out=False, cast=(), post=None, name="ffn"):
    n = sum(rows)
    d = _shape(gn)[1]
    dff = _shape(wg)[1]
    tf = FF_TILE if dff % FF_TILE == 0 else dff
    residents = ([mix[1], mix[2]] if mix is not None else []) + [gn, wg, wu, wd]
    if final_gain is not None:
        residents.append(final_gain)
    streamed = [x] + ([mix[0]] if mix is not None else [])
    post_widths = []
    if post is not None:
        kind, pgn, pw, pb, tables_first, tables_second, post_dtype = post
        residents += [pgn, pw, pb]
        streamed += [(tables_first[0], tables_second[0]), (tables_first[1], tables_second[1])]
        post_widths = [_shape(pw)[1]] if kind == "q" else [_shape(pw)[1] // 2] * 2
    flat_streamed = [a for s in streamed for a in (s if isinstance(s, tuple) else (s,))]
    cast_chunks = [(_shape(c)[0] // CAST_STEPS, _shape(c)[1]) for c in cast]

    nbytes = lambda shape, dtype: math.prod(shape) * jnp.dtype(dtype).itemsize
    fixed = (sum(nbytes(_shape(r), _split(r)[0].dtype) for r in residents)
             + 2 * sum(nbytes(c, F32) + nbytes(c, BF16) for c in cast_chunks))
    per_row = (2 * sum(nbytes(a.shape[1:], a.dtype) for a in flat_streamed)
               + 2 * (2 if split_out else 1) * d * 4 + dff * 2 + d * 4
               + 2 * sum(nbytes((w,), post_dtype) for w in post_widths))
    tm = 2 * TOKEN_TILE
    while math.gcd(*rows) % tm or fixed + tm * per_row > V7X_VMEM_LIMIT_BYTES:
        tm //= 2
    n_first = rows[0] // tm
    if cast and n // tm < CAST_STEPS:
        raise ValueError("too few grid steps for the weight rounding side job")

    def tiles(a):
        if isinstance(a, tuple):
            return list(zip(a, _stream_rows(tm, a[0].shape[1], n_first)))
        return [(a, _rows(tm, a.shape[1]))]

    tiled = tiles(x)
    n_x = len(tiled)
    n_hg = 0
    if mix is not None:
        tiled += tiles(mix[0])
        n_hg = len(tiled) - n_x
    chunk_of = lambda i: jnp.minimum(i, CAST_STEPS - 1)
    for c, chunk in zip(cast, cast_chunks):
        arr, idx = _split(c)
        tiled.append((arr, pl.BlockSpec((None,) * len(idx) + chunk, lambda i, idx=idx: idx + (chunk_of(i), 0))))
    if post is not None:
        blocks = [t[0].shape[0] // tm for t in (tables_first, tables_second)]
        table_specs = (pl.BlockSpec((tm, LANES), lambda i: (jnp.minimum(i, n_first - 1) % blocks[0], 0)),
                       pl.BlockSpec((tm, LANES), lambda i: (jnp.maximum(i - n_first, 0) % blocks[1], 0)))
        for which in (0, 1):
            tiled += list(zip((tables_first[which], tables_second[which]), table_specs))
    args, specs = _operands(tiled, residents)
    if split_out:
        out_specs = list(_stream_rows(tm, d, n_first))
        out_shape = [jax.ShapeDtypeStruct((r, d), F32) for r in rows]
    else:
        out_specs, out_shape = [_rows(tm, d)], [jax.ShapeDtypeStruct((n, d), F32)]
    n_out = len(out_specs)
    for w in post_widths:
        out_specs.append(_rows(tm, w))
        out_shape.append(jax.ShapeDtypeStruct((n, w), post_dtype))
    n_post = len(post_widths)
    out_specs += [pl.BlockSpec(chunk, lambda i: (chunk_of(i), 0)) for chunk in cast_chunks]
    out_shape += [jax.ShapeDtypeStruct(_shape(c), BF16) for c in cast]
    kern = functools.partial(_ffn_kernel, n_x=n_x, n_hg=n_hg, n_cast=len(cast), final=final_gain is not None,
                             post=post[0] if post is not None else None, n_out=n_out, n_first=n_first, tf=tf)
    outs = pl.pallas_call(
        kern,
        grid=(n // tm,),
        in_specs=specs,
        out_specs=out_specs,
        out_shape=out_shape,
        scratch_shapes=[pltpu.VMEM((tm, dff), BF16)],
        compiler_params=_params("arbitrary"),
        name=name,
    )(*args)
    y = tuple(outs[:n_out]) if split_out else outs[0]
    return y, list(outs[n_out:n_out + n_post]), tuple(outs[n_out + n_post:])


def _inproj_kernel(x_ref, gn_ref, wqk_ref, wv_ref, wo_ref, wgt_ref, bg_ref,
                   q_ref, k_ref, v_ref, o_ref, g_ref, *gt_refs, k_scale):
    h = _rms(x_ref[...], gn_ref[...]).astype(BF16)
    dqk = q_ref.shape[1]
    qk = _dot(h, wqk_ref[...])
    q_ref[...] = qk[:, :dqk].astype(q_ref.dtype)
    k_ref[...] = (qk[:, dqk:] * k_scale).astype(k_ref.dtype)
    v_ref[...] = _dot(h, wv_ref[...]).astype(v_ref.dtype)
    o_ref[...] = _dot(h, wo_ref[...])
    gates = _dot(h, wgt_ref[...]) + bg_ref[...]
    gates = GATE_CAP * jnp.tanh(gates / GATE_CAP)
    logsig = jnp.minimum(gates, 0.0) - jnp.log1p(jnp.exp(-jnp.abs(gates)))
    lane = lax.broadcasted_iota(jnp.int32, gates.shape, 1)
    gates = jnp.where(lane < A_HEADS, gates, logsig)
    g_ref[...] = gates
    for gt_ref in gt_refs:
        chunks, ng, L = gt_ref.shape
        for c in range(chunks):
            gt_ref[c] = gates[c * L:(c + 1) * L, :].T[:ng, :]


def _mlstm_chunk(S):
    return MLSTM_CHUNK if S % MLSTM_CHUNK == 0 else S


def _window_tile(row0, n):
    return _token_tile(math.gcd(row0, n))


def _window(xw):
    arr, row0, n = xw
    tm = _window_tile(row0, n)
    blk0 = row0 // tm
    return n, tm, (arr, pl.BlockSpec((tm, arr.shape[1]), lambda i: (i + blk0, 0)))


def _inproj(xw, seq_len, gn, wqk, wv, wo, wgt, bg, act_dtype, name):
    n, tm, x_op = _window(xw)
    dqk = _shape(wqk)[1] // 2
    dv = _shape(wv)[1]
    kern = functools.partial(_inproj_kernel, k_scale=(dqk // A_HEADS) ** -0.5)
    args, specs = _operands([x_op], [gn, wqk, wv, wo, wgt, bg])
    out_specs = [_rows(tm, dqk), _rows(tm, dqk), _rows(tm, dv), _rows(tm, dv), _rows(tm, LANES)]
    out_shape = [jax.ShapeDtypeStruct((n, dqk), act_dtype), jax.ShapeDtypeStruct((n, dqk), act_dtype),
                 jax.ShapeDtypeStruct((n, dv), act_dtype), jax.ShapeDtypeStruct((n, dv), F32),
                 jax.ShapeDtypeStruct((n, LANES), F32)]
    L = _mlstm_chunk(seq_len)
    with_gt = L % LANES == 0 and tm % L == 0
    if with_gt:
        out_specs.append(pl.BlockSpec((tm // L, 2 * A_HEADS, L), lambda i: (i, 0, 0)))
        out_shape.append(jax.ShapeDtypeStruct((n // L, 2 * A_HEADS, L), F32))
    outs = pl.pallas_call(
        kern,
        grid=(n // tm,),
        in_specs=specs,
        out_specs=out_specs,
        out_shape=out_shape,
        compiler_params=_params("parallel"),
        name=name,
    )(*args)
    return tuple(outs[:5]) + ((outs[5],) if with_gt else (None,))


def _rope(x, cos, sin):
    width = x.shape[1]
    reps = width // LANES
    cosw = jnp.concatenate([cos] * reps, axis=1)
    sinw = jnp.concatenate([sin] * reps, axis=1)
    half = ROT_DIM // 2
    lane = lax.broadcasted_iota(jnp.int32, (1, width), 1)
    first_half = (lane & (B_HEAD_DIM - 1)) < half
    partner = jnp.where(first_half, pltpu.roll(x, width - half, 1), pltpu.roll(x, half, 1))
    return x * cosw + partner * sinw


def _mlstm_kernel(q_ref, k_ref, v_ref, o_ref, gtm_ref, gt_ref, hn_ref, c0_ref, n0_ref, m0_ref, *rest,
                  bblk, chunk):
    out_ref, c_ref, n_ref, m_ref = rest[-4:]
    L = chunk
    dqk = q_ref.shape[2] // A_HEADS
    dv = v_ref.shape[2] // A_HEADS

    @pl.when(pl.program_id(1) == 0)
    def _():
        c_ref[...] = c0_ref[...]
        n_ref[...] = n0_ref[...]
        m_ref[...] = m0_ref[...]

    row = lax.broadcasted_iota(jnp.int32, (L, L), 0)
    col = lax.broadcasted_iota(jnp.int32, (L, L), 1)
    tril = col <= row
    lower_ones = tril.astype(F32)
    upper_ones = (row <= col).astype(F32)

    chains = [(b, h) for b in range(bblk) for h in range(A_HEADS)]
    per_chain = lambda f: jnp.stack([f(b, h) for b, h in chains])

    gts = [gt_ref[b, 0] for b in range(bblk)]
    gtms = [gtm_ref[b] for b in range(bblk)]
    csum_rows = [_dot_f32(gt, upper_ones) for gt in gts]
    csum_cols = [_dot_f32(lower_ones, gtm) for gtm in gtms]
    i_row = per_chain(lambda b, h: gts[b][h:h + 1, :])
    b_row = per_chain(lambda b, h: csum_rows[b][A_HEADS + h:A_HEADS + h + 1, :])
    i_col = per_chain(lambda b, h: gtms[b][:, h:h + 1])
    b_col = per_chain(lambda b, h: csum_cols[b][:, A_HEADS + h:A_HEADS + h + 1])
    b_last = b_col[:, L - 1:L, :]
    m_prev = per_chain(lambda b, h: m_ref[b, h:h + 1, :])
    n_prev = per_chain(lambda b, h: n_ref[b, h:h + 1, :])
    c_prev = per_chain(lambda b, h: c_ref[b, h])

    q = per_chain(lambda b, h: q_ref[b, :, h * dqk:(h + 1) * dqk]).astype(BF16)
    k = per_chain(lambda b, h: k_ref[b, :, h * dqk:(h + 1) * dqk]).astype(BF16)
    v = per_chain(lambda b, h: v_ref[b, :, h * dv:(h + 1) * dv]).astype(BF16)

    dmat = jnp.where(tril[None], (b_col - b_row) + i_row, -jnp.inf)
    g = b_col + m_prev
    mt = jnp.maximum(g, jnp.max(dmat, axis=-1, keepdims=True))
    w_inter = jnp.exp(g - mt)
    s = jnp.einsum("gtd,gsd->gts", q, k, preferred_element_type=F32) * jnp.exp(dmat - mt)
    G = len(chains)
    v_ones = jnp.concatenate([v, jnp.ones((G, L, LANES), BF16)], axis=-1)
    n_cols = jnp.broadcast_to(jnp.swapaxes(n_prev, 1, 2), (G, dqk, LANES))
    c_n = jnp.concatenate([c_prev, n_cols], axis=-1).astype(BF16)
    intra = jnp.einsum("gts,gsv->gtv", s.astype(BF16), v_ones, preferred_element_type=F32)
    inter = w_inter * jnp.einsum("gtd,gdv->gtv", q, c_n, preferred_element_type=F32)
    both = intra + inter
    den = jnp.maximum(jnp.abs(both[:, :, dv:]), jnp.exp(-mt))
    hc = both[:, :, :dv] / jnp.concatenate([den] * (dv // LANES), axis=-1)

    m_new = mt[:, L - 1:L, :]
    a_col = jnp.exp(((b_last - b_col) + i_col) - m_new)
    decay = jnp.exp((b_last + m_prev) - m_new)
    ka = k.astype(F32) * a_col
    kat = jnp.swapaxes(ka, 1, 2).astype(BF16)
    c_new = decay * c_prev + jnp.einsum("gds,gsv->gdv", kat, v, preferred_element_type=F32)
    n_new = decay * n_prev + jnp.sum(ka, axis=1, keepdims=True)

    gain = per_chain(lambda b, h: hn_ref[:, h * dv:(h + 1) * dv])
    gate = jax.nn.sigmoid(per_chain(lambda b, h: o_ref[b, :, h * dv:(h + 1) * dv]))
    res = (_rms(hc, gain) * gate).astype(out_ref.dtype)
    for gi, (b, h) in enumerate(chains):
        c_ref[b, h] = c_new[gi]
        n_ref[b, h:h + 1, :] = n_new[gi]
        m_ref[b, h:h + 1, :] = m_new[gi]
        out_ref[b, :, h * dv:(h + 1) * dv] = res[gi]


def _mlstm(q, k, v, o, gates, gt, head_norm, state0, layer0, layer, n_layers, prev_final, bblk, out_dtype,
           name):
    B, S, qw = q.shape
    vw = v.shape[2]
    L = _mlstm_chunk(S)
    nc = S // L
    ng = 2 * A_HEADS
    if gt is None:
        gt = jnp.swapaxes(gates[:, :, :ng].reshape(B, nc, L, ng), 2, 3)
    gt = gt.reshape(B, nc, ng, L)
    kern = functools.partial(_mlstm_kernel, bblk=bblk, chunk=L)
    seq = lambda w: pl.BlockSpec((bblk, L, w), lambda i, c: (i, c, 0))

    def state(a, row):
        return pl.BlockSpec((None, bblk) + a.shape[2:], lambda i, c: (row, i) + (0,) * (a.ndim - 2))

    hn_arr, hn_spec = _resident(head_norm)
    args = [q, k, v, o, gates, gt, hn_arr, *state0]
    specs = [seq(qw), seq(qw), seq(vw), seq(vw), seq(LANES),
             pl.BlockSpec((bblk, 1, ng, L), lambda i, c: (i, c, 0, 0)), hn_spec,
             *[state(a, layer0) for a in state0]]
    aliases = {}
    if prev_final is not None:
        aliases = {len(args) + t: 1 + t for t in range(len(prev_final))}
        args += list(prev_final)
        specs += [pl.BlockSpec(memory_space=pl.ANY)] * len(prev_final)
    return pl.pallas_call(
        kern,
        grid=(B // bblk, nc),
        in_specs=specs,
        out_specs=[seq(vw)] + [state(a, layer) for a in state0],
        out_shape=[jax.ShapeDtypeStruct((B, S, vw), out_dtype)]
                  + [jax.ShapeDtypeStruct((n_layers,) + a.shape[1:], F32) for a in state0],
        input_output_aliases=aliases,
        compiler_params=_params("parallel", "arbitrary"),
        name=name,
    )(*args)


def _pair_block_diag(x):
    zero = jnp.zeros_like(x)
    return jnp.concatenate([jnp.concatenate([x, zero], axis=1), jnp.concatenate([zero, x], axis=1)], axis=0)


def _swa_prompt_kernel(sinks_ref, q_ref, kp_ref, kc_ref, vp_ref, vc_ref, o_ref, *, group):
    W = WINDOW
    hd = B_HEAD_DIM
    pairs = group // 2
    R = pairs * W
    nblk = q_ref.shape[0] // W
    has_prev = pl.program_id(1) > 0
    t = lax.broadcasted_iota(jnp.int32, (R, 2 * W), 0) & (W - 1)
    s = lax.broadcasted_iota(jnp.int32, (R, 2 * W), 1) & (W - 1)
    use_cur = s <= t
    visible = jnp.logical_or(use_cur, has_prev)
    row = lax.broadcasted_iota(jnp.int32, (R, 1), 0)
    first_head = lax.broadcasted_iota(jnp.int32, (R, 2 * hd), 1) < hd
    ones = jnp.ones((W, hd), BF16)

    def key_blocks(ref_prev, ref_cur, blk, ksl):
        cur = ref_cur[blk * W:(blk + 1) * W, ksl]
        prev = ref_prev[:, ksl] if blk == 0 else ref_cur[(blk - 1) * W:blk * W, ksl]
        return prev.astype(BF16), cur.astype(BF16)

    def scores(item):
        blk, kh = item
        ksl = slice(kh * hd, (kh + 1) * hd)
        rows = slice(blk * W, (blk + 1) * W)
        q2 = jnp.concatenate(
            [q_ref[rows, (kh * pairs + pr) * 2 * hd:(kh * pairs + pr + 1) * 2 * hd] for pr in range(pairs)],
            axis=0).astype(BF16)
        k_prev, k_cur = key_blocks(kp_ref, kc_ref, blk, ksl)
        sc_cur = _dot_nt(q2, _pair_block_diag(k_cur))
        sc_prev = _dot_nt(q2, _pair_block_diag(k_prev))
        sc = jnp.where(use_cur, sc_cur, sc_prev)
        return jnp.where(visible, sc, -jnp.inf) if blk == 0 else sc

    def softmax(item, sc):
        _, kh = item
        probs, sink_terms = [], []
        for g in range(2):
            sink = jnp.full((R, 1), sinks_ref[0, kh * group + g], F32)
            for pr in range(1, pairs):
                sink = jnp.where(row >= pr * W, sinks_ref[0, kh * group + 2 * pr + g], sink)
            sg = sc[:, g * W:(g + 1) * W]
            mx = jnp.maximum(jnp.max(sg, axis=-1, keepdims=True), sink)
            probs.append(jnp.exp(sg - mx))
            sink_terms.append(jnp.exp(sink - mx))
        p = jnp.concatenate(probs, axis=1)
        p_cur = jnp.where(use_cur, p, 0.0).astype(BF16)
        p_prev = jnp.where(use_cur, 0.0, p).astype(BF16)
        return p_cur, p_prev, jnp.where(first_head, sink_terms[0], sink_terms[1])

    def weighted_values(item, p_cur, p_prev, sink_term):
        blk, kh = item
        v_prev, v_cur = key_blocks(vp_ref, vc_ref, blk, slice(kh * hd, (kh + 1) * hd))
        v_cur = jnp.concatenate([_pair_block_diag(v_cur), _pair_block_diag(ones)], axis=1)
        v_prev = jnp.concatenate([_pair_block_diag(v_prev), _pair_block_diag(ones)], axis=1)
        acc = _dot(p_cur, v_cur) + _dot(p_prev, v_prev)
        out = acc[:, :2 * hd] / (acc[:, 2 * hd:] + sink_term)
        for pr in range(pairs):
            lo = (kh * pairs + pr) * 2 * hd
            o_ref[0, blk * W:(blk + 1) * W, lo:lo + 2 * hd] = out[pr * W:(pr + 1) * W, :].astype(o_ref.dtype)

    items = [(blk, kh) for blk in range(nblk) for kh in range(B_KV_HEADS)]
    sc = {i: scores(items[i]) for i in range(min(2, len(items)))}
    probs = {0: softmax(items[0], sc.pop(0))}
    for i, item in enumerate(items):
        if i + 2 < len(items):
            sc[i + 2] = scores(items[i + 2])
        if i + 1 < len(items):
            probs[i + 1] = softmax(items[i + 1], sc.pop(i + 1))
        weighted_values(item, *probs.pop(i))


def _swa_prompt(q, k, v, row0, B, S, sinks, out_dtype, name):
    qw, kw = q.shape[1], k.shape[1]
    nb = S // WINDOW
    group = qw // kw
    if group % 2 or 2 * B_HEAD_DIM != LANES:
        raise ValueError("the prompt attention kernel pairs query heads into LANES-wide slices")
    nblk = SWA_BLOCKS_PER_STEP if nb % SWA_BLOCKS_PER_STEP == 0 and row0 % (SWA_BLOCKS_PER_STEP * WINDOW) == 0 else 1
    if row0 % (nblk * WINDOW):
        raise ValueError("the stream must start on a query block boundary")
    steps = nb // nblk
    first = row0 // (nblk * WINDOW)
    cur = lambda w: pl.BlockSpec((nblk * WINDOW, w), lambda b, j: (first + b * steps + j, 0))
    prev = lambda w: pl.BlockSpec((WINDOW, w), lambda b, j: (jnp.maximum((first + b * steps + j) * nblk - 1, 0), 0))
    kern = functools.partial(_swa_prompt_kernel, group=group)
    return pl.pallas_call(
        kern,
        grid=(B, steps),
        in_specs=[pl.BlockSpec(memory_space=pltpu.SMEM), cur(qw), prev(kw), cur(kw), prev(kw), cur(kw)],
        out_specs=pl.BlockSpec((1, nblk * WINDOW, qw), lambda b, j: (b, j, 0)),
        out_shape=jax.ShapeDtypeStruct((B, S, qw), out_dtype),
        compiler_params=_params("parallel", "parallel"),
        name=name,
    )(sinks, q, k, k, v, v)


def _swa_sample_kernel(sinks_ref, q_ref, kc_ref, kn_ref, vc_ref, vn_ref, o_ref, *, group, bblk):
    T = q_ref.shape[0] // bblk
    wb, kw = kc_ref.shape[1:]
    hd = B_HEAD_DIM
    nq = B_KV_HEADS * group
    R = nq * T
    per_seq = lambda ref: ref[...].astype(F32).reshape(bblk, T, ref.shape[1])
    k_new, v_new = per_seq(kn_ref).astype(BF16), per_seq(vn_ref).astype(BF16)
    qb = per_seq(q_ref)
    blocks = []
    for kh in range(B_KV_HEADS):
        for g in range(group):
            lo = (kh * group + g) * hd
            parts = [qb[:, :, lo:lo + hd]]
            if kh:
                parts.insert(0, jnp.zeros((bblk, T, kh * hd), qb.dtype))
            if kh < B_KV_HEADS - 1:
                parts.append(jnp.zeros((bblk, T, (B_KV_HEADS - 1 - kh) * hd), qb.dtype))
            blocks.append(jnp.concatenate(parts, axis=-1))
    q_all = jnp.concatenate(blocks, axis=1).astype(BF16)

    t = lax.broadcasted_iota(jnp.int32, (R, wb), 0) % T
    i = lax.broadcasted_iota(jnp.int32, (R, wb), 1)
    mask_buf = jnp.logical_and((t + wb) - i < WINDOW, (PAST_LEN - wb) + i >= 0)
    tn = lax.broadcasted_iota(jnp.int32, (R, T), 0) % T
    u = lax.broadcasted_iota(jnp.int32, (R, T), 1)
    mask_new = u <= tn
    ridx = lax.broadcasted_iota(jnp.int32, (R, 1), 0)
    sink = jnp.full((R, 1), sinks_ref[0, 0], F32)
    for hq in range(1, nq):
        sink = jnp.where(ridx >= hq * T, sinks_ref[0, hq], sink)

    sc_buf = jnp.einsum("bqd,bkd->bqk", q_all, kc_ref[...].astype(BF16), preferred_element_type=F32)
    sc_new = jnp.einsum("bqd,bkd->bqk", q_all, k_new, preferred_element_type=F32)
    sc_buf = jnp.where(mask_buf[None], sc_buf, -jnp.inf)
    sc_new = jnp.where(mask_new[None], sc_new, -jnp.inf)
    mx = jnp.maximum(jnp.maximum(jnp.max(sc_buf, axis=-1, keepdims=True),
                                 jnp.max(sc_new, axis=-1, keepdims=True)), sink[None])
    p_buf = jnp.exp(sc_buf - mx)
    p_new = jnp.exp(sc_new - mx)
    denom = (jnp.sum(p_buf, axis=-1, keepdims=True) + jnp.sum(p_new, axis=-1, keepdims=True)
             + jnp.exp(sink[None] - mx))
    acc = (jnp.einsum("bqk,bkd->bqd", p_buf.astype(BF16), vc_ref[...].astype(BF16), preferred_element_type=F32)
           + jnp.einsum("bqk,bkd->bqd", p_new.astype(BF16), v_new, preferred_element_type=F32))
    out = (acc / denom).astype(o_ref.dtype)
    for kh in range(B_KV_HEADS):
        for g in range(group):
            hq = kh * group + g
            o_ref[:, :, hq * hd:(hq + 1) * hd] = out[:, hq * T:(hq + 1) * T, kh * hd:(kh + 1) * hd]


def _swa_sample(q, k_buf, k_new, v_buf, v_new, row0, T, sinks, bblk, name):
    qw = q.shape[1]
    B, wb, kw = k_buf.shape
    group = qw // kw
    if row0 % (bblk * T):
        raise ValueError("the stream must start on a sequence block boundary")
    first = row0 // (bblk * T)
    blk = lambda r, w: pl.BlockSpec((bblk, r, w), lambda i: (i, 0, 0))
    new = lambda w: pl.BlockSpec((bblk * T, w), lambda i: (first + i, 0))
    kern = functools.partial(_swa_sample_kernel, group=group, bblk=bblk)
    return pl.pallas_call(
        kern,
        grid=(B // bblk,),
        in_specs=[pl.BlockSpec(memory_space=pltpu.SMEM), new(qw), blk(wb, kw), new(kw), blk(wb, kw), new(kw)],
        out_specs=blk(T, qw),
        out_shape=jax.ShapeDtypeStruct((B, T, qw), BF16),
        compiler_params=_params("parallel"),
        name=name,
    )(sinks, q, k_buf, k_new, v_buf, v_new)


def _rope_tables(pos, scale=1.0):
    inv_freq = jnp.exp(-math.log(ROPE_THETA) * jnp.arange(0, ROT_DIM, 2, dtype=F32) / ROT_DIM)
    ang = pos.astype(F32)[:, None] * inv_freq[None, :]
    cos, sin = jnp.cos(ang) * scale, jnp.sin(ang) * scale
    rest = (pos.shape[0], B_HEAD_DIM - ROT_DIM)
    cos_head = [cos, cos, jnp.full(rest, scale, F32)]
    sin_head = [-sin, sin, jnp.zeros(rest, F32)]
    heads = LANES // B_HEAD_DIM
    return jnp.concatenate(cos_head * heads, axis=1), jnp.concatenate(sin_head * heads, axis=1)


def _largest_divisor(n, cap):
    d = min(n, cap)
    while n % d:
        d -= 1
    return d


def _forward(streams, p):
    n_a = p["a_wqk"].shape[0]
    depth = p["ffn_norm"].shape[0]
    rows = tuple(st["x"].shape[0] * st["x"].shape[1] for st in streams)
    x = tuple(st["x"].reshape(n, -1) for st, n in zip(streams, rows))

    def window(i):
        if isinstance(x, tuple):
            return (x[i], 0, rows[i])
        return (x, sum(rows[:i]), rows[i])

    ffn_order = [(l, half) for l in range(depth) for half in (0, 1)]
    ffn_w = tuple(w[0, 0].astype(BF16) for w in p["ffn_w32"])

    def ffn_after(l, half):
        k = ffn_order.index((l, half)) + 1
        return tuple(_at(w, *ffn_order[k]) for w in p["ffn_w32"]) if k < len(ffn_order) else ()

    final_state = [None for _ in streams]
    kv_new = None
    if not 0 < n_a < depth:
        raise ValueError("expected mLSTM layers followed by attention layers")
    tables = [_rope_tables(st["pos"]) for st in streams]
    q_tables = [_rope_tables(st["pos"], B_HEAD_DIM ** -0.5) for st in streams]
    for l in range(depth):
        post = None
        if l >= n_a:
            post = ("q", _at(p["mix_norm"], l), _at(p["b_w_q"], l - n_a), _at(p["b_b_q"], l - n_a),
                    *q_tables, BF16)
        x, posts, ffn_w = _ffn(x, rows, _at(p["ffn_norm"], l, 0), *ffn_w, cast=ffn_after(l, 0), post=post,
                               name=f"ffn_{l}a")
        mixed = []
        for i, st in enumerate(streams):
            B, T, _ = st["x"].shape
            tag, act = st["tag"], st["dtype"]
            seq = lambda a: a.reshape(B, T, a.shape[-1])
            if l < n_a:
                q, k, v, o, gates, gt = _inproj(window(i), T, _at(p["mix_norm"], l), _at(p["a_wqk"], l),
                                                _at(p["a_wv"], l), _at(p["a_wo"], l), _at(p["a_wgt"], l),
                                                _at(p["a_bg"], l), act, name=f"inproj_{tag}_{l}")
                bblk = _largest_divisor(B, 2 if T >= MLSTM_CHUNK else 8)
                layer0 = l if st["state"][0].shape[0] == n_a else 0
                hg, *final_state[i] = _mlstm(seq(q), seq(k), seq(v), seq(o), seq(gates), gt,
                                             _at(p["a_head_norm"], l), st["state"], layer0, l, n_a,
                                             final_state[i], bblk, BF16, name=f"mlstm_{tag}_{l}")
                mixed.append(hg.reshape(B * T, -1))
            else:
                j = l - n_a
                row0 = sum(rows[:i])
                if st["cache"] is None:
                    o = _swa_prompt(posts[0], *kv_new, row0, B, T, p["b_sinks"][j], BF16, name=f"swa_{tag}_{l}")
                else:
                    kc, vc = st["cache"]
                    o = _swa_sample(posts[0], kc.reshape(B, kc.shape[1], -1), kv_new[0],
                                    vc.reshape(B, vc.shape[1], -1), kv_new[1], row0, T, p["b_sinks"][j],
                                    _largest_divisor(B, 8), name=f"swa_{tag}_{l}")
                mixed.append(o.reshape(B * T, -1))
        if l < n_a:
            mix = (tuple(mixed), _at(p["a_w_out"], l), p["zero_bias"])
        else:
            mix = (tuple(mixed), _at(p["b_w_o"], l - n_a), _at(p["b_b_o"], l - n_a))
        last = l == depth - 1
        post = ("kv", p["kv_norm"], p["w_kv"], p["b_kv"], *tables, F32) if l == n_a - 1 else None
        x, posts, ffn_w = _ffn(x, rows, _at(p["ffn_norm"], l, 1), *ffn_w, mix=mix, cast=ffn_after(l, 1),
                               post=post, final_gain=p["final_norm"] if last else None, split_out=last,
                               name=f"ffn_{l}b")
        if post is not None:
            kv_new = posts

    results = []
    for i, st in enumerate(streams):
        B, T, D = st["x"].shape
        new_c, new_n, new_m = final_state[i]
        row0 = sum(rows[:i])
        k_new, v_new = (a[row0:row0 + rows[i]].reshape(B, T, -1) for a in kv_new)
        if st["cache"] is None:
            wb = min(WINDOW, T)
            buf_k, buf_v = k_new[:, -wb:], v_new[:, -wb:]
        else:
            wb = st["cache"][0].shape[1]
            flat = lambda a: a.reshape(B, a.shape[1], -1)
            buf_k = jnp.concatenate([flat(st["cache"][0]), k_new], axis=1)[:, -wb:]
            buf_v = jnp.concatenate([flat(st["cache"][1]), v_new], axis=1)[:, -wb:]
        heads = lambda a: a.reshape(B, wb, B_KV_HEADS, B_HEAD_DIM)
        results.append((x[i].reshape(B, T, D), new_c, new_n, new_m[..., 0], heads(buf_k), heads(buf_v)))
    return results


def kernel(x_prompt, x_sample, state_mlstm_C, state_mlstm_n, state_mlstm_m, cache_swa_k, cache_swa_v,
           ffn_norm, ffn_w_gate, ffn_w_up, ffn_w_down, mix_norm, a_w_in, a_b_gate, a_head_norm, a_w_out,
           kv_norm, w_kv, b_kv, b_w_q, b_b_q, b_sinks, b_w_o, b_b_o, final_norm):
    n_a, d_model, a_proj = a_w_in.shape
    ng = 2 * A_HEADS
    qk_w = d_model // 2
    v_w = d_model
    if a_proj != 2 * qk_w + 2 * v_w + ng:
        raise ValueError("unexpected mLSTM projection width")
    row = lambda a: a[..., None, :].astype(F32)
    p = {
        "ffn_norm": row(ffn_norm), "mix_norm": row(mix_norm), "kv_norm": row(kv_norm),
        "final_norm": row(final_norm), "a_head_norm": row(a_head_norm),
        "ffn_w32": (ffn_w_gate, ffn_w_up, ffn_w_down),
        "a_wqk": a_w_in[:, :, :2 * qk_w].astype(BF16),
        "a_wv": a_w_in[:, :, 2 * qk_w:2 * qk_w + v_w].astype(BF16),
        "a_wo": a_w_in[:, :, 2 * qk_w + v_w:2 * qk_w + 2 * v_w].astype(BF16),
        "a_wgt": jnp.pad(a_w_in[:, :, 2 * qk_w + 2 * v_w:], ((0, 0), (0, 0), (0, LANES - ng))).astype(BF16),
        "a_bg": jnp.pad(a_b_gate, ((0, 0), (0, LANES - ng)))[:, None, :].astype(F32),
        "a_w_out": a_w_out.astype(BF16),
        "zero_bias": jnp.zeros((1, d_model), F32),
        "w_kv": w_kv.astype(BF16), "b_kv": row(b_kv),
        "b_w_q": b_w_q.astype(BF16), "b_b_q": row(b_b_q), "b_sinks": row(b_sinks),
        "b_w_o": b_w_o.astype(BF16), "b_b_o": row(b_b_o),
    }
    Bp, Sp, _ = x_prompt.shape
    Bs, Ts, _ = x_sample.shape
    zc = jnp.zeros((1, Bp) + state_mlstm_C.shape[2:], F32)
    zn = jnp.zeros((1, Bp) + state_mlstm_n.shape[2:], F32)
    zm = jnp.zeros((1, Bp) + state_mlstm_m.shape[2:] + (1,), F32)
    s_state = (state_mlstm_C, state_mlstm_n, state_mlstm_m[..., None])

    p_pos = jnp.arange(math.lcm(Sp, 2 * TOKEN_TILE), dtype=jnp.int32) % Sp
    s_pos = PAST_LEN + jnp.arange(math.lcm(Ts, 2 * TOKEN_TILE), dtype=jnp.int32) % Ts

    prompt = dict(x=x_prompt, pos=p_pos, state=(zc, zn, zm), cache=None, tag="p", dtype=BF16)
    sample = dict(x=x_sample, pos=s_pos, state=s_state, cache=(cache_swa_k, cache_swa_v), tag="s", dtype=F32)
    (y_p, p_c, p_n, p_m, p_k, p_v), (y_s, s_c, s_n, s_m, s_k, s_v) = _forward([prompt, sample], p)
    return (y_p, y_s, p_c, p_n, p_m, p_k, p_v, s_c, s_n, s_m, s_k, s_v)
```

```python
import functools
import math

import jax
import jax.numpy as jnp
from jax import lax
from jax.experimental import pallas as pl
from jax.experimental.pallas import tpu as pltpu

F32 = jnp.float32
BF16 = jnp.bfloat16

EPS = 1e-6
GATE_CAP = 15.0
A_HEADS = 4
MLSTM_CHUNK = 256
B_HEAD_DIM = 64
B_KV_HEADS = 4
WINDOW = 128
ROT_DIM = B_HEAD_DIM // 4
ROPE_THETA = 500000.0
PAST_LEN = 8192

LANES = 128
V7X_VMEM_LIMIT_BYTES = 60 * 1024 * 1024
TOKEN_TILE = 512
FFN_SUBTILE = 512
FF_TILE = 256
CAST_STEPS = 16
SWA_BLOCKS_PER_STEP = 4


def _rms(x, g):
    return (x * lax.rsqrt(jnp.mean(x * x, axis=-1, keepdims=True) + EPS)) * g


def _dot(a, b):
    return jnp.dot(a, b, preferred_element_type=F32)


def _dot_nt(a, b):
    return lax.dot_general(a, b, (((1,), (1,)), ((), ())), preferred_element_type=F32)


def _dot_f32(a, b):
    return jnp.dot(a, b, preferred_element_type=F32, precision=lax.Precision.HIGHEST)


def _at(arr, *idx):
    return (arr, idx)


def _split(param):
    return param if isinstance(param, tuple) else (param, ())


def _shape(param):
    arr, idx = _split(param)
    return arr.shape[len(idx):]


def _resident(param):
    arr, idx = _split(param)
    block = (None,) * len(idx) + arr.shape[len(idx):]
    index = idx + (0,) * (arr.ndim - len(idx))
    return arr, pl.BlockSpec(block, lambda *_: index, pipeline_mode=pl.Buffered(1))


def _operands(tiled, residents):
    pairs = list(tiled) + [_resident(r) for r in residents]
    return [a for a, _ in pairs], [sp for _, sp in pairs]


def _rows(tm, width):
    return pl.BlockSpec((tm, width), lambda i: (i, 0))


def _params(*sem):
    return pltpu.CompilerParams(dimension_semantics=sem, vmem_limit_bytes=V7X_VMEM_LIMIT_BYTES)


def _token_tile(n):
    tm = TOKEN_TILE
    while n % tm:
        tm //= 2
    return tm


def _ffn_kernel(*refs, n_x, n_hg, n_cast, final, post, n_out, n_first, tf):
    refs = list(refs)
    x_refs = [refs.pop(0) for _ in range(n_x)]
    hg_refs = [refs.pop(0) for _ in range(n_hg)]
    cast_src = [refs.pop(0) for _ in range(n_cast)]
    if post:
        cos_refs, sin_refs = [refs.pop(0), refs.pop(0)], [refs.pop(0), refs.pop(0)]
    if n_hg:
        wmix_ref, bmix_ref = refs.pop(0), refs.pop(0)
    gn_ref, wg_ref, wu_ref, wd_ref = refs.pop(0), refs.pop(0), refs.pop(0), refs.pop(0)
    if final:
        fg_ref = refs.pop(0)
    if post:
        pgn_ref, pw_ref, pb_ref = refs.pop(0), refs.pop(0), refs.pop(0)
    o_refs = [refs.pop(0) for _ in range(n_out)]
    post_refs = [refs.pop(0) for _ in range({None: 0, "q": 1, "kv": 2}[post])]
    cast_dst = [refs.pop(0) for _ in range(n_cast)]
    (a_ref,) = refs
    first = pl.program_id(0) < n_first

    if n_cast:
        @pl.when(pl.program_id(0) < CAST_STEPS)
        def _():
            for src, dst in zip(cast_src, cast_dst):
                dst[...] = src[...].astype(dst.dtype)

    tm = a_ref.shape[0]
    sub = min(tm, FFN_SUBTILE)
    for r0 in range(0, tm, sub):
        rows = slice(r0, r0 + sub)

        def current(rs, dtype):
            vals = [r[rows, :].astype(dtype) for r in rs]
            return vals[0] if len(vals) == 1 else jnp.where(first, vals[0], vals[1])

        x = current(x_refs, F32)
        if n_hg:
            x = x + (_dot(current(hg_refs, BF16), wmix_ref[...]) + bmix_ref[...])
        h = _rms(x, gn_ref[...]).astype(BF16)
        for j in range(wg_ref.shape[1] // tf):
            sl = slice(j * tf, (j + 1) * tf)
            g = _dot(h, wg_ref[:, sl])
            u = _dot(h, wu_ref[:, sl])
            a_ref[rows, sl] = ((g * jax.nn.sigmoid(g)) * u).astype(BF16)
        out = x + 0.5 * _dot(a_ref[rows, :], wd_ref[...])
        if post:
            proj = _dot(_rms(out, pgn_ref[...]).astype(BF16), pw_ref[...]) + pb_ref[...]
            cos, sin = current(cos_refs, F32), current(sin_refs, F32)
            if post == "q":
                post_refs[0][rows, :] = (_rope(proj, cos, sin) * B_HEAD_DIM ** -0.5).astype(post_refs[0].dtype)
            else:
                kw = proj.shape[1] // 2
                post_refs[0][rows, :] = _rope(proj[:, :kw], cos, sin).astype(post_refs[0].dtype)
                post_refs[1][rows, :] = proj[:, kw:].astype(post_refs[1].dtype)
        if final:
            out = _rms(out, fg_ref[...])
        if n_out == 1:
            o_refs[0][rows, :] = out
        else:
            @pl.when(first)
            def _():
                o_refs[0][rows, :] = out

            @pl.when(jnp.logical_not(first))
            def _():
                o_refs[1][rows, :] = out


def _stream_rows(tm, width, n_first):
    return (pl.BlockSpec((tm, width), lambda i: (jnp.minimum(i, n_first - 1), 0)),
            pl.BlockSpec((tm, width), lambda i: (jnp.maximum(i - n_first, 0), 0)))


def _ffn(x, rows, gn, wg, wu, wd, mix=None, final_gain=None, split_out=False, cast=(), post=None, name="ffn"):
    n = sum(rows)
    d = _shape(gn)[1]
    dff = _shape(wg)[1]
    tf = FF_TILE if dff % FF_TILE == 0 else dff
    residents = ([mix[1], mix[2]] if mix is not None else []) + [gn, wg, wu, wd]
    if final_gain is not None:
        residents.append(final_gain)
    streamed = [x] + ([mix[0]] if mix is not None else [])
    post_widths = []
    if post is not None:
        kind, pgn, pw, pb, tables_first, tables_second, post_dtype = post
        residents += [pgn, pw, pb]
        streamed += [(tables_first[0], tables_second[0]), (tables_first[1], tables_second[1])]
        post_widths = [_shape(pw)[1]] if kind == "q" else [_shape(pw)[1] // 2] * 2
    flat_streamed = [a for s in streamed for a in (s if isinstance(s, tuple) else (s,))]
    cast_chunks = [(_shape(c)[0] // CAST_STEPS, _shape(c)[1]) for c in cast]

    nbytes = lambda shape, dtype: math.prod(shape) * jnp.dtype(dtype).itemsize
    fixed = (sum(nbytes(_shape(r), _split(r)[0].dtype) for r in residents)
             + 2 * sum(nbytes(c, F32) + nbytes(c, BF16) for c in cast_chunks))
    per_row = (2 * sum(nbytes(a.shape[1:], a.dtype) for a in flat_streamed)
               + 2 * (2 if split_out else 1) * d * 4 + dff * 2 + d * 4
               + 2 * sum(nbytes((w,), post_dtype) for w in post_widths))
    tm = 2 * TOKEN_TILE
    while math.gcd(*rows) % tm or fixed + tm * per_row > V7X_VMEM_LIMIT_BYTES:
        tm //= 2
    n_first = rows[0] // tm
    if cast and n // tm < CAST_STEPS:
        raise ValueError("too few grid steps for the weight rounding side job")

    def tiles(a):
        if isinstance(a, tuple):
            return list(zip(a, _stream_rows(tm, a[0].shape[1], n_first)))
        return [(a, _rows(tm, a.shape[1]))]

    tiled = tiles(x)
    n_x = len(tiled)
    n_hg = 0
    if mix is not None:
        tiled += tiles(mix[0])
        n_hg = len(tiled) - n_x
    chunk_of = lambda i: jnp.minimum(i, CAST_STEPS - 1)
    for c, chunk in zip(cast, cast_chunks):
        arr, idx = _split(c)
        tiled.append((arr, pl.BlockSpec((None,) * len(idx) + chunk, lambda i, idx=idx: idx + (chunk_of(i), 0))))
    if post is not None:
        blocks = [t[0].shape[0] // tm for t in (tables_first, tables_second)]
        table_specs = (pl.BlockSpec((tm, LANES), lambda i: (jnp.minimum(i, n_first - 1) % blocks[0], 0)),
                       pl.BlockSpec((tm, LANES), lambda i: (jnp.maximum(i - n_first, 0) % blocks[1], 0)))
        for which in (0, 1):
            tiled += list(zip((tables_first[which], tables_second[which]), table_specs))
    args, specs = _operands(tiled, residents)
    if split_out:
        out_specs = list(_stream_rows(tm, d, n_first))
        out_shape = [jax.ShapeDtypeStruct((r, d), F32) for r in rows]
    else:
        out_specs, out_shape = [_rows(tm, d)], [jax.ShapeDtypeStruct((n, d), F32)]
    n_out = len(out_specs)
    for w in post_widths:
        out_specs.append(_rows(tm, w))
        out_shape.append(jax.ShapeDtypeStruct((n, w), post_dtype))
    n_post = len(post_widths)
    out_specs += [pl.BlockSpec(chunk, lambda i: (chunk_of(i), 0)) for chunk in cast_chunks]
    out_shape += [jax.ShapeDtypeStruct(_shape(c), BF16) for c in cast]
    kern = functools.partial(_ffn_kernel, n_x=n_x, n_hg=n_hg, n_cast=len(cast), final=final_gain is not None,
                             post=post[0] if post is not None else None, n_out=n_out, n_first=n_first, tf=tf)
    outs = pl.pallas_call(
        kern,
        grid=(n // tm,),
        in_specs=specs,
        out_specs=out_specs,
        out_shape=out_shape,
        scratch_shapes=[pltpu.VMEM((tm, dff), BF16)],
        compiler_params=_params("arbitrary"),
        name=name,
    )(*args)
    y = tuple(outs[:n_out]) if split_out else outs[0]
    return y, list(outs[n_out:n_out + n_post]), tuple(outs[n_out + n_post:])


def _inproj_kernel(x_ref, gn_ref, w_ref, bg_ref, q_ref, k_ref, v_ref, o_ref, g_ref, *gt_refs, k_scale):
    h = _rms(x_ref[...], gn_ref[...]).astype(BF16)
    dqk, dv = q_ref.shape[1], v_ref.shape[1]
    qk = _dot(h, w_ref[:, :2 * dqk])
    q_ref[...] = qk[:, :dqk].astype(q_ref.dtype)
    k_ref[...] = (qk[:, dqk:] * k_scale).astype(k_ref.dtype)
    v_ref[...] = _dot(h, w_ref[:, 2 * dqk:2 * dqk + dv]).astype(v_ref.dtype)
    o_ref[...] = _dot(h, w_ref[:, 2 * dqk + dv:2 * dqk + 2 * dv])
    gates = _dot(h, w_ref[:, 2 * dqk + 2 * dv:]) + bg_ref[...]
    gates = GATE_CAP * jnp.tanh(gates / GATE_CAP)
    logsig = jnp.minimum(gates, 0.0) - jnp.log1p(jnp.exp(-jnp.abs(gates)))
    lane = lax.broadcasted_iota(jnp.int32, gates.shape, 1)
    gates = jnp.where(lane < A_HEADS, gates, logsig)
    g_ref[...] = gates
    for gt_ref in gt_refs:
        chunks, ng, L = gt_ref.shape
        for c in range(chunks):
            gt_ref[c] = gates[c * L:(c + 1) * L, :].T[:ng, :]


def _mlstm_chunk(S):
    return MLSTM_CHUNK if S % MLSTM_CHUNK == 0 else S


def _window_tile(row0, n):
    return _token_tile(math.gcd(row0, n))


def _window(xw):
    arr, row0, n = xw
    tm = _window_tile(row0, n)
    blk0 = row0 // tm
    return n, tm, (arr, pl.BlockSpec((tm, arr.shape[1]), lambda i: (i + blk0, 0)))


def _inproj(xw, seq_len, gn, w, bg, act_dtype, name):
    n, tm, x_op = _window(xw)
    dv = _shape(w)[0]
    dqk = dv // 2
    kern = functools.partial(_inproj_kernel, k_scale=(dqk // A_HEADS) ** -0.5)
    args, specs = _operands([x_op], [gn, w, bg])
    out_specs = [_rows(tm, dqk), _rows(tm, dqk), _rows(tm, dv), _rows(tm, dv), _rows(tm, LANES)]
    out_shape = [jax.ShapeDtypeStruct((n, dqk), act_dtype), jax.ShapeDtypeStruct((n, dqk), act_dtype),
                 jax.ShapeDtypeStruct((n, dv), act_dtype), jax.ShapeDtypeStruct((n, dv), F32),
                 jax.ShapeDtypeStruct((n, LANES), F32)]
    L = _mlstm_chunk(seq_len)
    with_gt = L % LANES == 0 and tm % L == 0
    if with_gt:
        out_specs.append(pl.BlockSpec((tm // L, 2 * A_HEADS, L), lambda i: (i, 0, 0)))
        out_shape.append(jax.ShapeDtypeStruct((n // L, 2 * A_HEADS, L), F32))
    outs = pl.pallas_call(
        kern,
        grid=(n // tm,),
        in_specs=specs,
        out_specs=out_specs,
        out_shape=out_shape,
        compiler_params=_params("parallel"),
        name=name,
    )(*args)
    return tuple(outs[:5]) + ((outs[5],) if with_gt else (None,))


def _rope(x, cos, sin):
    width = x.shape[1]
    reps = width // LANES
    cosw = jnp.concatenate([cos] * reps, axis=1)
    sinw = jnp.concatenate([sin] * reps, axis=1)
    half = ROT_DIM // 2
    lane = lax.broadcasted_iota(jnp.int32, (1, width), 1)
    first_half = (lane & (B_HEAD_DIM - 1)) < half
    partner = jnp.where(first_half, pltpu.roll(x, width - half, 1), pltpu.roll(x, half, 1))
    return x * cosw + partner * sinw


def _mlstm_kernel(q_ref, k_ref, v_ref, o_ref, gtm_ref, gt_ref, hn_ref, c0_ref, n0_ref, m0_ref, *rest,
                  bblk, chunk):
    out_ref, c_ref, n_ref, m_ref = rest[-4:]
    L = chunk
    dqk = q_ref.shape[2] // A_HEADS
    dv = v_ref.shape[2] // A_HEADS

    @pl.when(pl.program_id(1) == 0)
    def _():
        c_ref[...] = c0_ref[...]
        n_ref[...] = n0_ref[...]
        m_ref[...] = m0_ref[...]

    row = lax.broadcasted_iota(jnp.int32, (L, L), 0)
    col = lax.broadcasted_iota(jnp.int32, (L, L), 1)
    tril = col <= row
    lower_ones = tril.astype(F32)
    upper_ones = (row <= col).astype(F32)

    chains = [(b, h) for b in range(bblk) for h in range(A_HEADS)]
    per_chain = lambda f: jnp.stack([f(b, h) for b, h in chains])

    gts = [gt_ref[b, 0] for b in range(bblk)]
    gtms = [gtm_ref[b] for b in range(bblk)]
    csum_rows = [_dot_f32(gt, upper_ones) for gt in gts]
    csum_cols = [_dot_f32(lower_ones, gtm) for gtm in gtms]
    i_row = per_chain(lambda b, h: gts[b][h:h + 1, :])
    b_row = per_chain(lambda b, h: csum_rows[b][A_HEADS + h:A_HEADS + h + 1, :])
    i_col = per_chain(lambda b, h: gtms[b][:, h:h + 1])
    b_col = per_chain(lambda b, h: csum_cols[b][:, A_HEADS + h:A_HEADS + h + 1])
    b_last = b_col[:, L - 1:L, :]
    m_prev = per_chain(lambda b, h: m_ref[b, h:h + 1, :])
    n_prev = per_chain(lambda b, h: n_ref[b, h:h + 1, :])
    c_prev = per_chain(lambda b, h: c_ref[b, h])

    q = per_chain(lambda b, h: q_ref[b, :, h * dqk:(h + 1) * dqk]).astype(BF16)
    k = per_chain(lambda b, h: k_ref[b, :, h * dqk:(h + 1) * dqk]).astype(BF16)
    v = per_chain(lambda b, h: v_ref[b, :, h * dv:(h + 1) * dv]).astype(BF16)

    dmat = jnp.where(tril[None], (b_col - b_row) + i_row, -jnp.inf)
    g = b_col + m_prev
    mt = jnp.maximum(g, jnp.max(dmat, axis=-1, keepdims=True))
    w_inter = jnp.exp(g - mt)
    s = jnp.einsum("gtd,gsd->gts", q, k, preferred_element_type=F32) * jnp.exp(dmat - mt)
    G = len(chains)
    v_ones = jnp.concatenate([v, jnp.ones((G, L, LANES), BF16)], axis=-1)
    n_cols = jnp.broadcast_to(jnp.swapaxes(n_prev, 1, 2), (G, dqk, LANES))
    c_n = jnp.concatenate([c_prev, n_cols], axis=-1).astype(BF16)
    intra = jnp.einsum("gts,gsv->gtv", s.astype(BF16), v_ones, preferred_element_type=F32)
    inter = w_inter * jnp.einsum("gtd,gdv->gtv", q, c_n, preferred_element_type=F32)
    both = intra + inter
    den = jnp.maximum(jnp.abs(both[:, :, dv:]), jnp.exp(-mt))
    hc = both[:, :, :dv] / jnp.concatenate([den] * (dv // LANES), axis=-1)

    m_new = mt[:, L - 1:L, :]
    a_col = jnp.exp(((b_last - b_col) + i_col) - m_new)
    decay = jnp.exp((b_last + m_prev) - m_new)
    ka = k.astype(F32) * a_col
    kat = jnp.swapaxes(ka, 1, 2).astype(BF16)
    c_new = decay * c_prev + jnp.einsum("gds,gsv->gdv", kat, v, preferred_element_type=F32)
    n_new = decay * n_prev + jnp.sum(ka, axis=1, keepdims=True)

    gain = per_chain(lambda b, h: hn_ref[:, h * dv:(h + 1) * dv])
    gate = jax.nn.sigmoid(per_chain(lambda b, h: o_ref[b, :, h * dv:(h + 1) * dv]))
    res = (_rms(hc, gain) * gate).astype(out_ref.dtype)
    for gi, (b, h) in enumerate(chains):
        c_ref[b, h] = c_new[gi]
        n_ref[b, h:h + 1, :] = n_new[gi]
        m_ref[b, h:h + 1, :] = m_new[gi]
        out_ref[b, :, h * dv:(h + 1) * dv] = res[gi]


def _mlstm(q, k, v, o, gates, gt, head_norm, state0, layer0, layer, n_layers, prev_final, bblk, out_dtype,
           name):
    B, S, qw = q.shape
    vw = v.shape[2]
    L = _mlstm_chunk(S)
    nc = S // L
    ng = 2 * A_HEADS
    if gt is None:
        gt = jnp.swapaxes(gates[:, :, :ng].reshape(B, nc, L, ng), 2, 3)
    gt = gt.reshape(B, nc, ng, L)
    kern = functools.partial(_mlstm_kernel, bblk=bblk, chunk=L)
    seq = lambda w: pl.BlockSpec((bblk, L, w), lambda i, c: (i, c, 0))

    def state(a, row):
        return pl.BlockSpec((None, bblk) + a.shape[2:], lambda i, c: (row, i) + (0,) * (a.ndim - 2))

    hn_arr, hn_spec = _resident(head_norm)
    args = [q, k, v, o, gates, gt, hn_arr, *state0]
    specs = [seq(qw), seq(qw), seq(vw), seq(vw), seq(LANES),
             pl.BlockSpec((bblk, 1, ng, L), lambda i, c: (i, c, 0, 0)), hn_spec,
             *[state(a, layer0) for a in state0]]
    aliases = {}
    if prev_final is not None:
        aliases = {len(args) + t: 1 + t for t in range(len(prev_final))}
        args += list(prev_final)
        specs += [pl.BlockSpec(memory_space=pl.ANY)] * len(prev_final)
    return pl.pallas_call(
        kern,
        grid=(B // bblk, nc),
        in_specs=specs,
        out_specs=[seq(vw)] + [state(a, layer) for a in state0],
        out_shape=[jax.ShapeDtypeStruct((B, S, vw), out_dtype)]
                  + [jax.ShapeDtypeStruct((n_layers,) + a.shape[1:], F32) for a in state0],
        input_output_aliases=aliases,
        compiler_params=_params("parallel", "arbitrary"),
        name=name,
    )(*args)


def _pair_block_diag(x):
    zero = jnp.zeros_like(x)
    return jnp.concatenate([jnp.concatenate([x, zero], axis=1), jnp.concatenate([zero, x], axis=1)], axis=0)


def _swa_prompt_kernel(sinks_ref, q_ref, kp_ref, kc_ref, vp_ref, vc_ref, o_ref, *, group):
    W = WINDOW
    hd = B_HEAD_DIM
    pairs = group // 2
    R = pairs * W
    nblk = q_ref.shape[0] // W
    has_prev = pl.program_id(1) > 0
    t = lax.broadcasted_iota(jnp.int32, (R, 2 * W), 0) & (W - 1)
    s = lax.broadcasted_iota(jnp.int32, (R, 2 * W), 1) & (W - 1)
    use_cur = s <= t
    visible = jnp.logical_or(use_cur, has_prev)
    row = lax.broadcasted_iota(jnp.int32, (R, 1), 0)
    first_head = lax.broadcasted_iota(jnp.int32, (R, 2 * hd), 1) < hd
    ones = jnp.ones((W, hd), BF16)

    def key_blocks(ref_prev, ref_cur, blk, ksl):
        cur = ref_cur[blk * W:(blk + 1) * W, ksl]
        prev = ref_prev[:, ksl] if blk == 0 else ref_cur[(blk - 1) * W:blk * W, ksl]
        return prev.astype(BF16), cur.astype(BF16)

    def scores(item):
        blk, kh = item
        ksl = slice(kh * hd, (kh + 1) * hd)
        rows = slice(blk * W, (blk + 1) * W)
        q2 = jnp.concatenate(
            [q_ref[rows, (kh * pairs + pr) * 2 * hd:(kh * pairs + pr + 1) * 2 * hd] for pr in range(pairs)],
            axis=0).astype(BF16)
        k_prev, k_cur = key_blocks(kp_ref, kc_ref, blk, ksl)
        sc_cur = _dot_nt(q2, _pair_block_diag(k_cur))
        sc_prev = _dot_nt(q2, _pair_block_diag(k_prev))
        sc = jnp.where(use_cur, sc_cur, sc_prev)
        return jnp.where(visible, sc, -jnp.inf) if blk == 0 else sc

    def softmax(item, sc):
        _, kh = item
        probs, sink_terms = [], []
        for g in range(2):
            sink = jnp.full((R, 1), sinks_ref[0, kh * group + g], F32)
            for pr in range(1, pairs):
                sink = jnp.where(row >= pr * W, sinks_ref[0, kh * group + 2 * pr + g], sink)
            sg = sc[:, g * W:(g + 1) * W]
            mx = jnp.maximum(jnp.max(sg, axis=-1, keepdims=True), sink)
            probs.append(jnp.exp(sg - mx))
            sink_terms.append(jnp.exp(sink - mx))
        p = jnp.concatenate(probs, axis=1)
        p_cur = jnp.where(use_cur, p, 0.0).astype(BF16)
        p_prev = jnp.where(use_cur, 0.0, p).astype(BF16)
        return p_cur, p_prev, jnp.where(first_head, sink_terms[0], sink_terms[1])

    def weighted_values(item, p_cur, p_prev, sink_term):
        blk, kh = item
        v_prev, v_cur = key_blocks(vp_ref, vc_ref, blk, slice(kh * hd, (kh + 1) * hd))
        v_cur = jnp.concatenate([_pair_block_diag(v_cur), _pair_block_diag(ones)], axis=1)
        v_prev = jnp.concatenate([_pair_block_diag(v_prev), _pair_block_diag(ones)], axis=1)
        acc = _dot(p_cur, v_cur) + _dot(p_prev, v_prev)
        out = acc[:, :2 * hd] / (acc[:, 2 * hd:] + sink_term)
        for pr in range(pairs):
            lo = (kh * pairs + pr) * 2 * hd
            o_ref[0, blk * W:(blk + 1) * W, lo:lo + 2 * hd] = out[pr * W:(pr + 1) * W, :].astype(o_ref.dtype)

    items = [(blk, kh) for blk in range(nblk) for kh in range(B_KV_HEADS)]
    sc = {i: scores(items[i]) for i in range(min(2, len(items)))}
    probs = {0: softmax(items[0], sc.pop(0))}
    for i, item in enumerate(items):
        if i + 2 < len(items):
            sc[i + 2] = scores(items[i + 2])
        if i + 1 < len(items):
            probs[i + 1] = softmax(items[i + 1], sc.pop(i + 1))
        weighted_values(item, *probs.pop(i))


def _swa_prompt(q, k, v, row0, B, S, sinks, out_dtype, name):
    qw, kw = q.shape[1], k.shape[1]
    nb = S // WINDOW
    group = qw // kw
    if group % 2 or 2 * B_HEAD_DIM != LANES:
        raise ValueError("the prompt attention kernel pairs query heads into LANES-wide slices")
    nblk = SWA_BLOCKS_PER_STEP if nb % SWA_BLOCKS_PER_STEP == 0 and row0 % (SWA_BLOCKS_PER_STEP * WINDOW) == 0 else 1
    if row0 % (nblk * WINDOW):
        raise ValueError("the stream must start on a query block boundary")
    steps = nb // nblk
    first = row0 // (nblk * WINDOW)
    cur = lambda w: pl.BlockSpec((nblk * WINDOW, w), lambda b, j: (first + b * steps + j, 0))
    prev = lambda w: pl.BlockSpec((WINDOW, w), lambda b, j: (jnp.maximum((first + b * steps + j) * nblk - 1, 0), 0))
    kern = functools.partial(_swa_prompt_kernel, group=group)
    return pl.pallas_call(
        kern,
        grid=(B, steps),
        in_specs=[pl.BlockSpec(memory_space=pltpu.SMEM), cur(qw), prev(kw), cur(kw), prev(kw), cur(kw)],
        out_specs=pl.BlockSpec((1, nblk * WINDOW, qw), lambda b, j: (b, j, 0)),
        out_shape=jax.ShapeDtypeStruct((B, S, qw), out_dtype),
        compiler_params=_params("parallel", "parallel"),
        name=name,
    )(sinks, q, k, k, v, v)


def _swa_sample_kernel(sinks_ref, q_ref, kc_ref, kn_ref, vc_ref, vn_ref, o_ref, *, group, bblk):
    T = q_ref.shape[0] // bblk
    wb, kw = kc_ref.shape[1:]
    hd = B_HEAD_DIM
    nq = B_KV_HEADS * group
    R = nq * T
    per_seq = lambda ref: ref[...].astype(F32).reshape(bblk, T, ref.shape[1])
    k_new, v_new = per_seq(kn_ref).astype(BF16), per_seq(vn_ref).astype(BF16)
    qb = per_seq(q_ref)
    blocks = []
    for kh in range(B_KV_HEADS):
        for g in range(group):
            lo = (kh * group + g) * hd
            parts = [qb[:, :, lo:lo + hd]]
            if kh:
                parts.insert(0, jnp.zeros((bblk, T, kh * hd), qb.dtype))
            if kh < B_KV_HEADS - 1:
                parts.append(jnp.zeros((bblk, T, (B_KV_HEADS - 1 - kh) * hd), qb.dtype))
            blocks.append(jnp.concatenate(parts, axis=-1))
    q_all = jnp.concatenate(blocks, axis=1).astype(BF16)

    t = lax.broadcasted_iota(jnp.int32, (R, wb), 0) % T
    i = lax.broadcasted_iota(jnp.int32, (R, wb), 1)
    mask_buf = jnp.logical_and((t + wb) - i < WINDOW, (PAST_LEN - wb) + i >= 0)
    tn = lax.broadcasted_iota(jnp.int32, (R, T), 0) % T
    u = lax.broadcasted_iota(jnp.int32, (R, T), 1)
    mask_new = u <= tn
    ridx = lax.broadcasted_iota(jnp.int32, (R, 1), 0)
    sink = jnp.full((R, 1), sinks_ref[0, 0], F32)
    for hq in range(1, nq):
        sink = jnp.where(ridx >= hq * T, sinks_ref[0, hq], sink)

    sc_buf = jnp.einsum("bqd,bkd->bqk", q_all, kc_ref[...].astype(BF16), preferred_element_type=F32)
    sc_new = jnp.einsum("bqd,bkd->bqk", q_all, k_new, preferred_element_type=F32)
    sc_buf = jnp.where(mask_buf[None], sc_buf, -jnp.inf)
    sc_new = jnp.where(mask_new[None], sc_new, -jnp.inf)
    mx = jnp.maximum(jnp.maximum(jnp.max(sc_buf, axis=-1, keepdims=True),
                                 jnp.max(sc_new, axis=-1, keepdims=True)), sink[None])
    p_buf = jnp.exp(sc_buf - mx)
    p_new = jnp.exp(sc_new - mx)
    denom = (jnp.sum(p_buf, axis=-1, keepdims=True) + jnp.sum(p_new, axis=-1, keepdims=True)
             + jnp.exp(sink[None] - mx))
    acc = (jnp.einsum("bqk,bkd->bqd", p_buf.astype(BF16), vc_ref[...].astype(BF16), preferred_element_type=F32)
           + jnp.einsum("bqk,bkd->bqd", p_new.astype(BF16), v_new, preferred_element_type=F32))
    out = (acc / denom).astype(o_ref.dtype)
    for kh in range(B_KV_HEADS):
        for g in range(group):
            hq = kh * group + g
            o_ref[:, :, hq * hd:(hq + 1) * hd] = out[:, hq * T:(hq + 1) * T, kh * hd:(kh + 1) * hd]


def _swa_sample(q, k_buf, k_new, v_buf, v_new, row0, T, sinks, bblk, name):
    qw = q.shape[1]
    B, wb, kw = k_buf.shape
    group = qw // kw
    if row0 % (bblk * T):
        raise ValueError("the stream must start on a sequence block boundary")
    first = row0 // (bblk * T)
    blk = lambda r, w: pl.BlockSpec((bblk, r, w), lambda i: (i, 0, 0))
    new = lambda w: pl.BlockSpec((bblk * T, w), lambda i: (first + i, 0))
    kern = functools.partial(_swa_sample_kernel, group=group, bblk=bblk)
    return pl.pallas_call(
        kern,
        grid=(B // bblk,),
        in_specs=[pl.BlockSpec(memory_space=pltpu.SMEM), new(qw), blk(wb, kw), new(kw), blk(wb, kw), new(kw)],
        out_specs=blk(T, qw),
        out_shape=jax.ShapeDtypeStruct((B, T, qw), BF16),
        compiler_params=_params("parallel"),
        name=name,
    )(sinks, q, k_buf, k_new, v_buf, v_new)


def _rope_tables(pos):
    inv_freq = jnp.exp(-math.log(ROPE_THETA) * jnp.arange(0, ROT_DIM, 2, dtype=F32) / ROT_DIM)
    ang = inv_freq[:, None] * pos.astype(F32)[None, :]
    cos, sin = jnp.cos(ang), jnp.sin(ang)
    rest = (B_HEAD_DIM - ROT_DIM, pos.shape[0])
    cos_head = [cos, cos, jnp.ones(rest, F32)]
    sin_head = [-sin, sin, jnp.zeros(rest, F32)]
    heads = LANES // B_HEAD_DIM
    return jnp.concatenate(cos_head * heads, axis=0).T, jnp.concatenate(sin_head * heads, axis=0).T


def _largest_divisor(n, cap):
    d = min(n, cap)
    while n % d:
        d -= 1
    return d


def _forward(streams, p):
    n_a = p["a_w_in"].shape[0]
    depth = p["ffn_norm"].shape[0]
    rows = tuple(st["x"].shape[0] * st["x"].shape[1] for st in streams)
    x = tuple(st["x"].reshape(n, -1) for st, n in zip(streams, rows))

    def window(i):
        if isinstance(x, tuple):
            return (x[i], 0, rows[i])
        return (x, sum(rows[:i]), rows[i])

    ffn_order = [(l, half) for l in range(depth) for half in (0, 1)]
    ffn_w = tuple(w[0, 0].astype(BF16) for w in p["ffn_w32"])

    def ffn_after(l, half):
        k = ffn_order.index((l, half)) + 1
        return tuple(_at(w, *ffn_order[k]) for w in p["ffn_w32"]) if k < len(ffn_order) else ()

    final_state = [None for _ in streams]
    kv_new = None
    if not 0 < n_a < depth:
        raise ValueError("expected mLSTM layers followed by attention layers")
    tables = [_rope_tables(st["pos"]) for st in streams]
    for l in range(depth):
        post = None
        if l >= n_a:
            post = ("q", _at(p["mix_norm"], l), _at(p["b_w_q"], l - n_a), _at(p["b_b_q"], l - n_a),
                    *tables, BF16)
        x, posts, ffn_w = _ffn(x, rows, _at(p["ffn_norm"], l, 0), *ffn_w, cast=ffn_after(l, 0), post=post,
                               name=f"ffn_{l}a")
        mixed = []
        for i, st in enumerate(streams):
            B, T, _ = st["x"].shape
            tag, act = st["tag"], st["dtype"]
            seq = lambda a: a.reshape(B, T, a.shape[-1])
            if l < n_a:
                q, k, v, o, gates, gt = _inproj(window(i), T, _at(p["mix_norm"], l), _at(p["a_w_in"], l),
                                                _at(p["a_bg"], l), act, name=f"inproj_{tag}_{l}")
                bblk = _largest_divisor(B, 2 if T >= MLSTM_CHUNK else 8)
                layer0 = l if st["state"][0].shape[0] == n_a else 0
                hg, *final_state[i] = _mlstm(seq(q), seq(k), seq(v), seq(o), seq(gates), gt,
                                             _at(p["a_head_norm"], l), st["state"], layer0, l, n_a,
                                             final_state[i], bblk, BF16, name=f"mlstm_{tag}_{l}")
                mixed.append(hg.reshape(B * T, -1))
            else:
                j = l - n_a
                row0 = sum(rows[:i])
                if st["cache"] is None:
                    o = _swa_prompt(posts[0], *kv_new, row0, B, T, p["b_sinks"][j], BF16, name=f"swa_{tag}_{l}")
                else:
                    kc, vc = st["cache"]
                    o = _swa_sample(posts[0], kc.reshape(B, kc.shape[1], -1), kv_new[0],
                                    vc.reshape(B, vc.shape[1], -1), kv_new[1], row0, T, p["b_sinks"][j],
                                    _largest_divisor(B, 8), name=f"swa_{tag}_{l}")
                mixed.append(o.reshape(B * T, -1))
        if l < n_a:
            mix = (tuple(mixed), _at(p["a_w_out"], l), p["zero_bias"])
        else:
            mix = (tuple(mixed), _at(p["b_w_o"], l - n_a), _at(p["b_b_o"], l - n_a))
        last = l == depth - 1
        post = ("kv", p["kv_norm"], p["w_kv"], p["b_kv"], *tables, F32) if l == n_a - 1 else None
        x, posts, ffn_w = _ffn(x, rows, _at(p["ffn_norm"], l, 1), *ffn_w, mix=mix, cast=ffn_after(l, 1),
                               post=post, final_gain=p["final_norm"] if last else None, split_out=last,
                               name=f"ffn_{l}b")
        if post is not None:
            kv_new = posts

    results = []
    for i, st in enumerate(streams):
        B, T, D = st["x"].shape
        new_c, new_n, new_m = final_state[i]
        row0 = sum(rows[:i])
        k_new, v_new = (a[row0:row0 + rows[i]].reshape(B, T, -1) for a in kv_new)
        if st["cache"] is None:
            wb = min(WINDOW, T)
            buf_k, buf_v = k_new[:, -wb:], v_new[:, -wb:]
        else:
            wb = st["cache"][0].shape[1]
            flat = lambda a: a.reshape(B, a.shape[1], -1)
            buf_k = jnp.concatenate([flat(st["cache"][0]), k_new], axis=1)[:, -wb:]
            buf_v = jnp.concatenate([flat(st["cache"][1]), v_new], axis=1)[:, -wb:]
        heads = lambda a: a.reshape(B, wb, B_KV_HEADS, B_HEAD_DIM)
        results.append((x[i].reshape(B, T, D), new_c, new_n, new_m[..., 0], heads(buf_k), heads(buf_v)))
    return results


def kernel(x_prompt, x_sample, state_mlstm_C, state_mlstm_n, state_mlstm_m, cache_swa_k, cache_swa_v,
           ffn_norm, ffn_w_gate, ffn_w_up, ffn_w_down, mix_norm, a_w_in, a_b_gate, a_head_norm, a_w_out,
           kv_norm, w_kv, b_kv, b_w_q, b_b_q, b_sinks, b_w_o, b_b_o, final_norm):
    n_a, d_model, a_proj = a_w_in.shape
    ng = 2 * A_HEADS
    qk_w = d_model // 2
    v_w = d_model
    if a_proj != 2 * qk_w + 2 * v_w + ng:
        raise ValueError("unexpected mLSTM projection width")
    row = lambda a: a[..., None, :].astype(F32)
    p = {
        "ffn_norm": row(ffn_norm), "mix_norm": row(mix_norm), "kv_norm": row(kv_norm),
        "final_norm": row(final_norm), "a_head_norm": row(a_head_norm),
        "ffn_w32": (ffn_w_gate, ffn_w_up, ffn_w_down),
        "a_w_in": jnp.pad(a_w_in, ((0, 0), (0, 0), (0, LANES - ng))).astype(BF16),
        "a_bg": jnp.pad(a_b_gate, ((0, 0), (0, LANES - ng)))[:, None, :].astype(F32),
        "a_w_out": a_w_out.astype(BF16),
        "zero_bias": jnp.zeros((1, d_model), F32),
        "w_kv": w_kv.astype(BF16), "b_kv": row(b_kv),
        "b_w_q": b_w_q.astype(BF16), "b_b_q": row(b_b_q), "b_sinks": row(b_sinks),
        "b_w_o": b_w_o.astype(BF16), "b_b_o": row(b_b_o),
    }
    Bp, Sp, _ = x_prompt.shape
    Bs, Ts, _ = x_sample.shape
    zc = jnp.zeros((1, Bp) + state_mlstm_C.shape[2:], F32)
    zn = jnp.zeros((1, Bp) + state_mlstm_n.shape[2:], F32)
    zm = jnp.zeros((1, Bp) + state_mlstm_m.shape[2:] + (1,), F32)
    s_state = (state_mlstm_C, state_mlstm_n, state_mlstm_m[..., None])

    p_pos = jnp.arange(math.lcm(Sp, 2 * TOKEN_TILE), dtype=jnp.int32) % Sp
    s_pos = PAST_LEN + jnp.arange(math.lcm(Ts, 2 * TOKEN_TILE), dtype=jnp.int32) % Ts

    prompt = dict(x=x_prompt, pos=p_pos, state=(zc, zn, zm), cache=None, tag="p", dtype=BF16)
    sample = dict(x=x_sample, pos=s_pos, state=s_state, cache=(cache_swa_k, cache_swa_v), tag="s", dtype=F32)
    (y_p, p_c, p_n, p_m, p_k, p_v), (y_s, s_c, s_n, s_m, s_k, s_v) = _forward([prompt, sample], p)
    return (y_p, y_s, p_c, p_n, p_m, p_k, p_v, s_c, s_n, s_m, s_k, s_v)
```

```python
import functools
import math

import jax
import jax.numpy as jnp
from jax import lax
from jax.experimental import pallas as pl
from jax.experimental.pallas import tpu as pltpu

F32 = jnp.float32
BF16 = jnp.bfloat16

EPS = 1e-6
GATE_CAP = 15.0
A_HEADS = 4
MLSTM_CHUNK = 256
B_HEAD_DIM = 64
B_KV_HEADS = 4
WINDOW = 128
ROT_DIM = B_HEAD_DIM // 4
ROPE_THETA = 500000.0
PAST_LEN = 8192

LANES = 128
V7X_VMEM_LIMIT_BYTES = 60 * 1024 * 1024
TOKEN_TILE = 512
FFN_SUBTILE = 512
FF_TILE = 256
CAST_STEPS = 16
SWA_BLOCKS_PER_STEP = 4


def _rms(x, g):
    return (x * lax.rsqrt(jnp.mean(x * x, axis=-1, keepdims=True) + EPS)) * g


def _dot(a, b):
    return jnp.dot(a, b, preferred_element_type=F32)


def _dot_nt(a, b):
    return lax.dot_general(a, b, (((1,), (1,)), ((), ())), preferred_element_type=F32)


def _dot_f32(a, b):
    return jnp.dot(a, b, preferred_element_type=F32, precision=lax.Precision.HIGHEST)


def _at(arr, *idx):
    return (arr, idx)


def _split(param):
    return param if isinstance(param, tuple) else (param, ())


def _shape(param):
    arr, idx = _split(param)
    return arr.shape[len(idx):]


def _resident(param):
    arr, idx = _split(param)
    block = (None,) * len(idx) + arr.shape[len(idx):]
    index = idx + (0,) * (arr.ndim - len(idx))
    return arr, pl.BlockSpec(block, lambda *_: index, pipeline_mode=pl.Buffered(1))


def _operands(tiled, residents):
    pairs = list(tiled) + [_resident(r) for r in residents]
    return [a for a, _ in pairs], [sp for _, sp in pairs]


def _rows(tm, width):
    return pl.BlockSpec((tm, width), lambda i: (i, 0))


def _params(*sem):
    return pltpu.CompilerParams(dimension_semantics=sem, vmem_limit_bytes=V7X_VMEM_LIMIT_BYTES)


def _token_tile(n):
    tm = TOKEN_TILE
    while n % tm:
        tm //= 2
    return tm


def _ffn_kernel(*refs, n_x, n_hg, n_cast, final, post, n_out, n_first, tf):
    refs = list(refs)
    x_refs = [refs.pop(0) for _ in range(n_x)]
    hg_refs = [refs.pop(0) for _ in range(n_hg)]
    cast_src = [refs.pop(0) for _ in range(n_cast)]
    if post:
        cos_refs, sin_refs = [refs.pop(0), refs.pop(0)], [refs.pop(0), refs.pop(0)]
    if n_hg:
        wmix_ref, bmix_ref = refs.pop(0), refs.pop(0)
    gn_ref, wg_ref, wu_ref, wd_ref = refs.pop(0), refs.pop(0), refs.pop(0), refs.pop(0)
    if final:
        fg_ref = refs.pop(0)
    if post:
        pgn_ref, pw_ref, pb_ref = refs.pop(0), refs.pop(0), refs.pop(0)
    o_refs = [refs.pop(0) for _ in range(n_out)]
    post_refs = [refs.pop(0) for _ in range({None: 0, "q": 1, "kv": 2}[post])]
    cast_dst = [refs.pop(0) for _ in range(n_cast)]
    (a_ref,) = refs
    first = pl.program_id(0) < n_first

    if n_cast:
        @pl.when(pl.program_id(0) < CAST_STEPS)
        def _():
            for src, dst in zip(cast_src, cast_dst):
                dst[...] = src[...].astype(dst.dtype)

    tm = a_ref.shape[0]
    sub = min(tm, FFN_SUBTILE)
    for r0 in range(0, tm, sub):
        rows = slice(r0, r0 + sub)

        def current(rs, dtype):
            vals = [r[rows, :].astype(dtype) for r in rs]
            return vals[0] if len(vals) == 1 else jnp.where(first, vals[0], vals[1])

        x = current(x_refs, F32)
        if n_hg:
            x = x + (_dot(current(hg_refs, BF16), wmix_ref[...]) + bmix_ref[...])
        h = _rms(x, gn_ref[...]).astype(BF16)
        for j in range(wg_ref.shape[1] // tf):
            sl = slice(j * tf, (j + 1) * tf)
            g = _dot(h, wg_ref[:, sl])
            u = _dot(h, wu_ref[:, sl])
            a_ref[rows, sl] = ((g * jax.nn.sigmoid(g)) * u).astype(BF16)
        out = x + 0.5 * _dot(a_ref[rows, :], wd_ref[...])
        if post:
            proj = _dot(_rms(out, pgn_ref[...]).astype(BF16), pw_ref[...]) + pb_ref[...]
            cos, sin = current(cos_refs, F32), current(sin_refs, F32)
            if post == "q":
                post_refs[0][rows, :] = (_rope(proj, cos, sin) * B_HEAD_DIM ** -0.5).astype(post_refs[0].dtype)
            else:
                kw = proj.shape[1] // 2
                post_refs[0][rows, :] = _rope(proj[:, :kw], cos, sin).astype(post_refs[0].dtype)
                post_refs[1][rows, :] = proj[:, kw:].astype(post_refs[1].dtype)
        if final:
            out = _rms(out, fg_ref[...])
        if n_out == 1:
            o_refs[0][rows, :] = out
        else:
            @pl.when(first)
            def _():
                o_refs[0][rows, :] = out

            @pl.when(jnp.logical_not(first))
            def _():
                o_refs[1][rows, :] = out


def _stream_rows(tm, width, n_first):
    return (pl.BlockSpec((tm, width), lambda i: (jnp.minimum(i, n_first - 1), 0)),
            pl.BlockSpec((tm, width), lambda i: (jnp.maximum(i - n_first, 0), 0)))


def _ffn(x, rows, gn, wg, wu, wd, mix=None, final_gain=None, split_out=False, cast=(), post=None, name="ffn"):
    n = sum(rows)
    d = _shape(gn)[1]
    dff = _shape(wg)[1]
    tf = FF_TILE if dff % FF_TILE == 0 else dff
    residents = ([mix[1], mix[2]] if mix is not None else []) + [gn, wg, wu, wd]
    if final_gain is not None:
        residents.append(final_gain)
    streamed = [x] + ([mix[0]] if mix is not None else [])
    post_widths = []
    if post is not None:
        kind, pgn, pw, pb, tables_first, tables_second, post_dtype = post
        residents += [pgn, pw, pb]
        streamed += [(tables_first[0], tables_second[0]), (tables_first[1], tables_second[1])]
        post_widths = [_shape(pw)[1]] if kind == "q" else [_shape(pw)[1] // 2] * 2
    flat_streamed = [a for s in streamed for a in (s if isinstance(s, tuple) else (s,))]
    cast_chunks = [(_shape(c)[0] // CAST_STEPS, _shape(c)[1]) for c in cast]

    nbytes = lambda shape, dtype: math.prod(shape) * jnp.dtype(dtype).itemsize
    fixed = (sum(nbytes(_shape(r), _split(r)[0].dtype) for r in residents)
             + 2 * sum(nbytes(c, F32) + nbytes(c, BF16) for c in cast_chunks))
    per_row = (2 * sum(nbytes(a.shape[1:], a.dtype) for a in flat_streamed)
               + 2 * (2 if split_out else 1) * d * 4 + dff * 2 + d * 4
               + 2 * sum(nbytes((w,), post_dtype) for w in post_widths))
    tm = 2 * TOKEN_TILE
    while math.gcd(*rows) % tm or fixed + tm * per_row > V7X_VMEM_LIMIT_BYTES:
        tm //= 2
    n_first = rows[0] // tm
    if cast and n // tm < CAST_STEPS:
        raise ValueError("too few grid steps for the weight rounding side job")

    def tiles(a):
        if isinstance(a, tuple):
            return list(zip(a, _stream_rows(tm, a[0].shape[1], n_first)))
        return [(a, _rows(tm, a.shape[1]))]

    tiled = tiles(x)
    n_x = len(tiled)
    n_hg = 0
    if mix is not None:
        tiled += tiles(mix[0])
        n_hg = len(tiled) - n_x
    chunk_of = lambda i: jnp.minimum(i, CAST_STEPS - 1)
    for c, chunk in zip(cast, cast_chunks):
        arr, idx = _split(c)
        tiled.append((arr, pl.BlockSpec((None,) * len(idx) + chunk, lambda i, idx=idx: idx + (chunk_of(i), 0))))
    if post is not None:
        blocks = [t[0].shape[0] // tm for t in (tables_first, tables_second)]
        table_specs = (pl.BlockSpec((tm, LANES), lambda i: (jnp.minimum(i, n_first - 1) % blocks[0], 0)),
                       pl.BlockSpec((tm, LANES), lambda i: (jnp.maximum(i - n_first, 0) % blocks[1], 0)))
        for which in (0, 1):
            tiled += list(zip((tables_first[which], tables_second[which]), table_specs))
    args, specs = _operands(tiled, residents)
    if split_out:
        out_specs = list(_stream_rows(tm, d, n_first))
        out_shape = [jax.ShapeDtypeStruct((r, d), F32) for r in rows]
    else:
        out_specs, out_shape = [_rows(tm, d)], [jax.ShapeDtypeStruct((n, d), F32)]
    n_out = len(out_specs)
    for w in post_widths:
        out_specs.append(_rows(tm, w))
        out_shape.append(jax.ShapeDtypeStruct((n, w), post_dtype))
    n_post = len(post_widths)
    out_specs += [pl.BlockSpec(chunk, lambda i: (chunk_of(i), 0)) for chunk in cast_chunks]
    out_shape += [jax.ShapeDtypeStruct(_shape(c), BF16) for c in cast]
    kern = functools.partial(_ffn_kernel, n_x=n_x, n_hg=n_hg, n_cast=len(cast), final=final_gain is not None,
                             post=post[0] if post is not None else None, n_out=n_out, n_first=n_first, tf=tf)
    outs = pl.pallas_call(
        kern,
        grid=(n // tm,),
        in_specs=specs,
        out_specs=out_specs,
        out_shape=out_shape,
        scratch_shapes=[pltpu.VMEM((tm, dff), BF16)],
        compiler_params=_params("arbitrary"),
        name=name,
    )(*args)
    y = tuple(outs[:n_out]) if split_out else outs[0]
    return y, list(outs[n_out:n_out + n_post]), tuple(outs[n_out + n_post:])


def _inproj_kernel(x_ref, gn_ref, wqk_ref, wv_ref, wo_ref, wgt_ref, bg_ref,
                   q_ref, k_ref, v_ref, o_ref, g_ref, *gt_refs, k_scale):
    h = _rms(x_ref[...], gn_ref[...]).astype(BF16)
    dqk = q_ref.shape[1]
    qk = _dot(h, wqk_ref[...])
    q_ref[...] = qk[:, :dqk].astype(q_ref.dtype)
    k_ref[...] = (qk[:, dqk:] * k_scale).astype(k_ref.dtype)
    v_ref[...] = _dot(h, wv_ref[...]).astype(v_ref.dtype)
    o_ref[...] = _dot(h, wo_ref[...])
    gates = _dot(h, wgt_ref[...]) + bg_ref[...]
    gates = GATE_CAP * jnp.tanh(gates / GATE_CAP)
    logsig = jnp.minimum(gates, 0.0) - jnp.log1p(jnp.exp(-jnp.abs(gates)))
    lane = lax.broadcasted_iota(jnp.int32, gates.shape, 1)
    gates = jnp.where(lane < A_HEADS, gates, logsig)
    g_ref[...] = gates
    for gt_ref in gt_refs:
        chunks, ng, L = gt_ref.shape
        for c in range(chunks):
            gt_ref[c] = gates[c * L:(c + 1) * L, :].T[:ng, :]


def _mlstm_chunk(S):
    return MLSTM_CHUNK if S % MLSTM_CHUNK == 0 else S


def _window_tile(row0, n):
    return _token_tile(math.gcd(row0, n))


def _window(xw):
    arr, row0, n = xw
    tm = _window_tile(row0, n)
    blk0 = row0 // tm
    return n, tm, (arr, pl.BlockSpec((tm, arr.shape[1]), lambda i: (i + blk0, 0)))


def _inproj(xw, seq_len, gn, wqk, wv, wo, wgt, bg, act_dtype, name):
    n, tm, x_op = _window(xw)
    dqk = _shape(wqk)[1] // 2
    dv = _shape(wv)[1]
    kern = functools.partial(_inproj_kernel, k_scale=(dqk // A_HEADS) ** -0.5)
    args, specs = _operands([x_op], [gn, wqk, wv, wo, wgt, bg])
    out_specs = [_rows(tm, dqk), _rows(tm, dqk), _rows(tm, dv), _rows(tm, dv), _rows(tm, LANES)]
    out_shape = [jax.ShapeDtypeStruct((n, dqk), act_dtype), jax.ShapeDtypeStruct((n, dqk), act_dtype),
                 jax.ShapeDtypeStruct((n, dv), act_dtype), jax.ShapeDtypeStruct((n, dv), F32),
                 jax.ShapeDtypeStruct((n, LANES), F32)]
    L = _mlstm_chunk(seq_len)
    with_gt = L % LANES == 0 and tm % L == 0
    if with_gt:
        out_specs.append(pl.BlockSpec((tm // L, 2 * A_HEADS, L), lambda i: (i, 0, 0)))
        out_shape.append(jax.ShapeDtypeStruct((n // L, 2 * A_HEADS, L), F32))
    outs = pl.pallas_call(
        kern,
        grid=(n // tm,),
        in_specs=specs,
        out_specs=out_specs,
        out_shape=out_shape,
        compiler_params=_params("parallel"),
        name=name,
    )(*args)
    return tuple(outs[:5]) + ((outs[5],) if with_gt else (None,))


def _rope(x, cos, sin):
    width = x.shape[1]
    reps = width // LANES
    cosw = jnp.concatenate([cos] * reps, axis=1)
    sinw = jnp.concatenate([sin] * reps, axis=1)
    half = ROT_DIM // 2
    lane = lax.broadcasted_iota(jnp.int32, (1, width), 1)
    first_half = (lane & (B_HEAD_DIM - 1)) < half
    partner = jnp.where(first_half, pltpu.roll(x, width - half, 1), pltpu.roll(x, half, 1))
    return x * cosw + partner * sinw


def _mlstm_kernel(q_ref, k_ref, v_ref, o_ref, gtm_ref, gt_ref, hn_ref, c0_ref, n0_ref, m0_ref, *rest,
                  bblk, chunk):
    out_ref, c_ref, n_ref, m_ref = rest[-4:]
    L = chunk
    dqk = q_ref.shape[2] // A_HEADS
    dv = v_ref.shape[2] // A_HEADS

    @pl.when(pl.program_id(1) == 0)
    def _():
        c_ref[...] = c0_ref[...]
        n_ref[...] = n0_ref[...]
        m_ref[...] = m0_ref[...]

    row = lax.broadcasted_iota(jnp.int32, (L, L), 0)
    col = lax.broadcasted_iota(jnp.int32, (L, L), 1)
    tril = col <= row
    lower_ones = tril.astype(F32)
    upper_ones = (row <= col).astype(F32)

    chains = [(b, h) for b in range(bblk) for h in range(A_HEADS)]
    per_chain = lambda f: jnp.stack([f(b, h) for b, h in chains])

    gts = [gt_ref[b, 0] for b in range(bblk)]
    gtms = [gtm_ref[b] for b in range(bblk)]
    csum_rows = [_dot_f32(gt, upper_ones) for gt in gts]
    csum_cols = [_dot_f32(lower_ones, gtm) for gtm in gtms]
    i_row = per_chain(lambda b, h: gts[b][h:h + 1, :])
    b_row = per_chain(lambda b, h: csum_rows[b][A_HEADS + h:A_HEADS + h + 1, :])
    i_col = per_chain(lambda b, h: gtms[b][:, h:h + 1])
    b_col = per_chain(lambda b, h: csum_cols[b][:, A_HEADS + h:A_HEADS + h + 1])
    b_last = b_col[:, L - 1:L, :]
    m_prev = per_chain(lambda b, h: m_ref[b, h:h + 1, :])
    n_prev = per_chain(lambda b, h: n_ref[b, h:h + 1, :])
    c_prev = per_chain(lambda b, h: c_ref[b, h])

    q = per_chain(lambda b, h: q_ref[b, :, h * dqk:(h + 1) * dqk]).astype(BF16)
    k = per_chain(lambda b, h: k_ref[b, :, h * dqk:(h + 1) * dqk]).astype(BF16)
    v = per_chain(lambda b, h: v_ref[b, :, h * dv:(h + 1) * dv]).astype(BF16)

    dmat = jnp.where(tril[None], (b_col - b_row) + i_row, -jnp.inf)
    g = b_col + m_prev
    mt = jnp.maximum(g, jnp.max(dmat, axis=-1, keepdims=True))
    w_inter = jnp.exp(g - mt)
    s = jnp.einsum("gtd,gsd->gts", q, k, preferred_element_type=F32) * jnp.exp(dmat - mt)
    G = len(chains)
    v_ones = jnp.concatenate([v, jnp.ones((G, L, LANES), BF16)], axis=-1)
    n_cols = jnp.broadcast_to(jnp.swapaxes(n_prev, 1, 2), (G, dqk, LANES))
    c_n = jnp.concatenate([c_prev, n_cols], axis=-1).astype(BF16)
    intra = jnp.einsum("gts,gsv->gtv", s.astype(BF16), v_ones, preferred_element_type=F32)
    inter = w_inter * jnp.einsum("gtd,gdv->gtv", q, c_n, preferred_element_type=F32)
    both = intra + inter
    den = jnp.maximum(jnp.abs(both[:, :, dv:]), jnp.exp(-mt))
    hc = both[:, :, :dv] / jnp.concatenate([den] * (dv // LANES), axis=-1)

    m_new = mt[:, L - 1:L, :]
    a_col = jnp.exp(((b_last - b_col) + i_col) - m_new)
    decay = jnp.exp((b_last + m_prev) - m_new)
    ka = k.astype(F32) * a_col
    kat = jnp.swapaxes(ka, 1, 2).astype(BF16)
    c_new = decay * c_prev + jnp.einsum("gds,gsv->gdv", kat, v, preferred_element_type=F32)
    n_new = decay * n_prev + jnp.sum(ka, axis=1, keepdims=True)

    gain = per_chain(lambda b, h: hn_ref[:, h * dv:(h + 1) * dv])
    gate = jax.nn.sigmoid(per_chain(lambda b, h: o_ref[b, :, h * dv:(h + 1) * dv]))
    res = (_rms(hc, gain) * gate).astype(out_ref.dtype)
    for gi, (b, h) in enumerate(chains):
        c_ref[b, h] = c_new[gi]
        n_ref[b, h:h + 1, :] = n_new[gi]
        m_ref[b, h:h + 1, :] = m_new[gi]
        out_ref[b, :, h * dv:(h + 1) * dv] = res[gi]


def _mlstm(q, k, v, o, gates, gt, head_norm, state0, layer0, layer, n_layers, prev_final, bblk, out_dtype,
           name):
    B, S, qw = q.shape
    vw = v.shape[2]
    L = _mlstm_chunk(S)
    nc = S // L
    ng = 2 * A_HEADS
    if gt is None:
        gt = jnp.swapaxes(gates[:, :, :ng].reshape(B, nc, L, ng), 2, 3)
    gt = gt.reshape(B, nc, ng, L)
    kern = functools.partial(_mlstm_kernel, bblk=bblk, chunk=L)
    seq = lambda w: pl.BlockSpec((bblk, L, w), lambda i, c: (i, c, 0))

    def state(a, row):
        return pl.BlockSpec((None, bblk) + a.shape[2:], lambda i, c: (row, i) + (0,) * (a.ndim - 2))

    hn_arr, hn_spec = _resident(head_norm)
    args = [q, k, v, o, gates, gt, hn_arr, *state0]
    specs = [seq(qw), seq(qw), seq(vw), seq(vw), seq(LANES),
             pl.BlockSpec((bblk, 1, ng, L), lambda i, c: (i, c, 0, 0)), hn_spec,
             *[state(a, layer0) for a in state0]]
    aliases = {}
    if prev_final is not None:
        aliases = {len(args) + t: 1 + t for t in range(len(prev_final))}
        args += list(prev_final)
        specs += [pl.BlockSpec(memory_space=pl.ANY)] * len(prev_final)
    return pl.pallas_call(
        kern,
        grid=(B // bblk, nc),
        in_specs=specs,
        out_specs=[seq(vw)] + [state(a, layer) for a in state0],
        out_shape=[jax.ShapeDtypeStruct((B, S, vw), out_dtype)]
                  + [jax.ShapeDtypeStruct((n_layers,) + a.shape[1:], F32) for a in state0],
        input_output_aliases=aliases,
        compiler_params=_params("parallel", "arbitrary"),
        name=name,
    )(*args)


def _pair_block_diag(x):
    zero = jnp.zeros_like(x)
    return jnp.concatenate([jnp.concatenate([x, zero], axis=1), jnp.concatenate([zero, x], axis=1)], axis=0)


def _swa_prompt_kernel(sinks_ref, q_ref, kp_ref, kc_ref, vp_ref, vc_ref, o_ref, *, group):
    W = WINDOW
    hd = B_HEAD_DIM
    pairs = group // 2
    R = pairs * W
    nblk = q_ref.shape[0] // W
    has_prev = pl.program_id(1) > 0
    t = lax.broadcasted_iota(jnp.int32, (R, 2 * W), 0) & (W - 1)
    s = lax.broadcasted_iota(jnp.int32, (R, 2 * W), 1) & (W - 1)
    use_cur = s <= t
    visible = jnp.logical_or(use_cur, has_prev)
    row = lax.broadcasted_iota(jnp.int32, (R, 1), 0)
    first_head = lax.broadcasted_iota(jnp.int32, (R, 2 * hd), 1) < hd
    ones = jnp.ones((W, hd), BF16)

    def key_blocks(ref_prev, ref_cur, blk, ksl):
        cur = ref_cur[blk * W:(blk + 1) * W, ksl]
        prev = ref_prev[:, ksl] if blk == 0 else ref_cur[(blk - 1) * W:blk * W, ksl]
        return prev.astype(BF16), cur.astype(BF16)

    def scores(item):
        blk, kh = item
        ksl = slice(kh * hd, (kh + 1) * hd)
        rows = slice(blk * W, (blk + 1) * W)
        q2 = jnp.concatenate(
            [q_ref[rows, (kh * pairs + pr) * 2 * hd:(kh * pairs + pr + 1) * 2 * hd] for pr in range(pairs)],
            axis=0).astype(BF16)
        k_prev, k_cur = key_blocks(kp_ref, kc_ref, blk, ksl)
        sc_cur = _dot_nt(q2, _pair_block_diag(k_cur))
        sc_prev = _dot_nt(q2, _pair_block_diag(k_prev))
        sc = jnp.where(use_cur, sc_cur, sc_prev)
        return jnp.where(visible, sc, -jnp.inf) if blk == 0 else sc

    def softmax(item, sc):
        _, kh = item
        probs, sink_terms = [], []
        for g in range(2):
            sink = jnp.full((R, 1), sinks_ref[0, kh * group + g], F32)
            for pr in range(1, pairs):
                sink = jnp.where(row >= pr * W, sinks_ref[0, kh * group + 2 * pr + g], sink)
            sg = sc[:, g * W:(g + 1) * W]
            mx = jnp.maximum(jnp.max(sg, axis=-1, keepdims=True), sink)
            probs.append(jnp.exp(sg - mx))
            sink_terms.append(jnp.exp(sink - mx))
        p = jnp.concatenate(probs, axis=1)
        p_cur = jnp.where(use_cur, p, 0.0).astype(BF16)
        p_prev = jnp.where(use_cur, 0.0, p).astype(BF16)
        return p_cur, p_prev, jnp.where(first_head, sink_terms[0], sink_terms[1])

    def weighted_values(item, p_cur, p_prev, sink_term):
        blk, kh = item
        v_prev, v_cur = key_blocks(vp_ref, vc_ref, blk, slice(kh * hd, (kh + 1) * hd))
        v_cur = jnp.concatenate([_pair_block_diag(v_cur), _pair_block_diag(ones)], axis=1)
        v_prev = jnp.concatenate([_pair_block_diag(v_prev), _pair_block_diag(ones)], axis=1)
        acc = _dot(p_cur, v_cur) + _dot(p_prev, v_prev)
        out = acc[:, :2 * hd] / (acc[:, 2 * hd:] + sink_term)
        for pr in range(pairs):
            lo = (kh * pairs + pr) * 2 * hd
            o_ref[0, blk * W:(blk + 1) * W, lo:lo + 2 * hd] = out[pr * W:(pr + 1) * W, :].astype(o_ref.dtype)

    items = [(blk, kh) for blk in range(nblk) for kh in range(B_KV_HEADS)]
    sc = {i: scores(items[i]) for i in range(min(2, len(items)))}
    probs = {0: softmax(items[0], sc.pop(0))}
    for i, item in enumerate(items):
        if i + 2 < len(items):
            sc[i + 2] = scores(items[i + 2])
        if i + 1 < len(items):
            probs[i + 1] = softmax(items[i + 1], sc.pop(i + 1))
        weighted_values(item, *probs.pop(i))


def _swa_prompt(q, k, v, row0, B, S, sinks, out_dtype, name):
    qw, kw = q.shape[1], k.shape[1]
    nb = S // WINDOW
    group = qw // kw
    if group % 2 or 2 * B_HEAD_DIM != LANES:
        raise ValueError("the prompt attention kernel pairs query heads into LANES-wide slices")
    nblk = SWA_BLOCKS_PER_STEP if nb % SWA_BLOCKS_PER_STEP == 0 and row0 % (SWA_BLOCKS_PER_STEP * WINDOW) == 0 else 1
    if row0 % (nblk * WINDOW):
        raise ValueError("the stream must start on a query block boundary")
    steps = nb // nblk
    first = row0 // (nblk * WINDOW)
    cur = lambda w: pl.BlockSpec((nblk * WINDOW, w), lambda b, j: (first + b * steps + j, 0))
    prev = lambda w: pl.BlockSpec((WINDOW, w), lambda b, j: (jnp.maximum((first + b * steps + j) * nblk - 1, 0), 0))
    kern = functools.partial(_swa_prompt_kernel, group=group)
    return pl.pallas_call(
        kern,
        grid=(B, steps),
        in_specs=[pl.BlockSpec(memory_space=pltpu.SMEM), cur(qw), prev(kw), cur(kw), prev(kw), cur(kw)],
        out_specs=pl.BlockSpec((1, nblk * WINDOW, qw), lambda b, j: (b, j, 0)),
        out_shape=jax.ShapeDtypeStruct((B, S, qw), out_dtype),
        compiler_params=_params("parallel", "parallel"),
        name=name,
    )(sinks, q, k, k, v, v)


def _swa_sample_kernel(sinks_ref, q_ref, kc_ref, kn_ref, vc_ref, vn_ref, o_ref, *window_refs, group, bblk):
    T = q_ref.shape[0] // bblk
    wb, kw = kc_ref.shape[1:]
    for buf_ref, new_ref, out_ref in zip((kc_ref, vc_ref), (kn_ref, vn_ref), window_refs):
        out_ref[:, :wb - T, :] = buf_ref[:, T:, :]
        out_ref[:, wb - T:, :] = new_ref[...].reshape(bblk, T, kw)
    hd = B_HEAD_DIM
    nq = B_KV_HEADS * group
    R = nq * T
    per_seq = lambda ref: ref[...].astype(F32).reshape(bblk, T, ref.shape[1])
    k_new, v_new = per_seq(kn_ref).astype(BF16), per_seq(vn_ref).astype(BF16)
    qb = per_seq(q_ref)
    blocks = []
    for kh in range(B_KV_HEADS):
        for g in range(group):
            lo = (kh * group + g) * hd
            parts = [qb[:, :, lo:lo + hd]]
            if kh:
                parts.insert(0, jnp.zeros((bblk, T, kh * hd), qb.dtype))
            if kh < B_KV_HEADS - 1:
                parts.append(jnp.zeros((bblk, T, (B_KV_HEADS - 1 - kh) * hd), qb.dtype))
            blocks.append(jnp.concatenate(parts, axis=-1))
    q_all = jnp.concatenate(blocks, axis=1).astype(BF16)

    t = lax.broadcasted_iota(jnp.int32, (R, wb), 0) % T
    i = lax.broadcasted_iota(jnp.int32, (R, wb), 1)
    mask_buf = jnp.logical_and((t + wb) - i < WINDOW, (PAST_LEN - wb) + i >= 0)
    tn = lax.broadcasted_iota(jnp.int32, (R, T), 0) % T
    u = lax.broadcasted_iota(jnp.int32, (R, T), 1)
    mask_new = u <= tn
    ridx = lax.broadcasted_iota(jnp.int32, (R, 1), 0)
    sink = jnp.full((R, 1), sinks_ref[0, 0], F32)
    for hq in range(1, nq):
        sink = jnp.where(ridx >= hq * T, sinks_ref[0, hq], sink)

    sc_buf = jnp.einsum("bqd,bkd->bqk", q_all, kc_ref[...].astype(BF16), preferred_element_type=F32)
    sc_new = jnp.einsum("bqd,bkd->bqk", q_all, k_new, preferred_element_type=F32)
    sc_buf = jnp.where(mask_buf[None], sc_buf, -jnp.inf)
    sc_new = jnp.where(mask_new[None], sc_new, -jnp.inf)
    mx = jnp.maximum(jnp.maximum(jnp.max(sc_buf, axis=-1, keepdims=True),
                                 jnp.max(sc_new, axis=-1, keepdims=True)), sink[None])
    p_buf = jnp.exp(sc_buf - mx)
    p_new = jnp.exp(sc_new - mx)
    denom = (jnp.sum(p_buf, axis=-1, keepdims=True) + jnp.sum(p_new, axis=-1, keepdims=True)
             + jnp.exp(sink[None] - mx))
    acc = (jnp.einsum("bqk,bkd->bqd", p_buf.astype(BF16), vc_ref[...].astype(BF16), preferred_element_type=F32)
           + jnp.einsum("bqk,bkd->bqd", p_new.astype(BF16), v_new, preferred_element_type=F32))
    out = (acc / denom).astype(o_ref.dtype)
    for kh in range(B_KV_HEADS):
        for g in range(group):
            hq = kh * group + g
            o_ref[:, :, hq * hd:(hq + 1) * hd] = out[:, hq * T:(hq + 1) * T, kh * hd:(kh + 1) * hd]


def _swa_sample(q, k_buf, k_new, v_buf, v_new, row0, T, sinks, bblk, with_window, name):
    qw = q.shape[1]
    B, wb, kw = k_buf.shape
    group = qw // kw
    if row0 % (bblk * T):
        raise ValueError("the stream must start on a sequence block boundary")
    if with_window and not 0 < T < wb:
        raise ValueError("the window buffer must be longer than the new tokens")
    first = row0 // (bblk * T)
    blk = lambda r, w: pl.BlockSpec((bblk, r, w), lambda i: (i, 0, 0))
    new = lambda w: pl.BlockSpec((bblk * T, w), lambda i: (first + i, 0))
    kern = functools.partial(_swa_sample_kernel, group=group, bblk=bblk)
    return pl.pallas_call(
        kern,
        grid=(B // bblk,),
        in_specs=[pl.BlockSpec(memory_space=pltpu.SMEM), new(qw), blk(wb, kw), new(kw), blk(wb, kw), new(kw)],
        out_specs=[blk(T, qw)] + with_window * [blk(wb, kw), blk(wb, kw)],
        out_shape=[jax.ShapeDtypeStruct((B, T, qw), BF16)]
                  + with_window * [jax.ShapeDtypeStruct((B, wb, kw), F32)] * 2,
        compiler_params=_params("parallel"),
        name=name,
    )(sinks, q, k_buf, k_new, v_buf, v_new)


def _rope_tables(pos):
    half = ROT_DIM // 2
    inv_freq = jnp.exp(-math.log(ROPE_THETA) * jnp.arange(0, ROT_DIM, 2, dtype=F32) / ROT_DIM)
    ang = pos.astype(F32)[:, None] * inv_freq[None, :]
    d = jnp.arange(LANES, dtype=jnp.int32) % B_HEAD_DIM
    ang_l = jnp.take(ang, d % half, axis=1)
    rot = (d < ROT_DIM)[None, :]
    sign = jnp.where(d < half, -1.0, 1.0)[None, :]
    cos = jnp.where(rot, jnp.cos(ang_l), 1.0)
    sin = jnp.where(rot, jnp.sin(ang_l) * sign, 0.0)
    return cos.astype(F32), sin.astype(F32)


def _largest_divisor(n, cap):
    d = min(n, cap)
    while n % d:
        d -= 1
    return d


def _forward(streams, p):
    n_a = p["a_wqk"].shape[0]
    depth = p["ffn_norm"].shape[0]
    rows = tuple(st["x"].shape[0] * st["x"].shape[1] for st in streams)
    x = tuple(st["x"].reshape(n, -1) for st, n in zip(streams, rows))

    def window(i):
        if isinstance(x, tuple):
            return (x[i], 0, rows[i])
        return (x, sum(rows[:i]), rows[i])

    ffn_order = [(l, half) for l in range(depth) for half in (0, 1)]
    ffn_w = tuple(w[0, 0].astype(BF16) for w in p["ffn_w32"])

    def ffn_after(l, half):
        k = ffn_order.index((l, half)) + 1
        return tuple(_at(w, *ffn_order[k]) for w in p["ffn_w32"]) if k < len(ffn_order) else ()

    final_state = [None for _ in streams]
    kv_new = None
    windows = [None for _ in streams]
    if not 0 < n_a < depth:
        raise ValueError("expected mLSTM layers followed by attention layers")
    tables = [_rope_tables(st["pos"]) for st in streams]
    for l in range(depth):
        post = None
        if l >= n_a:
            post = ("q", _at(p["mix_norm"], l), _at(p["b_w_q"], l - n_a), _at(p["b_b_q"], l - n_a),
                    *tables, BF16)
        x, posts, ffn_w = _ffn(x, rows, _at(p["ffn_norm"], l, 0), *ffn_w, cast=ffn_after(l, 0), post=post,
                               name=f"ffn_{l}a")
        mixed = []
        for i, st in enumerate(streams):
            B, T, _ = st["x"].shape
            tag, act = st["tag"], st["dtype"]
            seq = lambda a: a.reshape(B, T, a.shape[-1])
            if l < n_a:
                q, k, v, o, gates, gt = _inproj(window(i), T, _at(p["mix_norm"], l), _at(p["a_wqk"], l),
                                                _at(p["a_wv"], l), _at(p["a_wo"], l), _at(p["a_wgt"], l),
                                                _at(p["a_bg"], l), act, name=f"inproj_{tag}_{l}")
                bblk = _largest_divisor(B, 2 if T >= MLSTM_CHUNK else 8)
                layer0 = l if st["state"][0].shape[0] == n_a else 0
                hg, *final_state[i] = _mlstm(seq(q), seq(k), seq(v), seq(o), seq(gates), gt,
                                             _at(p["a_head_norm"], l), st["state"], layer0, l, n_a,
                                             final_state[i], bblk, BF16, name=f"mlstm_{tag}_{l}")
                mixed.append(hg.reshape(B * T, -1))
            else:
                j = l - n_a
                row0 = sum(rows[:i])
                if st["cache"] is None:
                    o = _swa_prompt(posts[0], *kv_new, row0, B, T, p["b_sinks"][j], BF16, name=f"swa_{tag}_{l}")
                else:
                    kc, vc = st["cache"]
                    o, *window = _swa_sample(posts[0], kc.reshape(B, kc.shape[1], -1), kv_new[0],
                                             vc.reshape(B, vc.shape[1], -1), kv_new[1], row0, T,
                                             p["b_sinks"][j], _largest_divisor(B, 8), j == 0,
                                             name=f"swa_{tag}_{l}")
                    if window:
                        windows[i] = window
                mixed.append(o.reshape(B * T, -1))
        if l < n_a:
            mix = (tuple(mixed), _at(p["a_w_out"], l), p["zero_bias"])
        else:
            mix = (tuple(mixed), _at(p["b_w_o"], l - n_a), _at(p["b_b_o"], l - n_a))
        last = l == depth - 1
        post = ("kv", p["kv_norm"], p["w_kv"], p["b_kv"], *tables, F32) if l == n_a - 1 else None
        x, posts, ffn_w = _ffn(x, rows, _at(p["ffn_norm"], l, 1), *ffn_w, mix=mix, cast=ffn_after(l, 1),
                               post=post, final_gain=p["final_norm"] if last else None, split_out=last,
                               name=f"ffn_{l}b")
        if post is not None:
            kv_new = posts

    results = []
    for i, st in enumerate(streams):
        B, T, D = st["x"].shape
        new_c, new_n, new_m = final_state[i]
        if st["cache"] is None:
            row0 = sum(rows[:i])
            k_new, v_new = (a[row0:row0 + rows[i]].reshape(B, T, -1) for a in kv_new)
            wb = min(WINDOW, T)
            buf_k, buf_v = k_new[:, -wb:], v_new[:, -wb:]
        else:
            wb = st["cache"][0].shape[1]
            buf_k, buf_v = windows[i]
        heads = lambda a: a.reshape(B, wb, B_KV_HEADS, B_HEAD_DIM)
        results.append((x[i].reshape(B, T, D), new_c, new_n, new_m[..., 0], heads(buf_k), heads(buf_v)))
    return results


def kernel(x_prompt, x_sample, state_mlstm_C, state_mlstm_n, state_mlstm_m, cache_swa_k, cache_swa_v,
           ffn_norm, ffn_w_gate, ffn_w_up, ffn_w_down, mix_norm, a_w_in, a_b_gate, a_head_norm, a_w_out,
           kv_norm, w_kv, b_kv, b_w_q, b_b_q, b_sinks, b_w_o, b_b_o, final_norm):
    n_a, d_model, a_proj = a_w_in.shape
    ng = 2 * A_HEADS
    qk_w = d_model // 2
    v_w = d_model
    if a_proj != 2 * qk_w + 2 * v_w + ng:
        raise ValueError("unexpected mLSTM projection width")
    row = lambda a: a[..., None, :].astype(F32)
    p = {
        "ffn_norm": row(ffn_norm), "mix_norm": row(mix_norm), "kv_norm": row(kv_norm),
        "final_norm": row(final_norm), "a_head_norm": row(a_head_norm),
        "ffn_w32": (ffn_w_gate, ffn_w_up, ffn_w_down),
        "a_wqk": a_w_in[:, :, :2 * qk_w].astype(BF16),
        "a_wv": a_w_in[:, :, 2 * qk_w:2 * qk_w + v_w].astype(BF16),
        "a_wo": a_w_in[:, :, 2 * qk_w + v_w:2 * qk_w + 2 * v_w].astype(BF16),
        "a_wgt": jnp.pad(a_w_in[:, :, 2 * qk_w + 2 * v_w:], ((0, 0), (0, 0), (0, LANES - ng))).astype(BF16),
        "a_bg": jnp.pad(a_b_gate, ((0, 0), (0, LANES - ng)))[:, None, :].astype(F32),
        "a_w_out": a_w_out.astype(BF16),
        "zero_bias": jnp.zeros((1, d_model), F32),
        "w_kv": w_kv.astype(BF16), "b_kv": row(b_kv),
        "b_w_q": b_w_q.astype(BF16), "b_b_q": row(b_b_q), "b_sinks": row(b_sinks),
        "b_w_o": b_w_o.astype(BF16), "b_b_o": row(b_b_o),
    }
    Bp, Sp, _ = x_prompt.shape
    Bs, Ts, _ = x_sample.shape
    zc = jnp.zeros((1, Bp) + state_mlstm_C.shape[2:], F32)
    zn = jnp.zeros((1, Bp) + state_mlstm_n.shape[2:], F32)
    zm = jnp.zeros((1, Bp) + state_mlstm_m.shape[2:] + (1,), F32)
    s_state = (state_mlstm_C, state_mlstm_n, state_mlstm_m[..., None])

    p_pos = jnp.arange(math.lcm(Sp, 2 * TOKEN_TILE), dtype=jnp.int32) % Sp
    s_pos = PAST_LEN + jnp.arange(math.lcm(Ts, 2 * TOKEN_TILE), dtype=jnp.int32) % Ts

    prompt = dict(x=x_prompt, pos=p_pos, state=(zc, zn, zm), cache=None, tag="p", dtype=BF16)
    sample = dict(x=x_sample, pos=s_pos, state=s_state, cache=(cache_swa_k, cache_swa_v), tag="s", dtype=F32)
    (y_p, p_c, p_n, p_m, p_k, p_v), (y_s, s_c, s_n, s_m, s_k, s_v) = _forward([prompt, sample], p)
    return (y_p, y_s, p_c, p_n, p_m, p_k, p_v, s_c, s_n, s_m, s_k, s_v)
```

```python
import functools
import math

import jax
import jax.numpy as jnp
from jax import lax
from jax.experimental import pallas as pl
from jax.experimental.pallas import tpu as pltpu

F32 = jnp.float32
BF16 = jnp.bfloat16

EPS = 1e-6
GATE_CAP = 15.0
A_HEADS = 4
MLSTM_CHUNK = 256
B_HEAD_DIM = 64
B_KV_HEADS = 4
WINDOW = 128
ROT_DIM = B_HEAD_DIM // 4
ROPE_THETA = 500000.0
PAST_LEN = 8192

LANES = 128
V7X_VMEM_LIMIT_BYTES = 60 * 1024 * 1024
TOKEN_TILE = 512
FFN_SUBTILE = 512
FF_TILE = 256
CAST_STEPS = 16
SWA_BLOCKS_PER_STEP = 4


def _rms(x, g):
    return (x * lax.rsqrt(jnp.mean(x * x, axis=-1, keepdims=True) + EPS)) * g


def _dot(a, b):
    return jnp.dot(a, b, preferred_element_type=F32)


def _dot_nt(a, b):
    return lax.dot_general(a, b, (((1,), (1,)), ((), ())), preferred_element_type=F32)


def _dot_f32(a, b):
    return jnp.dot(a, b, preferred_element_type=F32, precision=lax.Precision.HIGHEST)


def _at(arr, *idx):
    return (arr, idx)


def _split(param):
    return param if isinstance(param, tuple) else (param, ())


def _shape(param):
    arr, idx = _split(param)
    return arr.shape[len(idx):]


def _resident(param):
    arr, idx = _split(param)
    block = (None,) * len(idx) + arr.shape[len(idx):]
    index = idx + (0,) * (arr.ndim - len(idx))
    return arr, pl.BlockSpec(block, lambda *_: index, pipeline_mode=pl.Buffered(1))


def _operands(tiled, residents):
    pairs = list(tiled) + [_resident(r) for r in residents]
    return [a for a, _ in pairs], [sp for _, sp in pairs]


def _rows(tm, width):
    return pl.BlockSpec((tm, width), lambda i: (i, 0))


def _params(*sem):
    return pltpu.CompilerParams(dimension_semantics=sem, vmem_limit_bytes=V7X_VMEM_LIMIT_BYTES)


def _token_tile(n):
    tm = TOKEN_TILE
    while n % tm:
        tm //= 2
    return tm


def _ffn_kernel(*refs, n_x, n_hg, n_cast, final, post, n_out, n_first, tf):
    refs = list(refs)
    x_refs = [refs.pop(0) for _ in range(n_x)]
    hg_refs = [refs.pop(0) for _ in range(n_hg)]
    cast_src = [refs.pop(0) for _ in range(n_cast)]
    if post:
        cos_refs, sin_refs = [refs.pop(0), refs.pop(0)], [refs.pop(0), refs.pop(0)]
    if n_hg:
        wmix_ref, bmix_ref = refs.pop(0), refs.pop(0)
    gn_ref, wg_ref, wu_ref, wd_ref = refs.pop(0), refs.pop(0), refs.pop(0), refs.pop(0)
    if final:
        fg_ref = refs.pop(0)
    if post:
        pgn_ref, pw_ref, pb_ref = refs.pop(0), refs.pop(0), refs.pop(0)
    o_refs = [refs.pop(0) for _ in range(n_out)]
    post_refs = [refs.pop(0) for _ in range({None: 0, "q": 1, "kv": 2}[post])]
    cast_dst = [refs.pop(0) for _ in range(n_cast)]
    (a_ref,) = refs
    first = pl.program_id(0) < n_first

    if n_cast:
        @pl.when(pl.program_id(0) < CAST_STEPS)
        def _():
            for src, dst in zip(cast_src, cast_dst):
                dst[...] = src[...].astype(dst.dtype)

    tm = a_ref.shape[0]
    sub = min(tm, FFN_SUBTILE)
    for r0 in range(0, tm, sub):
        rows = slice(r0, r0 + sub)

        def current(rs, dtype):
            vals = [r[rows, :].astype(dtype) for r in rs]
            return vals[0] if len(vals) == 1 else jnp.where(first, vals[0], vals[1])

        x = current(x_refs, F32)
        if n_hg:
            x = x + (_dot(current(hg_refs, BF16), wmix_ref[...]) + bmix_ref[...])
        h = _rms(x, gn_ref[...]).astype(BF16)
        for j in range(wg_ref.shape[1] // tf):
            sl = slice(j * tf, (j + 1) * tf)
            g = _dot(h, wg_ref[:, sl])
            u = _dot(h, wu_ref[:, sl])
            a_ref[rows, sl] = ((g * jax.nn.sigmoid(g)) * u).astype(BF16)
        out = x + 0.5 * _dot(a_ref[rows, :], wd_ref[...])
        if post:
            proj = _dot(_rms(out, pgn_ref[...]).astype(BF16), pw_ref[...]) + pb_ref[...]
            cos, sin = current(cos_refs, F32), current(sin_refs, F32)
            if post == "q":
                post_refs[0][rows, :] = (_rope(proj, cos, sin) * B_HEAD_DIM ** -0.5).astype(post_refs[0].dtype)
            else:
                kw = proj.shape[1] // 2
                post_refs[0][rows, :] = _rope(proj[:, :kw], cos, sin).astype(post_refs[0].dtype)
                post_refs[1][rows, :] = proj[:, kw:].astype(post_refs[1].dtype)
        if final:
            out = _rms(out, fg_ref[...])
        if n_out == 1:
            o_refs[0][rows, :] = out
        else:
            @pl.when(first)
            def _():
                o_refs[0][rows, :] = out

            @pl.when(jnp.logical_not(first))
            def _():
                o_refs[1][rows, :] = out


def _stream_rows(tm, width, n_first):
    return (pl.BlockSpec((tm, width), lambda i: (jnp.minimum(i, n_first - 1), 0)),
            pl.BlockSpec((tm, width), lambda i: (jnp.maximum(i - n_first, 0), 0)))


def _ffn(x, rows, gn, wg, wu, wd, mix=None, final_gain=None, split_out=False, cast=(), post=None, name="ffn"):
    n = sum(rows)
    d = _shape(gn)[1]
    dff = _shape(wg)[1]
    tf = FF_TILE if dff % FF_TILE == 0 else dff
    residents = ([mix[1], mix[2]] if mix is not None else []) + [gn, wg, wu, wd]
    if final_gain is not None:
        residents.append(final_gain)
    streamed = [x] + ([mix[0]] if mix is not None else [])
    post_widths = []
    if post is not None:
        kind, pgn, pw, pb, tables_first, tables_second, post_dtype = post
        residents += [pgn, pw, pb]
        streamed += [(tables_first[0], tables_second[0]), (tables_first[1], tables_second[1])]
        post_widths = [_shape(pw)[1]] if kind == "q" else [_shape(pw)[1] // 2] * 2
    flat_streamed = [a for s in streamed for a in (s if isinstance(s, tuple) else (s,))]
    cast_chunks = [(_shape(c)[0] // CAST_STEPS, _shape(c)[1]) for c in cast]

    nbytes = lambda shape, dtype: math.prod(shape) * jnp.dtype(dtype).itemsize
    fixed = (sum(nbytes(_shape(r), _split(r)[0].dtype) for r in residents)
             + 2 * sum(nbytes(c, F32) + nbytes(c, BF16) for c in cast_chunks))
    per_row = (2 * sum(nbytes(a.shape[1:], a.dtype) for a in flat_streamed)
               + 2 * (2 if split_out else 1) * d * 4 + dff * 2 + d * 4
               + 2 * sum(nbytes((w,), post_dtype) for w in post_widths))
    tm = 2 * TOKEN_TILE
    while math.gcd(*rows) % tm or fixed + tm * per_row > V7X_VMEM_LIMIT_BYTES:
        tm //= 2
    n_first = rows[0] // tm
    if cast and n // tm < CAST_STEPS:
        raise ValueError("too few grid steps for the weight rounding side job")

    def tiles(a):
        if isinstance(a, tuple):
            return list(zip(a, _stream_rows(tm, a[0].shape[1], n_first)))
        return [(a, _rows(tm, a.shape[1]))]

    tiled = tiles(x)
    n_x = len(tiled)
    n_hg = 0
    if mix is not None:
        tiled += tiles(mix[0])
        n_hg = len(tiled) - n_x
    chunk_of = lambda i: jnp.minimum(i, CAST_STEPS - 1)
    for c, chunk in zip(cast, cast_chunks):
        arr, idx = _split(c)
        tiled.append((arr, pl.BlockSpec((None,) * len(idx) + chunk, lambda i, idx=idx: idx + (chunk_of(i), 0))))
    if post is not None:
        blocks = [t[0].shape[0] // tm for t in (tables_first, tables_second)]
        table_specs = (pl.BlockSpec((tm, LANES), lambda i: (jnp.minimum(i, n_first - 1) % blocks[0], 0)),
                       pl.BlockSpec((tm, LANES), lambda i: (jnp.maximum(i - n_first, 0) % blocks[1], 0)))
        for which in (0, 1):
            tiled += list(zip((tables_first[which], tables_second[which]), table_specs))
    args, specs = _operands(tiled, residents)
    if split_out:
        out_specs = list(_stream_rows(tm, d, n_first))
        out_shape = [jax.ShapeDtypeStruct((r, d), F32) for r in rows]
    else:
        out_specs, out_shape = [_rows(tm, d)], [jax.ShapeDtypeStruct((n, d), F32)]
    n_out = len(out_specs)
    for w in post_widths:
        out_specs.append(_rows(tm, w))
        out_shape.append(jax.ShapeDtypeStruct((n, w), post_dtype))
    n_post = len(post_widths)
    out_specs += [pl.BlockSpec(chunk, lambda i: (chunk_of(i), 0)) for chunk in cast_chunks]
    out_shape += [jax.ShapeDtypeStruct(_shape(c), BF16) for c in cast]
    kern = functools.partial(_ffn_kernel, n_x=n_x, n_hg=n_hg, n_cast=len(cast), final=final_gain is not None,
                             post=post[0] if post is not None else None, n_out=n_out, n_first=n_first, tf=tf)
    outs = pl.pallas_call(
        kern,
        grid=(n // tm,),
        in_specs=specs,
        out_specs=out_specs,
        out_shape=out_shape,
        scratch_shapes=[pltpu.VMEM((tm, dff), BF16)],
        compiler_params=_params("arbitrary"),
        name=name,
    )(*args)
    y = tuple(outs[:n_out]) if split_out else outs[0]
    return y, list(outs[n_out:n_out + n_post]), tuple(outs[n_out + n_post:])


def _inproj_kernel(x_ref, gn_ref, w_ref, wgt_ref, bg_ref, q_ref, k_ref, v_ref, o_ref, g_ref, *gt_refs,
                   k_scale):
    h = _rms(x_ref[...], gn_ref[...]).astype(BF16)
    dqk, dv = q_ref.shape[1], v_ref.shape[1]
    qk = _dot(h, w_ref[:, :2 * dqk])
    q_ref[...] = qk[:, :dqk].astype(q_ref.dtype)
    k_ref[...] = (qk[:, dqk:] * k_scale).astype(k_ref.dtype)
    v_ref[...] = _dot(h, w_ref[:, 2 * dqk:2 * dqk + dv]).astype(v_ref.dtype)
    o_ref[...] = _dot(h, w_ref[:, 2 * dqk + dv:2 * dqk + 2 * dv])
    gates = _dot(h, wgt_ref[...]) + bg_ref[...]
    gates = GATE_CAP * jnp.tanh(gates / GATE_CAP)
    logsig = jnp.minimum(gates, 0.0) - jnp.log1p(jnp.exp(-jnp.abs(gates)))
    lane = lax.broadcasted_iota(jnp.int32, gates.shape, 1)
    gates = jnp.where(lane < A_HEADS, gates, logsig)
    g_ref[...] = gates
    for gt_ref in gt_refs:
        chunks, ng, L = gt_ref.shape
        for c in range(chunks):
            gt_ref[c] = gates[c * L:(c + 1) * L, :].T[:ng, :]


def _mlstm_chunk(S):
    return MLSTM_CHUNK if S % MLSTM_CHUNK == 0 else S


def _window_tile(row0, n):
    return _token_tile(math.gcd(row0, n))


def _window(xw):
    arr, row0, n = xw
    tm = _window_tile(row0, n)
    blk0 = row0 // tm
    return n, tm, (arr, pl.BlockSpec((tm, arr.shape[1]), lambda i: (i + blk0, 0)))


def _inproj(xw, seq_len, gn, w, wgt, bg, act_dtype, name):
    n, tm, x_op = _window(xw)
    dv = _shape(w)[0]
    dqk = dv // 2
    kern = functools.partial(_inproj_kernel, k_scale=(dqk // A_HEADS) ** -0.5)
    args, specs = _operands([x_op], [gn, w, wgt, bg])
    out_specs = [_rows(tm, dqk), _rows(tm, dqk), _rows(tm, dv), _rows(tm, dv), _rows(tm, LANES)]
    out_shape = [jax.ShapeDtypeStruct((n, dqk), act_dtype), jax.ShapeDtypeStruct((n, dqk), act_dtype),
                 jax.ShapeDtypeStruct((n, dv), act_dtype), jax.ShapeDtypeStruct((n, dv), F32),
                 jax.ShapeDtypeStruct((n, LANES), F32)]
    L = _mlstm_chunk(seq_len)
    with_gt = L % LANES == 0 and tm % L == 0
    if with_gt:
        out_specs.append(pl.BlockSpec((tm // L, 2 * A_HEADS, L), lambda i: (i, 0, 0)))
        out_shape.append(jax.ShapeDtypeStruct((n // L, 2 * A_HEADS, L), F32))
    outs = pl.pallas_call(
        kern,
        grid=(n // tm,),
        in_specs=specs,
        out_specs=out_specs,
        out_shape=out_shape,
        compiler_params=_params("parallel"),
        name=name,
    )(*args)
    return tuple(outs[:5]) + ((outs[5],) if with_gt else (None,))


def _rope(x, cos, sin):
    width = x.shape[1]
    reps = width // LANES
    cosw = jnp.concatenate([cos] * reps, axis=1)
    sinw = jnp.concatenate([sin] * reps, axis=1)
    half = ROT_DIM // 2
    lane = lax.broadcasted_iota(jnp.int32, (1, width), 1)
    first_half = (lane & (B_HEAD_DIM - 1)) < half
    partner = jnp.where(first_half, pltpu.roll(x, width - half, 1), pltpu.roll(x, half, 1))
    return x * cosw + partner * sinw


def _mlstm_kernel(q_ref, k_ref, v_ref, o_ref, gtm_ref, gt_ref, hn_ref, c0_ref, n0_ref, m0_ref, *rest,
                  bblk, chunk):
    out_ref, c_ref, n_ref, m_ref = rest[-4:]
    L = chunk
    dqk = q_ref.shape[2] // A_HEADS
    dv = v_ref.shape[2] // A_HEADS

    @pl.when(pl.program_id(1) == 0)
    def _():
        c_ref[...] = c0_ref[...]
        n_ref[...] = n0_ref[...]
        m_ref[...] = m0_ref[...]

    row = lax.broadcasted_iota(jnp.int32, (L, L), 0)
    col = lax.broadcasted_iota(jnp.int32, (L, L), 1)
    tril = col <= row
    lower_ones = tril.astype(F32)
    upper_ones = (row <= col).astype(F32)

    chains = [(b, h) for b in range(bblk) for h in range(A_HEADS)]
    per_chain = lambda f: jnp.stack([f(b, h) for b, h in chains])

    gts = [gt_ref[b, 0] for b in range(bblk)]
    gtms = [gtm_ref[b] for b in range(bblk)]
    csum_rows = [_dot_f32(gt, upper_ones) for gt in gts]
    csum_cols = [_dot_f32(lower_ones, gtm) for gtm in gtms]
    i_row = per_chain(lambda b, h: gts[b][h:h + 1, :])
    b_row = per_chain(lambda b, h: csum_rows[b][A_HEADS + h:A_HEADS + h + 1, :])
    i_col = per_chain(lambda b, h: gtms[b][:, h:h + 1])
    b_col = per_chain(lambda b, h: csum_cols[b][:, A_HEADS + h:A_HEADS + h + 1])
    b_last = b_col[:, L - 1:L, :]
    m_prev = per_chain(lambda b, h: m_ref[b, h:h + 1, :])
    n_prev = per_chain(lambda b, h: n_ref[b, h:h + 1, :])
    c_prev = per_chain(lambda b, h: c_ref[b, h])

    q = per_chain(lambda b, h: q_ref[b, :, h * dqk:(h + 1) * dqk]).astype(BF16)
    k = per_chain(lambda b, h: k_ref[b, :, h * dqk:(h + 1) * dqk]).astype(BF16)
    v = per_chain(lambda b, h: v_ref[b, :, h * dv:(h + 1) * dv]).astype(BF16)

    dmat = jnp.where(tril[None], (b_col - b_row) + i_row, -jnp.inf)
    g = b_col + m_prev
    mt = jnp.maximum(g, jnp.max(dmat, axis=-1, keepdims=True))
    w_inter = jnp.exp(g - mt)
    s = jnp.einsum("gtd,gsd->gts", q, k, preferred_element_type=F32) * jnp.exp(dmat - mt)
    G = len(chains)
    v_ones = jnp.concatenate([v, jnp.ones((G, L, LANES), BF16)], axis=-1)
    n_cols = jnp.broadcast_to(jnp.swapaxes(n_prev, 1, 2), (G, dqk, LANES))
    c_n = jnp.concatenate([c_prev, n_cols], axis=-1).astype(BF16)
    intra = jnp.einsum("gts,gsv->gtv", s.astype(BF16), v_ones, preferred_element_type=F32)
    inter = w_inter * jnp.einsum("gtd,gdv->gtv", q, c_n, preferred_element_type=F32)
    both = intra + inter
    den = jnp.maximum(jnp.abs(both[:, :, dv:]), jnp.exp(-mt))
    hc = both[:, :, :dv] / jnp.concatenate([den] * (dv // LANES), axis=-1)

    m_new = mt[:, L - 1:L, :]
    a_col = jnp.exp(((b_last - b_col) + i_col) - m_new)
    decay = jnp.exp((b_last + m_prev) - m_new)
    ka = k.astype(F32) * a_col
    kat = jnp.swapaxes(ka, 1, 2).astype(BF16)
    c_new = decay * c_prev + jnp.einsum("gds,gsv->gdv", kat, v, preferred_element_type=F32)
    n_new = decay * n_prev + jnp.sum(ka, axis=1, keepdims=True)

    gain = per_chain(lambda b, h: hn_ref[:, h * dv:(h + 1) * dv])
    gate = jax.nn.sigmoid(per_chain(lambda b, h: o_ref[b, :, h * dv:(h + 1) * dv]))
    res = (_rms(hc, gain) * gate).astype(out_ref.dtype)
    for gi, (b, h) in enumerate(chains):
        c_ref[b, h] = c_new[gi]
        n_ref[b, h:h + 1, :] = n_new[gi]
        m_ref[b, h:h + 1, :] = m_new[gi]
        out_ref[b, :, h * dv:(h + 1) * dv] = res[gi]


def _mlstm(q, k, v, o, gates, gt, head_norm, state0, layer0, layer, n_layers, prev_final, bblk, out_dtype,
           name):
    B, S, qw = q.shape
    vw = v.shape[2]
    L = _mlstm_chunk(S)
    nc = S // L
    ng = 2 * A_HEADS
    if gt is None:
        gt = jnp.swapaxes(gates[:, :, :ng].reshape(B, nc, L, ng), 2, 3)
    gt = gt.reshape(B, nc, ng, L)
    kern = functools.partial(_mlstm_kernel, bblk=bblk, chunk=L)
    seq = lambda w: pl.BlockSpec((bblk, L, w), lambda i, c: (i, c, 0))

    def state(a, row):
        return pl.BlockSpec((None, bblk) + a.shape[2:], lambda i, c: (row, i) + (0,) * (a.ndim - 2))

    hn_arr, hn_spec = _resident(head_norm)
    args = [q, k, v, o, gates, gt, hn_arr, *state0]
    specs = [seq(qw), seq(qw), seq(vw), seq(vw), seq(LANES),
             pl.BlockSpec((bblk, 1, ng, L), lambda i, c: (i, c, 0, 0)), hn_spec,
             *[state(a, layer0) for a in state0]]
    aliases = {}
    if prev_final is not None:
        aliases = {len(args) + t: 1 + t for t in range(len(prev_final))}
        args += list(prev_final)
        specs += [pl.BlockSpec(memory_space=pl.ANY)] * len(prev_final)
    return pl.pallas_call(
        kern,
        grid=(B // bblk, nc),
        in_specs=specs,
        out_specs=[seq(vw)] + [state(a, layer) for a in state0],
        out_shape=[jax.ShapeDtypeStruct((B, S, vw), out_dtype)]
                  + [jax.ShapeDtypeStruct((n_layers,) + a.shape[1:], F32) for a in state0],
        input_output_aliases=aliases,
        compiler_params=_params("parallel", "arbitrary"),
        name=name,
    )(*args)


def _pair_block_diag(x):
    zero = jnp.zeros_like(x)
    return jnp.concatenate([jnp.concatenate([x, zero], axis=1), jnp.concatenate([zero, x], axis=1)], axis=0)


def _swa_prompt_kernel(sinks_ref, q_ref, kp_ref, kc_ref, vp_ref, vc_ref, o_ref, *, group):
    W = WINDOW
    hd = B_HEAD_DIM
    pairs = group // 2
    R = pairs * W
    nblk = q_ref.shape[0] // W
    has_prev = pl.program_id(1) > 0
    t = lax.broadcasted_iota(jnp.int32, (R, 2 * W), 0) & (W - 1)
    s = lax.broadcasted_iota(jnp.int32, (R, 2 * W), 1) & (W - 1)
    use_cur = s <= t
    visible = jnp.logical_or(use_cur, has_prev)
    row = lax.broadcasted_iota(jnp.int32, (R, 1), 0)
    first_head = lax.broadcasted_iota(jnp.int32, (R, 2 * hd), 1) < hd
    ones = jnp.ones((W, hd), BF16)

    def key_blocks(ref_prev, ref_cur, blk, ksl):
        cur = ref_cur[blk * W:(blk + 1) * W, ksl]
        prev = ref_prev[:, ksl] if blk == 0 else ref_cur[(blk - 1) * W:blk * W, ksl]
        return prev.astype(BF16), cur.astype(BF16)

    def scores(item):
        blk, kh = item
        ksl = slice(kh * hd, (kh + 1) * hd)
        rows = slice(blk * W, (blk + 1) * W)
        q2 = jnp.concatenate(
            [q_ref[rows, (kh * pairs + pr) * 2 * hd:(kh * pairs + pr + 1) * 2 * hd] for pr in range(pairs)],
            axis=0).astype(BF16)
        k_prev, k_cur = key_blocks(kp_ref, kc_ref, blk, ksl)
        sc_cur = _dot_nt(q2, _pair_block_diag(k_cur))
        sc_prev = _dot_nt(q2, _pair_block_diag(k_prev))
        sc = jnp.where(use_cur, sc_cur, sc_prev)
        return jnp.where(visible, sc, -jnp.inf) if blk == 0 else sc

    def softmax(item, sc):
        _, kh = item
        probs, sink_terms = [], []
        for g in range(2):
            sink = jnp.full((R, 1), sinks_ref[0, kh * group + g], F32)
            for pr in range(1, pairs):
                sink = jnp.where(row >= pr * W, sinks_ref[0, kh * group + 2 * pr + g], sink)
            sg = sc[:, g * W:(g + 1) * W]
            mx = jnp.maximum(jnp.max(sg, axis=-1, keepdims=True), sink)
            probs.append(jnp.exp(sg - mx))
            sink_terms.append(jnp.exp(sink - mx))
        p = jnp.concatenate(probs, axis=1)
        p_cur = jnp.where(use_cur, p, 0.0).astype(BF16)
        p_prev = jnp.where(use_cur, 0.0, p).astype(BF16)
        return p_cur, p_prev, jnp.where(first_head, sink_terms[0], sink_terms[1])

    def weighted_values(item, p_cur, p_prev, sink_term):
        blk, kh = item
        v_prev, v_cur = key_blocks(vp_ref, vc_ref, blk, slice(kh * hd, (kh + 1) * hd))
        v_cur = jnp.concatenate([_pair_block_diag(v_cur), _pair_block_diag(ones)], axis=1)
        v_prev = jnp.concatenate([_pair_block_diag(v_prev), _pair_block_diag(ones)], axis=1)
        acc = _dot(p_cur, v_cur) + _dot(p_prev, v_prev)
        out = acc[:, :2 * hd] / (acc[:, 2 * hd:] + sink_term)
        for pr in range(pairs):
            lo = (kh * pairs + pr) * 2 * hd
            o_ref[0, blk * W:(blk + 1) * W, lo:lo + 2 * hd] = out[pr * W:(pr + 1) * W, :].astype(o_ref.dtype)

    items = [(blk, kh) for blk in range(nblk) for kh in range(B_KV_HEADS)]
    sc = {i: scores(items[i]) for i in range(min(2, len(items)))}
    probs = {0: softmax(items[0], sc.pop(0))}
    for i, item in enumerate(items):
        if i + 2 < len(items):
            sc[i + 2] = scores(items[i + 2])
        if i + 1 < len(items):
            probs[i + 1] = softmax(items[i + 1], sc.pop(i + 1))
        weighted_values(item, *probs.pop(i))


def _swa_prompt(q, k, v, row0, B, S, sinks, out_dtype, name):
    qw, kw = q.shape[1], k.shape[1]
    nb = S // WINDOW
    group = qw // kw
    if group % 2 or 2 * B_HEAD_DIM != LANES:
        raise ValueError("the prompt attention kernel pairs query heads into LANES-wide slices")
    nblk = SWA_BLOCKS_PER_STEP if nb % SWA_BLOCKS_PER_STEP == 0 and row0 % (SWA_BLOCKS_PER_STEP * WINDOW) == 0 else 1
    if row0 % (nblk * WINDOW):
        raise ValueError("the stream must start on a query block boundary")
    steps = nb // nblk
    first = row0 // (nblk * WINDOW)
    cur = lambda w: pl.BlockSpec((nblk * WINDOW, w), lambda b, j: (first + b * steps + j, 0))
    prev = lambda w: pl.BlockSpec((WINDOW, w), lambda b, j: (jnp.maximum((first + b * steps + j) * nblk - 1, 0), 0))
    kern = functools.partial(_swa_prompt_kernel, group=group)
    return pl.pallas_call(
        kern,
        grid=(B, steps),
        in_specs=[pl.BlockSpec(memory_space=pltpu.SMEM), cur(qw), prev(kw), cur(kw), prev(kw), cur(kw)],
        out_specs=pl.BlockSpec((1, nblk * WINDOW, qw), lambda b, j: (b, j, 0)),
        out_shape=jax.ShapeDtypeStruct((B, S, qw), out_dtype),
        compiler_params=_params("parallel", "parallel"),
        name=name,
    )(sinks, q, k, k, v, v)


def _swa_sample_kernel(sinks_ref, q_ref, kc_ref, kn_ref, vc_ref, vn_ref, o_ref, *window_refs, group, bblk):
    T = q_ref.shape[0] // bblk
    wb, kw = kc_ref.shape[1:]
    for buf_ref, new_ref, out_ref in zip((kc_ref, vc_ref), (kn_ref, vn_ref), window_refs):
        out_ref[:, :wb - T, :] = buf_ref[:, T:, :]
        out_ref[:, wb - T:, :] = new_ref[...].reshape(bblk, T, kw)
    hd = B_HEAD_DIM
    nq = B_KV_HEADS * group
    R = nq * T
    per_seq = lambda ref: ref[...].astype(F32).reshape(bblk, T, ref.shape[1])
    k_new, v_new = per_seq(kn_ref).astype(BF16), per_seq(vn_ref).astype(BF16)
    qb = per_seq(q_ref)
    blocks = []
    for kh in range(B_KV_HEADS):
        for g in range(group):
            lo = (kh * group + g) * hd
            parts = [qb[:, :, lo:lo + hd]]
            if kh:
                parts.insert(0, jnp.zeros((bblk, T, kh * hd), qb.dtype))
            if kh < B_KV_HEADS - 1:
                parts.append(jnp.zeros((bblk, T, (B_KV_HEADS - 1 - kh) * hd), qb.dtype))
            blocks.append(jnp.concatenate(parts, axis=-1))
    q_all = jnp.concatenate(blocks, axis=1).astype(BF16)

    t = lax.broadcasted_iota(jnp.int32, (R, wb), 0) % T
    i = lax.broadcasted_iota(jnp.int32, (R, wb), 1)
    mask_buf = jnp.logical_and((t + wb) - i < WINDOW, (PAST_LEN - wb) + i >= 0)
    tn = lax.broadcasted_iota(jnp.int32, (R, T), 0) % T
    u = lax.broadcasted_iota(jnp.int32, (R, T), 1)
    mask_new = u <= tn
    ridx = lax.broadcasted_iota(jnp.int32, (R, 1), 0)
    sink = jnp.full((R, 1), sinks_ref[0, 0], F32)
    for hq in range(1, nq):
        sink = jnp.where(ridx >= hq * T, sinks_ref[0, hq], sink)

    sc_buf = jnp.einsum("bqd,bkd->bqk", q_all, kc_ref[...].astype(BF16), preferred_element_type=F32)
    sc_new = jnp.einsum("bqd,bkd->bqk", q_all, k_new, preferred_element_type=F32)
    sc_buf = jnp.where(mask_buf[None], sc_buf, -jnp.inf)
    sc_new = jnp.where(mask_new[None], sc_new, -jnp.inf)
    mx = jnp.maximum(jnp.maximum(jnp.max(sc_buf, axis=-1, keepdims=True),
                                 jnp.max(sc_new, axis=-1, keepdims=True)), sink[None])
    p_buf = jnp.exp(sc_buf - mx)
    p_new = jnp.exp(sc_new - mx)
    denom = (jnp.sum(p_buf, axis=-1, keepdims=True) + jnp.sum(p_new, axis=-1, keepdims=True)
             + jnp.exp(sink[None] - mx))
    acc = (jnp.einsum("bqk,bkd->bqd", p_buf.astype(BF16), vc_ref[...].astype(BF16), preferred_element_type=F32)
           + jnp.einsum("bqk,bkd->bqd", p_new.astype(BF16), v_new, preferred_element_type=F32))
    out = (acc / denom).astype(o_ref.dtype)
    for kh in range(B_KV_HEADS):
        for g in range(group):
            hq = kh * group + g
            o_ref[:, :, hq * hd:(hq + 1) * hd] = out[:, hq * T:(hq + 1) * T, kh * hd:(kh + 1) * hd]


def _swa_sample(q, k_buf, k_new, v_buf, v_new, row0, T, sinks, bblk, with_window, name):
    qw = q.shape[1]
    B, wb, kw = k_buf.shape
    group = qw // kw
    if row0 % (bblk * T):
        raise ValueError("the stream must start on a sequence block boundary")
    if with_window and not 0 < T < wb:
        raise ValueError("the window buffer must be longer than the new tokens")
    first = row0 // (bblk * T)
    blk = lambda r, w: pl.BlockSpec((bblk, r, w), lambda i: (i, 0, 0))
    new = lambda w: pl.BlockSpec((bblk * T, w), lambda i: (first + i, 0))
    kern = functools.partial(_swa_sample_kernel, group=group, bblk=bblk)
    return pl.pallas_call(
        kern,
        grid=(B // bblk,),
        in_specs=[pl.BlockSpec(memory_space=pltpu.SMEM), new(qw), blk(wb, kw), new(kw), blk(wb, kw), new(kw)],
        out_specs=[blk(T, qw)] + with_window * [blk(wb, kw), blk(wb, kw)],
        out_shape=[jax.ShapeDtypeStruct((B, T, qw), BF16)]
                  + with_window * [jax.ShapeDtypeStruct((B, wb, kw), F32)] * 2,
        compiler_params=_params("parallel"),
        name=name,
    )(sinks, q, k_buf, k_new, v_buf, v_new)


def _rope_tables(pos):
    half = ROT_DIM // 2
    inv_freq = jnp.exp(-math.log(ROPE_THETA) * jnp.arange(0, ROT_DIM, 2, dtype=F32) / ROT_DIM)
    ang = pos.astype(F32)[:, None] * inv_freq[None, :]
    d = jnp.arange(LANES, dtype=jnp.int32) % B_HEAD_DIM
    ang_l = jnp.take(ang, d % half, axis=1)
    rot = (d < ROT_DIM)[None, :]
    sign = jnp.where(d < half, -1.0, 1.0)[None, :]
    cos = jnp.where(rot, jnp.cos(ang_l), 1.0)
    sin = jnp.where(rot, jnp.sin(ang_l) * sign, 0.0)
    return cos.astype(F32), sin.astype(F32)


def _largest_divisor(n, cap):
    d = min(n, cap)
    while n % d:
        d -= 1
    return d


def _forward(streams, p):
    n_a = p["a_w_in32"].shape[0]
    depth = p["ffn_norm"].shape[0]
    rows = tuple(st["x"].shape[0] * st["x"].shape[1] for st in streams)
    x = tuple(st["x"].reshape(n, -1) for st, n in zip(streams, rows))

    def window(i):
        if isinstance(x, tuple):
            return (x[i], 0, rows[i])
        return (x, sum(rows[:i]), rows[i])

    ffn_order = [(l, half) for l in range(depth) for half in (0, 1)]
    ffn_w = tuple(w[0, 0].astype(BF16) for w in p["ffn_w32"])

    def ffn_after(l, half):
        k = ffn_order.index((l, half)) + 1
        return tuple(_at(w, *ffn_order[k]) for w in p["ffn_w32"]) if k < len(ffn_order) else ()

    final_state = [None for _ in streams]
    kv_new = None
    windows = [None for _ in streams]
    if not 0 < n_a < depth:
        raise ValueError("expected mLSTM layers followed by attention layers")
    tables = [_rope_tables(st["pos"]) for st in streams]
    for l in range(depth):
        post = None
        if l >= n_a:
            post = ("q", _at(p["mix_norm"], l), _at(p["b_w_q"], l - n_a), _at(p["b_b_q"], l - n_a),
                    *tables, BF16)
        mixer_w = (_at(p["a_w_in32"], l),) if l < n_a else ()
        x, posts, casts = _ffn(x, rows, _at(p["ffn_norm"], l, 0), *ffn_w, cast=ffn_after(l, 0) + mixer_w,
                               post=post, name=f"ffn_{l}a")
        ffn_w, mixer_w = casts[:len(casts) - len(mixer_w)], casts[len(casts) - len(mixer_w):]
        mixed = []
        for i, st in enumerate(streams):
            B, T, _ = st["x"].shape
            tag, act = st["tag"], st["dtype"]
            seq = lambda a: a.reshape(B, T, a.shape[-1])
            if l < n_a:
                q, k, v, o, gates, gt = _inproj(window(i), T, _at(p["mix_norm"], l), mixer_w[0],
                                                _at(p["a_wgt"], l), _at(p["a_bg"], l), act,
                                                name=f"inproj_{tag}_{l}")
                bblk = _largest_divisor(B, 2 if T >= MLSTM_CHUNK else 8)
                layer0 = l if st["state"][0].shape[0] == n_a else 0
                hg, *final_state[i] = _mlstm(seq(q), seq(k), seq(v), seq(o), seq(gates), gt,
                                             _at(p["a_head_norm"], l), st["state"], layer0, l, n_a,
                                             final_state[i], bblk, BF16, name=f"mlstm_{tag}_{l}")
                mixed.append(hg.reshape(B * T, -1))
            else:
                j = l - n_a
                row0 = sum(rows[:i])
                if st["cache"] is None:
                    o = _swa_prompt(posts[0], *kv_new, row0, B, T, p["b_sinks"][j], BF16, name=f"swa_{tag}_{l}")
                else:
                    kc, vc = st["cache"]
                    o, *window = _swa_sample(posts[0], kc.reshape(B, kc.shape[1], -1), kv_new[0],
                                             vc.reshape(B, vc.shape[1], -1), kv_new[1], row0, T,
                                             p["b_sinks"][j], _largest_divisor(B, 8), j == 0,
                                             name=f"swa_{tag}_{l}")
                    if window:
                        windows[i] = window
                mixed.append(o.reshape(B * T, -1))
        if l < n_a:
            mix = (tuple(mixed), _at(p["a_w_out"], l), p["zero_bias"])
        else:
            mix = (tuple(mixed), _at(p["b_w_o"], l - n_a), _at(p["b_b_o"], l - n_a))
        last = l == depth - 1
        post = ("kv", p["kv_norm"], p["w_kv"], p["b_kv"], *tables, F32) if l == n_a - 1 else None
        x, posts, ffn_w = _ffn(x, rows, _at(p["ffn_norm"], l, 1), *ffn_w, mix=mix, cast=ffn_after(l, 1),
                               post=post, final_gain=p["final_norm"] if last else None, split_out=last,
                               name=f"ffn_{l}b")
        if post is not None:
            kv_new = posts

    results = []
    for i, st in enumerate(streams):
        B, T, D = st["x"].shape
        new_c, new_n, new_m = final_state[i]
        if st["cache"] is None:
            row0 = sum(rows[:i])
            k_new, v_new = (a[row0:row0 + rows[i]].reshape(B, T, -1) for a in kv_new)
            wb = min(WINDOW, T)
            buf_k, buf_v = k_new[:, -wb:], v_new[:, -wb:]
        else:
            wb = st["cache"][0].shape[1]
            buf_k, buf_v = windows[i]
        heads = lambda a: a.reshape(B, wb, B_KV_HEADS, B_HEAD_DIM)
        results.append((x[i].reshape(B, T, D), new_c, new_n, new_m[..., 0], heads(buf_k), heads(buf_v)))
    return results


def kernel(x_prompt, x_sample, state_mlstm_C, state_mlstm_n, state_mlstm_m, cache_swa_k, cache_swa_v,
           ffn_norm, ffn_w_gate, ffn_w_up, ffn_w_down, mix_norm, a_w_in, a_b_gate, a_head_norm, a_w_out,
           kv_norm, w_kv, b_kv, b_w_q, b_b_q, b_sinks, b_w_o, b_b_o, final_norm):
    n_a, d_model, a_proj = a_w_in.shape
    ng = 2 * A_HEADS
    qk_w = d_model // 2
    v_w = d_model
    if a_proj != 2 * qk_w + 2 * v_w + ng:
        raise ValueError("unexpected mLSTM projection width")
    row = lambda a: a[..., None, :].astype(F32)
    p = {
        "ffn_norm": row(ffn_norm), "mix_norm": row(mix_norm), "kv_norm": row(kv_norm),
        "final_norm": row(final_norm), "a_head_norm": row(a_head_norm),
        "ffn_w32": (ffn_w_gate, ffn_w_up, ffn_w_down),
        "a_w_in32": a_w_in,
        "a_wgt": jnp.pad(a_w_in[:, :, 2 * qk_w + 2 * v_w:], ((0, 0), (0, 0), (0, LANES - ng))).astype(BF16),
        "a_bg": jnp.pad(a_b_gate, ((0, 0), (0, LANES - ng)))[:, None, :].astype(F32),
        "a_w_out": a_w_out.astype(BF16),
        "zero_bias": jnp.zeros((1, d_model), F32),
        "w_kv": w_kv.astype(BF16), "b_kv": row(b_kv),
        "b_w_q": b_w_q.astype(BF16), "b_b_q": row(b_b_q), "b_sinks": row(b_sinks),
        "b_w_o": b_w_o.astype(BF16), "b_b_o": row(b_b_o),
    }
    Bp, Sp, _ = x_prompt.shape
    Bs, Ts, _ = x_sample.shape
    zc = jnp.zeros((1, Bp) + state_mlstm_C.shape[2:], F32)
    zn = jnp.zeros((1, Bp) + state_mlstm_n.shape[2:], F32)
    zm = jnp.zeros((1, Bp) + state_mlstm_m.shape[2:] + (1,), F32)
    s_state = (state_mlstm_C, state_mlstm_n, state_mlstm_m[..., None])

    p_pos = jnp.arange(math.lcm(Sp, 2 * TOKEN_TILE), dtype=jnp.int32) % Sp
    s_pos = PAST_LEN + jnp.arange(math.lcm(Ts, 2 * TOKEN_TILE), dtype=jnp.int32) % Ts

    prompt = dict(x=x_prompt, pos=p_pos, state=(zc, zn, zm), cache=None, tag="p", dtype=BF16)
    sample = dict(x=x_sample, pos=s_pos, state=s_state, cache=(cache_swa_k, cache_swa_v), tag="s", dtype=F32)
    (y_p, p_c, p_n, p_m, p_k, p_v), (y_s, s_c, s_n, s_m, s_k, s_v) = _forward([prompt, sample], p)
    return (y_p, y_s, p_c, p_n, p_m, p_k, p_v, s_c, s_n, s_m, s_k, s_v)
```

```python
import functools
import math

import jax
import jax.numpy as jnp
from jax import lax
from jax.experimental import pallas as pl
from jax.experimental.pallas import tpu as pltpu

F32 = jnp.float32
BF16 = jnp.bfloat16

EPS = 1e-6
GATE_CAP = 15.0
A_HEADS = 4
MLSTM_CHUNK = 256
B_HEAD_DIM = 64
B_KV_HEADS = 4
WINDOW = 128
ROT_DIM = B_HEAD_DIM // 4
ROPE_THETA = 500000.0
PAST_LEN = 8192

LANES = 128
V7X_VMEM_LIMIT_BYTES = 60 * 1024 * 1024
TOKEN_TILE = 512
FFN_SUBTILE = 512
FF_TILE = 256
CAST_STEPS = 16
SWA_BLOCKS_PER_STEP = 8


def _rms(x, g):
    return (x * lax.rsqrt(jnp.mean(x * x, axis=-1, keepdims=True) + EPS)) * g


def _dot(a, b):
    return jnp.dot(a, b, preferred_element_type=F32)


def _dot_nt(a, b):
    return lax.dot_general(a, b, (((1,), (1,)), ((), ())), preferred_element_type=F32)


def _dot_f32(a, b):
    return jnp.dot(a, b, preferred_element_type=F32, precision=lax.Precision.HIGHEST)


def _at(arr, *idx):
    return (arr, idx)


def _split(param):
    return param if isinstance(param, tuple) else (param, ())


def _shape(param):
    arr, idx = _split(param)
    return arr.shape[len(idx):]


def _resident(param):
    arr, idx = _split(param)
    block = (None,) * len(idx) + arr.shape[len(idx):]
    index = idx + (0,) * (arr.ndim - len(idx))
    return arr, pl.BlockSpec(block, lambda *_: index, pipeline_mode=pl.Buffered(1))


def _operands(tiled, residents):
    pairs = list(tiled) + [_resident(r) for r in residents]
    return [a for a, _ in pairs], [sp for _, sp in pairs]


def _rows(tm, width):
    return pl.BlockSpec((tm, width), lambda i: (i, 0))


def _params(*sem):
    return pltpu.CompilerParams(dimension_semantics=sem, vmem_limit_bytes=V7X_VMEM_LIMIT_BYTES)


def _token_tile(n):
    tm = 2 * TOKEN_TILE
    while n % tm:
        tm //= 2
    return tm


def _ffn_kernel(*refs, n_x, n_hg, n_cast, final, post, n_out, n_first, tf):
    refs = list(refs)
    x_refs = [refs.pop(0) for _ in range(n_x)]
    hg_refs = [refs.pop(0) for _ in range(n_hg)]
    cast_src = [refs.pop(0) for _ in range(n_cast)]
    if post:
        cos_refs, sin_refs = [refs.pop(0), refs.pop(0)], [refs.pop(0), refs.pop(0)]
    if n_hg:
        wmix_ref, bmix_ref = refs.pop(0), refs.pop(0)
    gn_ref, wg_ref, wu_ref, wd_ref = refs.pop(0), refs.pop(0), refs.pop(0), refs.pop(0)
    if final:
        fg_ref = refs.pop(0)
    if post:
        pgn_ref, pw_ref, pb_ref = refs.pop(0), refs.pop(0), refs.pop(0)
    o_refs = [refs.pop(0) for _ in range(n_out)]
    post_refs = [refs.pop(0) for _ in range({None: 0, "q": 1, "kv": 2}[post])]
    cast_dst = [refs.pop(0) for _ in range(n_cast)]
    (a_ref,) = refs
    first = pl.program_id(0) < n_first

    if n_cast:
        @pl.when(pl.program_id(0) < CAST_STEPS)
        def _():
            for src, dst in zip(cast_src, cast_dst):
                dst[...] = src[...].astype(dst.dtype)

    tm = a_ref.shape[0]
    sub = min(tm, FFN_SUBTILE)
    for r0 in range(0, tm, sub):
        rows = slice(r0, r0 + sub)

        def current(rs, dtype):
            vals = [r[rows, :].astype(dtype) for r in rs]
            return vals[0] if len(vals) == 1 else jnp.where(first, vals[0], vals[1])

        x = current(x_refs, F32)
        if n_hg:
            x = x + (_dot(current(hg_refs, BF16), wmix_ref[...]) + bmix_ref[...])
        h = _rms(x, gn_ref[...]).astype(BF16)
        for j in range(wg_ref.shape[1] // tf):
            sl = slice(j * tf, (j + 1) * tf)
            g = _dot(h, wg_ref[:, sl])
            u = _dot(h, wu_ref[:, sl])
            a_ref[rows, sl] = ((g * jax.nn.sigmoid(g)) * u).astype(BF16)
        out = x + 0.5 * _dot(a_ref[rows, :], wd_ref[...])
        if post:
            proj = _dot(_rms(out, pgn_ref[...]).astype(BF16), pw_ref[...]) + pb_ref[...]
            cos, sin = current(cos_refs, F32), current(sin_refs, F32)
            if post == "q":
                post_refs[0][rows, :] = (_rope(proj, cos, sin) * B_HEAD_DIM ** -0.5).astype(post_refs[0].dtype)
            else:
                kw = proj.shape[1] // 2
                post_refs[0][rows, :] = _rope(proj[:, :kw], cos, sin).astype(post_refs[0].dtype)
                post_refs[1][rows, :] = proj[:, kw:].astype(post_refs[1].dtype)
        if final:
            out = _rms(out, fg_ref[...])
        if n_out == 1:
            o_refs[0][rows, :] = out
        else:
            @pl.when(first)
            def _():
                o_refs[0][rows, :] = out

            @pl.when(jnp.logical_not(first))
            def _():
                o_refs[1][rows, :] = out


def _stream_rows(tm, width, n_first):
    return (pl.BlockSpec((tm, width), lambda i: (jnp.minimum(i, n_first - 1), 0)),
            pl.BlockSpec((tm, width), lambda i: (jnp.maximum(i - n_first, 0), 0)))


def _ffn(x, rows, gn, wg, wu, wd, mix=None, final_gain=None, split_out=False, cast=(), post=None, name="ffn"):
    n = sum(rows)
    d = _shape(gn)[1]
    dff = _shape(wg)[1]
    tf = FF_TILE if dff % FF_TILE == 0 else dff
    residents = ([mix[1], mix[2]] if mix is not None else []) + [gn, wg, wu, wd]
    if final_gain is not None:
        residents.append(final_gain)
    streamed = [x] + ([mix[0]] if mix is not None else [])
    post_widths = []
    if post is not None:
        kind, pgn, pw, pb, tables_first, tables_second, post_dtype = post
        residents += [pgn, pw, pb]
        streamed += [(tables_first[0], tables_second[0]), (tables_first[1], tables_second[1])]
        post_widths = [_shape(pw)[1]] if kind == "q" else [_shape(pw)[1] // 2] * 2
    flat_streamed = [a for s in streamed for a in (s if isinstance(s, tuple) else (s,))]
    cast_chunks = [(_shape(c)[0] // CAST_STEPS, _shape(c)[1]) for c in cast]

    nbytes = lambda shape, dtype: math.prod(shape) * jnp.dtype(dtype).itemsize
    fixed = (sum(nbytes(_shape(r), _split(r)[0].dtype) for r in residents)
             + 2 * sum(nbytes(c, F32) + nbytes(c, BF16) for c in cast_chunks))
    per_row = (2 * sum(nbytes(a.shape[1:], a.dtype) for a in flat_streamed)
               + 2 * (2 if split_out else 1) * d * 4 + dff * 2 + d * 4
               + 2 * sum(nbytes((w,), post_dtype) for w in post_widths))
    tm = 2 * TOKEN_TILE
    while math.gcd(*rows) % tm or fixed + tm * per_row > V7X_VMEM_LIMIT_BYTES:
        tm //= 2
    n_first = rows[0] // tm
    if cast and n // tm < CAST_STEPS:
        raise ValueError("too few grid steps for the weight rounding side job")

    def tiles(a):
        if isinstance(a, tuple):
            return list(zip(a, _stream_rows(tm, a[0].shape[1], n_first)))
        return [(a, _rows(tm, a.shape[1]))]

    tiled = tiles(x)
    n_x = len(tiled)
    n_hg = 0
    if mix is not None:
        tiled += tiles(mix[0])
        n_hg = len(tiled) - n_x
    chunk_of = lambda i: jnp.minimum(i, CAST_STEPS - 1)
    for c, chunk in zip(cast, cast_chunks):
        arr, idx = _split(c)
        tiled.append((arr, pl.BlockSpec((None,) * len(idx) + chunk, lambda i, idx=idx: idx + (chunk_of(i), 0))))
    if post is not None:
        blocks = [t[0].shape[0] // tm for t in (tables_first, tables_second)]
        table_specs = (pl.BlockSpec((tm, LANES), lambda i: (jnp.minimum(i, n_first - 1) % blocks[0], 0)),
                       pl.BlockSpec((tm, LANES), lambda i: (jnp.maximum(i - n_first, 0) % blocks[1], 0)))
        for which in (0, 1):
            tiled += list(zip((tables_first[which], tables_second[which]), table_specs))
    args, specs = _operands(tiled, residents)
    if split_out:
        out_specs = list(_stream_rows(tm, d, n_first))
        out_shape = [jax.ShapeDtypeStruct((r, d), F32) for r in rows]
    else:
        out_specs, out_shape = [_rows(tm, d)], [jax.ShapeDtypeStruct((n, d), F32)]
    n_out = len(out_specs)
    for w in post_widths:
        out_specs.append(_rows(tm, w))
        out_shape.append(jax.ShapeDtypeStruct((n, w), post_dtype))
    n_post = len(post_widths)
    out_specs += [pl.BlockSpec(chunk, lambda i: (chunk_of(i), 0)) for chunk in cast_chunks]
    out_shape += [jax.ShapeDtypeStruct(_shape(c), BF16) for c in cast]
    kern = functools.partial(_ffn_kernel, n_x=n_x, n_hg=n_hg, n_cast=len(cast), final=final_gain is not None,
                             post=post[0] if post is not None else None, n_out=n_out, n_first=n_first, tf=tf)
    outs = pl.pallas_call(
        kern,
        grid=(n // tm,),
        in_specs=specs,
        out_specs=out_specs,
        out_shape=out_shape,
        scratch_shapes=[pltpu.VMEM((tm, dff), BF16)],
        compiler_params=_params("arbitrary"),
        name=name,
    )(*args)
    y = tuple(outs[:n_out]) if split_out else outs[0]
    return y, list(outs[n_out:n_out + n_post]), tuple(outs[n_out + n_post:])


def _inproj_kernel(x_ref, gn_ref, w_ref, wgt_ref, bg_ref, q_ref, k_ref, v_ref, o_ref, g_ref, *gt_refs,
                   k_scale):
    h = _rms(x_ref[...], gn_ref[...]).astype(BF16)
    dqk, dv = q_ref.shape[1], v_ref.shape[1]
    qk = _dot(h, w_ref[:, :2 * dqk])
    q_ref[...] = qk[:, :dqk].astype(q_ref.dtype)
    k_ref[...] = (qk[:, dqk:] * k_scale).astype(k_ref.dtype)
    v_ref[...] = _dot(h, w_ref[:, 2 * dqk:2 * dqk + dv]).astype(v_ref.dtype)
    o_ref[...] = _dot(h, w_ref[:, 2 * dqk + dv:2 * dqk + 2 * dv])
    gates = _dot(h, wgt_ref[...]) + bg_ref[...]
    gates = GATE_CAP * jnp.tanh(gates / GATE_CAP)
    logsig = jnp.minimum(gates, 0.0) - jnp.log1p(jnp.exp(-jnp.abs(gates)))
    lane = lax.broadcasted_iota(jnp.int32, gates.shape, 1)
    gates = jnp.where(lane < A_HEADS, gates, logsig)
    g_ref[...] = gates
    for gt_ref in gt_refs:
        chunks, ng, L = gt_ref.shape
        for c in range(chunks):
            gt_ref[c] = gates[c * L:(c + 1) * L, :].T[:ng, :]


def _mlstm_chunk(S):
    return MLSTM_CHUNK if S % MLSTM_CHUNK == 0 else S


def _window_tile(row0, n):
    return _token_tile(math.gcd(row0, n))


def _window(xw):
    arr, row0, n = xw
    tm = _window_tile(row0, n)
    blk0 = row0 // tm
    return n, tm, (arr, pl.BlockSpec((tm, arr.shape[1]), lambda i: (i + blk0, 0)))


def _inproj(xw, seq_len, gn, w, wgt, bg, act_dtype, name):
    n, tm, x_op = _window(xw)
    dv = _shape(w)[0]
    dqk = dv // 2
    kern = functools.partial(_inproj_kernel, k_scale=(dqk // A_HEADS) ** -0.5)
    args, specs = _operands([x_op], [gn, w, wgt, bg])
    out_specs = [_rows(tm, dqk), _rows(tm, dqk), _rows(tm, dv), _rows(tm, dv), _rows(tm, LANES)]
    out_shape = [jax.ShapeDtypeStruct((n, dqk), act_dtype), jax.ShapeDtypeStruct((n, dqk), act_dtype),
                 jax.ShapeDtypeStruct((n, dv), act_dtype), jax.ShapeDtypeStruct((n, dv), F32),
                 jax.ShapeDtypeStruct((n, LANES), F32)]
    L = _mlstm_chunk(seq_len)
    with_gt = L % LANES == 0 and tm % L == 0
    if with_gt:
        out_specs.append(pl.BlockSpec((tm // L, 2 * A_HEADS, L), lambda i: (i, 0, 0)))
        out_shape.append(jax.ShapeDtypeStruct((n // L, 2 * A_HEADS, L), F32))
    outs = pl.pallas_call(
        kern,
        grid=(n // tm,),
        in_specs=specs,
        out_specs=out_specs,
        out_shape=out_shape,
        compiler_params=_params("parallel"),
        name=name,
    )(*args)
    return tuple(outs[:5]) + ((outs[5],) if with_gt else (None,))


def _rope(x, cos, sin):
    width = x.shape[1]
    reps = width // LANES
    cosw = jnp.concatenate([cos] * reps, axis=1)
    sinw = jnp.concatenate([sin] * reps, axis=1)
    half = ROT_DIM // 2
    lane = lax.broadcasted_iota(jnp.int32, (1, width), 1)
    first_half = (lane & (B_HEAD_DIM - 1)) < half
    partner = jnp.where(first_half, pltpu.roll(x, width - half, 1), pltpu.roll(x, half, 1))
    return x * cosw + partner * sinw


def _mlstm_kernel(q_ref, k_ref, v_ref, o_ref, gtm_ref, gt_ref, hn_ref, c0_ref, n0_ref, m0_ref, *rest,
                  bblk, chunk):
    out_ref, c_ref, n_ref, m_ref = rest[-4:]
    L = chunk
    dqk = q_ref.shape[2] // A_HEADS
    dv = v_ref.shape[2] // A_HEADS

    @pl.when(pl.program_id(1) == 0)
    def _():
        c_ref[...] = c0_ref[...]
        n_ref[...] = n0_ref[...]
        m_ref[...] = m0_ref[...]

    row = lax.broadcasted_iota(jnp.int32, (L, L), 0)
    col = lax.broadcasted_iota(jnp.int32, (L, L), 1)
    tril = col <= row
    lower_ones = tril.astype(F32)
    upper_ones = (row <= col).astype(F32)

    chains = [(b, h) for b in range(bblk) for h in range(A_HEADS)]
    per_chain = lambda f: jnp.stack([f(b, h) for b, h in chains])

    gts = [gt_ref[b, 0] for b in range(bblk)]
    gtms = [gtm_ref[b] for b in range(bblk)]
    csum_rows = [_dot_f32(gt, upper_ones) for gt in gts]
    csum_cols = [_dot_f32(lower_ones, gtm) for gtm in gtms]
    i_row = per_chain(lambda b, h: gts[b][h:h + 1, :])
    b_row = per_chain(lambda b, h: csum_rows[b][A_HEADS + h:A_HEADS + h + 1, :])
    i_col = per_chain(lambda b, h: gtms[b][:, h:h + 1])
    b_col = per_chain(lambda b, h: csum_cols[b][:, A_HEADS + h:A_HEADS + h + 1])
    b_last = b_col[:, L - 1:L, :]
    m_prev = per_chain(lambda b, h: m_ref[b, h:h + 1, :])
    n_prev = per_chain(lambda b, h: n_ref[b, h:h + 1, :])
    c_prev = per_chain(lambda b, h: c_ref[b, h])

    q = per_chain(lambda b, h: q_ref[b, :, h * dqk:(h + 1) * dqk]).astype(BF16)
    k = per_chain(lambda b, h: k_ref[b, :, h * dqk:(h + 1) * dqk]).astype(BF16)
    v = per_chain(lambda b, h: v_ref[b, :, h * dv:(h + 1) * dv]).astype(BF16)

    dmat = jnp.where(tril[None], (b_col - b_row) + i_row, -jnp.inf)
    g = b_col + m_prev
    mt = jnp.maximum(g, jnp.max(dmat, axis=-1, keepdims=True))
    w_inter = jnp.exp(g - mt)
    s = jnp.einsum("gtd,gsd->gts", q, k, preferred_element_type=F32) * jnp.exp(dmat - mt)
    G = len(chains)
    v_ones = jnp.concatenate([v, jnp.ones((G, L, LANES), BF16)], axis=-1)
    n_cols = jnp.broadcast_to(jnp.swapaxes(n_prev, 1, 2), (G, dqk, LANES))
    c_n = jnp.concatenate([c_prev, n_cols], axis=-1).astype(BF16)
    intra = jnp.einsum("gts,gsv->gtv", s.astype(BF16), v_ones, preferred_element_type=F32)
    inter = w_inter * jnp.einsum("gtd,gdv->gtv", q, c_n, preferred_element_type=F32)
    both = intra + inter
    den = jnp.maximum(jnp.abs(both[:, :, dv:]), jnp.exp(-mt))
    hc = both[:, :, :dv] / jnp.concatenate([den] * (dv // LANES), axis=-1)

    m_new = mt[:, L - 1:L, :]
    a_col = jnp.exp(((b_last - b_col) + i_col) - m_new)
    decay = jnp.exp((b_last + m_prev) - m_new)
    ka = k.astype(F32) * a_col
    kat = jnp.swapaxes(ka, 1, 2).astype(BF16)
    c_new = decay * c_prev + jnp.einsum("gds,gsv->gdv", kat, v, preferred_element_type=F32)
    n_new = decay * n_prev + jnp.sum(ka, axis=1, keepdims=True)

    gain = per_chain(lambda b, h: hn_ref[:, h * dv:(h + 1) * dv])
    gate = jax.nn.sigmoid(per_chain(lambda b, h: o_ref[b, :, h * dv:(h + 1) * dv]))
    res = (_rms(hc, gain) * gate).astype(out_ref.dtype)
    for gi, (b, h) in enumerate(chains):
        c_ref[b, h] = c_new[gi]
        n_ref[b, h:h + 1, :] = n_new[gi]
        m_ref[b, h:h + 1, :] = m_new[gi]
        out_ref[b, :, h * dv:(h + 1) * dv] = res[gi]


def _mlstm(q, k, v, o, gates, gt, head_norm, state0, layer0, layer, n_layers, prev_final, bblk, out_dtype,
           name):
    B, S, qw = q.shape
    vw = v.shape[2]
    L = _mlstm_chunk(S)
    nc = S // L
    ng = 2 * A_HEADS
    if gt is None:
        gt = jnp.swapaxes(gates[:, :, :ng].reshape(B, nc, L, ng), 2, 3)
    gt = gt.reshape(B, nc, ng, L)
    kern = functools.partial(_mlstm_kernel, bblk=bblk, chunk=L)
    seq = lambda w: pl.BlockSpec((bblk, L, w), lambda i, c: (i, c, 0))

    def state(a, row):
        return pl.BlockSpec((None, bblk) + a.shape[2:], lambda i, c: (row, i) + (0,) * (a.ndim - 2))

    hn_arr, hn_spec = _resident(head_norm)
    args = [q, k, v, o, gates, gt, hn_arr, *state0]
    specs = [seq(qw), seq(qw), seq(vw), seq(vw), seq(LANES),
             pl.BlockSpec((bblk, 1, ng, L), lambda i, c: (i, c, 0, 0)), hn_spec,
             *[state(a, layer0) for a in state0]]
    aliases = {}
    if prev_final is not None:
        aliases = {len(args) + t: 1 + t for t in range(len(prev_final))}
        args += list(prev_final)
        specs += [pl.BlockSpec(memory_space=pl.ANY)] * len(prev_final)
    return pl.pallas_call(
        kern,
        grid=(B // bblk, nc),
        in_specs=specs,
        out_specs=[seq(vw)] + [state(a, layer) for a in state0],
        out_shape=[jax.ShapeDtypeStruct((B, S, vw), out_dtype)]
                  + [jax.ShapeDtypeStruct((n_layers,) + a.shape[1:], F32) for a in state0],
        input_output_aliases=aliases,
        compiler_params=_params("parallel", "arbitrary"),
        name=name,
    )(*args)


def _pair_block_diag(x):
    zero = jnp.zeros_like(x)
    return jnp.concatenate([jnp.concatenate([x, zero], axis=1), jnp.concatenate([zero, x], axis=1)], axis=0)


def _swa_prompt_kernel(sinks_ref, q_ref, kp_ref, kc_ref, vp_ref, vc_ref, o_ref, *, group):
    W = WINDOW
    hd = B_HEAD_DIM
    pairs = group // 2
    R = pairs * W
    nblk = q_ref.shape[0] // W
    has_prev = pl.program_id(1) > 0
    t = lax.broadcasted_iota(jnp.int32, (R, 2 * W), 0) & (W - 1)
    s = lax.broadcasted_iota(jnp.int32, (R, 2 * W), 1) & (W - 1)
    use_cur = s <= t
    visible = jnp.logical_or(use_cur, has_prev)
    row = lax.broadcasted_iota(jnp.int32, (R, 1), 0)
    first_head = lax.broadcasted_iota(jnp.int32, (R, 2 * hd), 1) < hd
    ones = jnp.ones((W, hd), BF16)

    def key_blocks(ref_prev, ref_cur, blk, ksl):
        cur = ref_cur[blk * W:(blk + 1) * W, ksl]
        prev = ref_prev[:, ksl] if blk == 0 else ref_cur[(blk - 1) * W:blk * W, ksl]
        return prev.astype(BF16), cur.astype(BF16)

    def scores(item):
        blk, kh = item
        ksl = slice(kh * hd, (kh + 1) * hd)
        rows = slice(blk * W, (blk + 1) * W)
        q2 = jnp.concatenate(
            [q_ref[rows, (kh * pairs + pr) * 2 * hd:(kh * pairs + pr + 1) * 2 * hd] for pr in range(pairs)],
            axis=0).astype(BF16)
        k_prev, k_cur = key_blocks(kp_ref, kc_ref, blk, ksl)
        sc_cur = _dot_nt(q2, _pair_block_diag(k_cur))
        sc_prev = _dot_nt(q2, _pair_block_diag(k_prev))
        sc = jnp.where(use_cur, sc_cur, sc_prev)
        return jnp.where(visible, sc, -jnp.inf) if blk == 0 else sc

    def softmax(item, sc):
        _, kh = item
        probs, sink_terms = [], []
        for g in range(2):
            sink = jnp.full((R, 1), sinks_ref[0, kh * group + g], F32)
            for pr in range(1, pairs):
                sink = jnp.where(row >= pr * W, sinks_ref[0, kh * group + 2 * pr + g], sink)
            sg = sc[:, g * W:(g + 1) * W]
            mx = jnp.maximum(jnp.max(sg, axis=-1, keepdims=True), sink)
            probs.append(jnp.exp(sg - mx))
            sink_terms.append(jnp.exp(sink - mx))
        p = jnp.concatenate(probs, axis=1)
        p_cur = jnp.where(use_cur, p, 0.0).astype(BF16)
        p_prev = jnp.where(use_cur, 0.0, p).astype(BF16)
        return p_cur, p_prev, jnp.where(first_head, sink_terms[0], sink_terms[1])

    def weighted_values(item, p_cur, p_prev, sink_term):
        blk, kh = item
        v_prev, v_cur = key_blocks(vp_ref, vc_ref, blk, slice(kh * hd, (kh + 1) * hd))
        v_cur = jnp.concatenate([_pair_block_diag(v_cur), _pair_block_diag(ones)], axis=1)
        v_prev = jnp.concatenate([_pair_block_diag(v_prev), _pair_block_diag(ones)], axis=1)
        acc = _dot(p_cur, v_cur) + _dot(p_prev, v_prev)
        out = acc[:, :2 * hd] / (acc[:, 2 * hd:] + sink_term)
        for pr in range(pairs):
            lo = (kh * pairs + pr) * 2 * hd
            o_ref[0, blk * W:(blk + 1) * W, lo:lo + 2 * hd] = out[pr * W:(pr + 1) * W, :].astype(o_ref.dtype)

    items = [(blk, kh) for blk in range(nblk) for kh in range(B_KV_HEADS)]
    sc = {i: scores(items[i]) for i in range(min(2, len(items)))}
    probs = {0: softmax(items[0], sc.pop(0))}
    for i, item in enumerate(items):
        if i + 2 < len(items):
            sc[i + 2] = scores(items[i + 2])
        if i + 1 < len(items):
            probs[i + 1] = softmax(items[i + 1], sc.pop(i + 1))
        weighted_values(item, *probs.pop(i))


def _swa_prompt(q, k, v, row0, B, S, sinks, out_dtype, name):
    qw, kw = q.shape[1], k.shape[1]
    nb = S // WINDOW
    group = qw // kw
    if group % 2 or 2 * B_HEAD_DIM != LANES:
        raise ValueError("the prompt attention kernel pairs query heads into LANES-wide slices")
    nblk = SWA_BLOCKS_PER_STEP if nb % SWA_BLOCKS_PER_STEP == 0 and row0 % (SWA_BLOCKS_PER_STEP * WINDOW) == 0 else 1
    if row0 % (nblk * WINDOW):
        raise ValueError("the stream must start on a query block boundary")
    steps = nb // nblk
    first = row0 // (nblk * WINDOW)
    cur = lambda w: pl.BlockSpec((nblk * WINDOW, w), lambda b, j: (first + b * steps + j, 0))
    prev = lambda w: pl.BlockSpec((WINDOW, w), lambda b, j: (jnp.maximum((first + b * steps + j) * nblk - 1, 0), 0))
    kern = functools.partial(_swa_prompt_kernel, group=group)
    return pl.pallas_call(
        kern,
        grid=(B, steps),
        in_specs=[pl.BlockSpec(memory_space=pltpu.SMEM), cur(qw), prev(kw), cur(kw), prev(kw), cur(kw)],
        out_specs=pl.BlockSpec((1, nblk * WINDOW, qw), lambda b, j: (b, j, 0)),
        out_shape=jax.ShapeDtypeStruct((B, S, qw), out_dtype),
        compiler_params=_params("parallel", "parallel"),
        name=name,
    )(sinks, q, k, k, v, v)


def _swa_sample_kernel(sinks_ref, q_ref, kc_ref, kn_ref, vc_ref, vn_ref, o_ref, *window_refs, group, bblk):
    T = q_ref.shape[0] // bblk
    wb, kw = kc_ref.shape[1:]
    for buf_ref, new_ref, out_ref in zip((kc_ref, vc_ref), (kn_ref, vn_ref), window_refs):
        out_ref[:, :wb - T, :] = buf_ref[:, T:, :]
        out_ref[:, wb - T:, :] = new_ref[...].reshape(bblk, T, kw)
    hd = B_HEAD_DIM
    nq = B_KV_HEADS * group
    R = nq * T
    per_seq = lambda ref: ref[...].astype(F32).reshape(bblk, T, ref.shape[1])
    k_new, v_new = per_seq(kn_ref).astype(BF16), per_seq(vn_ref).astype(BF16)
    qb = per_seq(q_ref)
    blocks = []
    for kh in range(B_KV_HEADS):
        for g in range(group):
            lo = (kh * group + g) * hd
            parts = [qb[:, :, lo:lo + hd]]
            if kh:
                parts.insert(0, jnp.zeros((bblk, T, kh * hd), qb.dtype))
            if kh < B_KV_HEADS - 1:
                parts.append(jnp.zeros((bblk, T, (B_KV_HEADS - 1 - kh) * hd), qb.dtype))
            blocks.append(jnp.concatenate(parts, axis=-1))
    q_all = jnp.concatenate(blocks, axis=1).astype(BF16)

    t = lax.broadcasted_iota(jnp.int32, (R, wb), 0) % T
    i = lax.broadcasted_iota(jnp.int32, (R, wb), 1)
    mask_buf = jnp.logical_and((t + wb) - i < WINDOW, (PAST_LEN - wb) + i >= 0)
    tn = lax.broadcasted_iota(jnp.int32, (R, T), 0) % T
    u = lax.broadcasted_iota(jnp.int32, (R, T), 1)
    mask_new = u <= tn
    ridx = lax.broadcasted_iota(jnp.int32, (R, 1), 0)
    sink = jnp.full((R, 1), sinks_ref[0, 0], F32)
    for hq in range(1, nq):
        sink = jnp.where(ridx >= hq * T, sinks_ref[0, hq], sink)

    sc_buf = jnp.einsum("bqd,bkd->bqk", q_all, kc_ref[...].astype(BF16), preferred_element_type=F32)
    sc_new = jnp.einsum("bqd,bkd->bqk", q_all, k_new, preferred_element_type=F32)
    sc_buf = jnp.where(mask_buf[None], sc_buf, -jnp.inf)
    sc_new = jnp.where(mask_new[None], sc_new, -jnp.inf)
    mx = jnp.maximum(jnp.maximum(jnp.max(sc_buf, axis=-1, keepdims=True),
                                 jnp.max(sc_new, axis=-1, keepdims=True)), sink[None])
    p_buf = jnp.exp(sc_buf - mx)
    p_new = jnp.exp(sc_new - mx)
    denom = (jnp.sum(p_buf, axis=-1, keepdims=True) + jnp.sum(p_new, axis=-1, keepdims=True)
             + jnp.exp(sink[None] - mx))
    acc = (jnp.einsum("bqk,bkd->bqd", p_buf.astype(BF16), vc_ref[...].astype(BF16), preferred_element_type=F32)
           + jnp.einsum("bqk,bkd->bqd", p_new.astype(BF16), v_new, preferred_element_type=F32))
    out = (acc / denom).astype(o_ref.dtype)
    for kh in range(B_KV_HEADS):
        for g in range(group):
            hq = kh * group + g
            o_ref[:, :, hq * hd:(hq + 1) * hd] = out[:, hq * T:(hq + 1) * T, kh * hd:(kh + 1) * hd]


def _swa_sample(q, k_buf, k_new, v_buf, v_new, row0, T, sinks, bblk, with_window, name):
    qw = q.shape[1]
    B, wb, kw = k_buf.shape
    group = qw // kw
    if row0 % (bblk * T):
        raise ValueError("the stream must start on a sequence block boundary")
    if with_window and not 0 < T < wb:
        raise ValueError("the window buffer must be longer than the new tokens")
    first = row0 // (bblk * T)
    blk = lambda r, w: pl.BlockSpec((bblk, r, w), lambda i: (i, 0, 0))
    new = lambda w: pl.BlockSpec((bblk * T, w), lambda i: (first + i, 0))
    kern = functools.partial(_swa_sample_kernel, group=group, bblk=bblk)
    return pl.pallas_call(
        kern,
        grid=(B // bblk,),
        in_specs=[pl.BlockSpec(memory_space=pltpu.SMEM), new(qw), blk(wb, kw), new(kw), blk(wb, kw), new(kw)],
        out_specs=[blk(T, qw)] + with_window * [blk(wb, kw), blk(wb, kw)],
        out_shape=[jax.ShapeDtypeStruct((B, T, qw), BF16)]
                  + with_window * [jax.ShapeDtypeStruct((B, wb, kw), F32)] * 2,
        compiler_params=_params("parallel"),
        name=name,
    )(sinks, q, k_buf, k_new, v_buf, v_new)


def _rope_tables(pos):
    half = ROT_DIM // 2
    inv_freq = jnp.exp(-math.log(ROPE_THETA) * jnp.arange(0, ROT_DIM, 2, dtype=F32) / ROT_DIM)
    ang = pos.astype(F32)[:, None] * inv_freq[None, :]
    d = jnp.arange(LANES, dtype=jnp.int32) % B_HEAD_DIM
    ang_l = jnp.take(ang, d % half, axis=1)
    rot = (d < ROT_DIM)[None, :]
    sign = jnp.where(d < half, -1.0, 1.0)[None, :]
    cos = jnp.where(rot, jnp.cos(ang_l), 1.0)
    sin = jnp.where(rot, jnp.sin(ang_l) * sign, 0.0)
    return cos.astype(F32), sin.astype(F32)


def _largest_divisor(n, cap):
    d = min(n, cap)
    while n % d:
        d -= 1
    return d


def _forward(streams, p):
    n_a = p["a_w_in32"].shape[0]
    depth = p["ffn_norm"].shape[0]
    rows = tuple(st["x"].shape[0] * st["x"].shape[1] for st in streams)
    x = tuple(st["x"].reshape(n, -1) for st, n in zip(streams, rows))

    def window(i):
        if isinstance(x, tuple):
            return (x[i], 0, rows[i])
        return (x, sum(rows[:i]), rows[i])

    ffn_order = [(l, half) for l in range(depth) for half in (0, 1)]
    ffn_w = tuple(w[0, 0].astype(BF16) for w in p["ffn_w32"])

    def ffn_after(l, half):
        k = ffn_order.index((l, half)) + 1
        return tuple(_at(w, *ffn_order[k]) for w in p["ffn_w32"]) if k < len(ffn_order) else ()

    final_state = [None for _ in streams]
    kv_new = None
    windows = [None for _ in streams]
    if not 0 < n_a < depth:
        raise ValueError("expected mLSTM layers followed by attention layers")
    tables = [_rope_tables(st["pos"]) for st in streams]
    for l in range(depth):
        post = None
        if l >= n_a:
            post = ("q", _at(p["mix_norm"], l), _at(p["b_w_q"], l - n_a), _at(p["b_b_q"], l - n_a),
                    *tables, BF16)
        mixer_w = (_at(p["a_w_in32"], l),) if l < n_a else ()
        x, posts, casts = _ffn(x, rows, _at(p["ffn_norm"], l, 0), *ffn_w, cast=ffn_after(l, 0) + mixer_w,
                               post=post, name=f"ffn_{l}a")
        ffn_w, mixer_w = casts[:len(casts) - len(mixer_w)], casts[len(casts) - len(mixer_w):]
        mixed = []
        for i, st in enumerate(streams):
            B, T, _ = st["x"].shape
            tag, act = st["tag"], st["dtype"]
            seq = lambda a: a.reshape(B, T, a.shape[-1])
            if l < n_a:
                q, k, v, o, gates, gt = _inproj(window(i), T, _at(p["mix_norm"], l), mixer_w[0],
                                                _at(p["a_wgt"], l), _at(p["a_bg"], l), act,
                                                name=f"inproj_{tag}_{l}")
                bblk = _largest_divisor(B, 2 if T >= MLSTM_CHUNK else 8)
                layer0 = l if st["state"][0].shape[0] == n_a else 0
                hg, *final_state[i] = _mlstm(seq(q), seq(k), seq(v), seq(o), seq(gates), gt,
                                             _at(p["a_head_norm"], l), st["state"], layer0, l, n_a,
                                             final_state[i], bblk, BF16, name=f"mlstm_{tag}_{l}")
                mixed.append(hg.reshape(B * T, -1))
            else:
                j = l - n_a
                row0 = sum(rows[:i])
                if st["cache"] is None:
                    o = _swa_prompt(posts[0], *kv_new, row0, B, T, p["b_sinks"][j], BF16, name=f"swa_{tag}_{l}")
                else:
                    kc, vc = st["cache"]
                    o, *window = _swa_sample(posts[0], kc.reshape(B, kc.shape[1], -1), kv_new[0],
                                             vc.reshape(B, vc.shape[1], -1), kv_new[1], row0, T,
                                             p["b_sinks"][j], _largest_divisor(B, 8), j == 0,
                                             name=f"swa_{tag}_{l}")
                    if window:
                        windows[i] = window
                mixed.append(o.reshape(B * T, -1))
        if l < n_a:
            mix = (tuple(mixed), _at(p["a_w_out"], l), p["zero_bias"])
        else:
            mix = (tuple(mixed), _at(p["b_w_o"], l - n_a), _at(p["b_b_o"], l - n_a))
        last = l == depth - 1
        post = ("kv", p["kv_norm"], p["w_kv"], p["b_kv"], *tables, F32) if l == n_a - 1 else None
        x, posts, ffn_w = _ffn(x, rows, _at(p["ffn_norm"], l, 1), *ffn_w, mix=mix, cast=ffn_after(l, 1),
                               post=post, final_gain=p["final_norm"] if last else None, split_out=last,
                               name=f"ffn_{l}b")
        if post is not None:
            kv_new = posts

    results = []
    for i, st in enumerate(streams):
        B, T, D = st["x"].shape
        new_c, new_n, new_m = final_state[i]
        if st["cache"] is None:
            row0 = sum(rows[:i])
            k_new, v_new = (a[row0:row0 + rows[i]].reshape(B, T, -1) for a in kv_new)
            wb = min(WINDOW, T)
            buf_k, buf_v = k_new[:, -wb:], v_new[:, -wb:]
        else:
            wb = st["cache"][0].shape[1]
            buf_k, buf_v = windows[i]
        heads = lambda a: a.reshape(B, wb, B_KV_HEADS, B_HEAD_DIM)
        results.append((x[i].reshape(B, T, D), new_c, new_n, new_m[..., 0], heads(buf_k), heads(buf_v)))
    return results


def kernel(x_prompt, x_sample, state_mlstm_C, state_mlstm_n, state_mlstm_m, cache_swa_k, cache_swa_v,
           ffn_norm, ffn_w_gate, ffn_w_up, ffn_w_down, mix_norm, a_w_in, a_b_gate, a_head_norm, a_w_out,
           kv_norm, w_kv, b_kv, b_w_q, b_b_q, b_sinks, b_w_o, b_b_o, final_norm):
    n_a, d_model, a_proj = a_w_in.shape
    ng = 2 * A_HEADS
    qk_w = d_model // 2
    v_w = d_model
    if a_proj != 2 * qk_w + 2 * v_w + ng:
        raise ValueError("unexpected mLSTM projection width")
    row = lambda a: a[..., None, :].astype(F32)
    p = {
        "ffn_norm": row(ffn_norm), "mix_norm": row(mix_norm), "kv_norm": row(kv_norm),
        "final_norm": row(final_norm), "a_head_norm": row(a_head_norm),
        "ffn_w32": (ffn_w_gate, ffn_w_up, ffn_w_down),
        "a_w_in32": a_w_in,
        "a_wgt": jnp.pad(a_w_in[:, :, 2 * qk_w + 2 * v_w:], ((0, 0), (0, 0), (0, LANES - ng))).astype(BF16),
        "a_bg": jnp.pad(a_b_gate, ((0, 0), (0, LANES - ng)))[:, None, :].astype(F32),
        "a_w_out": a_w_out.astype(BF16),
        "zero_bias": jnp.zeros((1, d_model), F32),
        "w_kv": w_kv.astype(BF16), "b_kv": row(b_kv),
        "b_w_q": b_w_q.astype(BF16), "b_b_q": row(b_b_q), "b_sinks": row(b_sinks),
        "b_w_o": b_w_o.astype(BF16), "b_b_o": row(b_b_o),
    }
    Bp, Sp, _ = x_prompt.shape
    Bs, Ts, _ = x_sample.shape
    zc = jnp.zeros((1, Bp) + state_mlstm_C.shape[2:], F32)
    zn = jnp.zeros((1, Bp) + state_mlstm_n.shape[2:], F32)
    zm = jnp.zeros((1, Bp) + state_mlstm_m.shape[2:] + (1,), F32)
    s_state = (state_mlstm_C, state_mlstm_n, state_mlstm_m[..., None])

    p_pos = jnp.arange(math.lcm(Sp, 2 * TOKEN_TILE), dtype=jnp.int32) % Sp
    s_pos = PAST_LEN + jnp.arange(math.lcm(Ts, 2 * TOKEN_TILE), dtype=jnp.int32) % Ts

    prompt = dict(x=x_prompt, pos=p_pos, state=(zc, zn, zm), cache=None, tag="p", dtype=BF16)
    sample = dict(x=x_sample, pos=s_pos, state=s_state, cache=(cache_swa_k, cache_swa_v), tag="s", dtype=F32)
    (y_p, p_c, p_n, p_m, p_k, p_v), (y_s, s_c, s_n, s_m, s_k, s_v) = _forward([prompt, sample], p)
    return (y_p, y_s, p_c, p_n, p_m, p_k, p_v, s_c, s_n, s_m, s_k, s_v)
```

```python
import functools
import math

import jax
import jax.numpy as jnp
from jax import lax
from jax.experimental import pallas as pl
from jax.experimental.pallas import tpu as pltpu

F32 = jnp.float32
BF16 = jnp.bfloat16

EPS = 1e-6
GATE_CAP = 15.0
A_HEADS = 4
MLSTM_CHUNK = 256
B_HEAD_DIM = 64
B_KV_HEADS = 4
WINDOW = 128
ROT_DIM = B_HEAD_DIM // 4
ROPE_THETA = 500000.0
PAST_LEN = 8192

LANES = 128
V7X_VMEM_LIMIT_BYTES = 60 * 1024 * 1024
TOKEN_TILE = 1024
FFN_SUBTILE = 512
FF_TILE = 256
CAST_STEPS = 16
SWA_BLOCKS_PER_STEP = 8


def _rms(x, g):
    return (x * lax.rsqrt(jnp.mean(x * x, axis=-1, keepdims=True) + EPS)) * g


def _dot(a, b):
    return jnp.dot(a, b, preferred_element_type=F32)


def _dot_nt(a, b):
    return lax.dot_general(a, b, (((1,), (1,)), ((), ())), preferred_element_type=F32)


def _dot_f32(a, b):
    return jnp.dot(a, b, preferred_element_type=F32, precision=lax.Precision.HIGHEST)


def _at(arr, *idx):
    return (arr, idx)


def _split(param):
    return param if isinstance(param, tuple) else (param, ())


def _shape(param):
    arr, idx = _split(param)
    return arr.shape[len(idx):]


def _resident(param):
    arr, idx = _split(param)
    block = (None,) * len(idx) + arr.shape[len(idx):]
    index = idx + (0,) * (arr.ndim - len(idx))
    return arr, pl.BlockSpec(block, lambda *_: index, pipeline_mode=pl.Buffered(1))


def _operands(tiled, residents):
    pairs = list(tiled) + [_resident(r) for r in residents]
    return [a for a, _ in pairs], [sp for _, sp in pairs]


def _rows(tm, width):
    return pl.BlockSpec((tm, width), lambda i: (i, 0))


def _params(*sem):
    return pltpu.CompilerParams(dimension_semantics=sem, vmem_limit_bytes=V7X_VMEM_LIMIT_BYTES)


def _token_tile(n):
    tm = TOKEN_TILE
    while n % tm:
        tm //= 2
    return tm


def _ffn_kernel(*refs, n_x, n_hg, n_cast, final, post, n_out, n_first, tf):
    refs = list(refs)
    x_refs = [refs.pop(0) for _ in range(n_x)]
    hg_refs = [refs.pop(0) for _ in range(n_hg)]
    cast_src = [refs.pop(0) for _ in range(n_cast)]
    if post:
        cos_refs, sin_refs = [refs.pop(0), refs.pop(0)], [refs.pop(0), refs.pop(0)]
    if n_hg:
        wmix_ref, bmix_ref = refs.pop(0), refs.pop(0)
    gn_ref, wg_ref, wu_ref, wd_ref = refs.pop(0), refs.pop(0), refs.pop(0), refs.pop(0)
    if final:
        fg_ref = refs.pop(0)
    if post:
        pgn_ref, pw_ref, pb_ref = refs.pop(0), refs.pop(0), refs.pop(0)
    o_refs = [refs.pop(0) for _ in range(n_out)]
    post_refs = [refs.pop(0) for _ in range({None: 0, "q": 1, "kv": 2}[post])]
    cast_dst = [refs.pop(0) for _ in range(n_cast)]
    (a_ref,) = refs
    first = pl.program_id(0) < n_first

    if n_cast:
        @pl.when(pl.program_id(0) < CAST_STEPS)
        def _():
            for src, dst in zip(cast_src, cast_dst):
                dst[...] = src[...].astype(dst.dtype)

    tm = a_ref.shape[0]
    sub = min(tm, FFN_SUBTILE)
    for r0 in range(0, tm, sub):
        rows = slice(r0, r0 + sub)

        def current(rs, dtype):
            vals = [r[rows, :].astype(dtype) for r in rs]
            return vals[0] if len(vals) == 1 else jnp.where(first, vals[0], vals[1])

        x = current(x_refs, F32)
        if n_hg:
            x = x + (_dot(current(hg_refs, BF16), wmix_ref[...]) + bmix_ref[...])
        h = _rms(x, gn_ref[...]).astype(BF16)
        for j in range(wg_ref.shape[1] // tf):
            sl = slice(j * tf, (j + 1) * tf)
            g = _dot(h, wg_ref[:, sl])
            u = _dot(h, wu_ref[:, sl])
            a_ref[rows, sl] = ((g * jax.nn.sigmoid(g)) * u).astype(BF16)
        out = x + 0.5 * _dot(a_ref[rows, :], wd_ref[...])
        if post:
            proj = _dot(_rms(out, pgn_ref[...]).astype(BF16), pw_ref[...]) + pb_ref[...]
            cos, sin = current(cos_refs, F32), current(sin_refs, F32)
            if post == "q":
                post_refs[0][rows, :] = (_rope(proj, cos, sin) * B_HEAD_DIM ** -0.5).astype(post_refs[0].dtype)
            else:
                kw = proj.shape[1] // 2
                post_refs[0][rows, :] = _rope(proj[:, :kw], cos, sin).astype(post_refs[0].dtype)
                post_refs[1][rows, :] = proj[:, kw:].astype(post_refs[1].dtype)
        if final:
            out = _rms(out, fg_ref[...])
        if n_out == 1:
            o_refs[0][rows, :] = out
        else:
            @pl.when(first)
            def _():
                o_refs[0][rows, :] = out

            @pl.when(jnp.logical_not(first))
            def _():
                o_refs[1][rows, :] = out


def _stream_rows(tm, width, n_first):
    return (pl.BlockSpec((tm, width), lambda i: (jnp.minimum(i, n_first - 1), 0)),
            pl.BlockSpec((tm, width), lambda i: (jnp.maximum(i - n_first, 0), 0)))


def _ffn(x, rows, gn, wg, wu, wd, mix=None, final_gain=None, split_out=False, cast=(), post=None, name="ffn"):
    n = sum(rows)
    d = _shape(gn)[1]
    dff = _shape(wg)[1]
    tf = FF_TILE if dff % FF_TILE == 0 else dff
    residents = ([mix[1], mix[2]] if mix is not None else []) + [gn, wg, wu, wd]
    if final_gain is not None:
        residents.append(final_gain)
    streamed = [x] + ([mix[0]] if mix is not None else [])
    post_widths = []
    if post is not None:
        kind, pgn, pw, pb, tables_first, tables_second, post_dtype = post
        residents += [pgn, pw, pb]
        streamed += [(tables_first[0], tables_second[0]), (tables_first[1], tables_second[1])]
        post_widths = [_shape(pw)[1]] if kind == "q" else [_shape(pw)[1] // 2] * 2
    flat_streamed = [a for s in streamed for a in (s if isinstance(s, tuple) else (s,))]
    cast_chunks = [(_shape(c)[0] // CAST_STEPS, _shape(c)[1]) for c in cast]

    nbytes = lambda shape, dtype: math.prod(shape) * jnp.dtype(dtype).itemsize
    fixed = (sum(nbytes(_shape(r), _split(r)[0].dtype) for r in residents)
             + 2 * sum(nbytes(c, F32) + nbytes(c, BF16) for c in cast_chunks))
    per_row = (2 * sum(nbytes(a.shape[1:], a.dtype) for a in flat_streamed)
               + 2 * (2 if split_out else 1) * d * 4 + dff * 2 + d * 4
               + 2 * sum(nbytes((w,), post_dtype) for w in post_widths))
    tm = TOKEN_TILE
    while math.gcd(*rows) % tm or fixed + tm * per_row > V7X_VMEM_LIMIT_BYTES:
        tm //= 2
    n_first = rows[0] // tm
    if cast and n // tm < CAST_STEPS:
        raise ValueError("too few grid steps for the weight rounding side job")

    def tiles(a):
        if isinstance(a, tuple):
            return list(zip(a, _stream_rows(tm, a[0].shape[1], n_first)))
        return [(a, _rows(tm, a.shape[1]))]

    tiled = tiles(x)
    n_x = len(tiled)
    n_hg = 0
    if mix is not None:
        tiled += tiles(mix[0])
        n_hg = len(tiled) - n_x
    chunk_of = lambda i: jnp.minimum(i, CAST_STEPS - 1)
    for c, chunk in zip(cast, cast_chunks):
        arr, idx = _split(c)
        tiled.append((arr, pl.BlockSpec((None,) * len(idx) + chunk, lambda i, idx=idx: idx + (chunk_of(i), 0))))
    if post is not None:
        blocks = [t[0].shape[0] // tm for t in (tables_first, tables_second)]
        table_specs = (pl.BlockSpec((tm, LANES), lambda i: (jnp.minimum(i, n_first - 1) % blocks[0], 0)),
                       pl.BlockSpec((tm, LANES), lambda i: (jnp.maximum(i - n_first, 0) % blocks[1], 0)))
        for which in (0, 1):
            tiled += list(zip((tables_first[which], tables_second[which]), table_specs))
    args, specs = _operands(tiled, residents)
    if split_out:
        out_specs = list(_stream_rows(tm, d, n_first))
        out_shape = [jax.ShapeDtypeStruct((r, d), F32) for r in rows]
    else:
        out_specs, out_shape = [_rows(tm, d)], [jax.ShapeDtypeStruct((n, d), F32)]
    n_out = len(out_specs)
    for w in post_widths:
        out_specs.append(_rows(tm, w))
        out_shape.append(jax.ShapeDtypeStruct((n, w), post_dtype))
    n_post = len(post_widths)
    out_specs += [pl.BlockSpec(chunk, lambda i: (chunk_of(i), 0)) for chunk in cast_chunks]
    out_shape += [jax.ShapeDtypeStruct(_shape(c), BF16) for c in cast]
    kern = functools.partial(_ffn_kernel, n_x=n_x, n_hg=n_hg, n_cast=len(cast), final=final_gain is not None,
                             post=post[0] if post is not None else None, n_out=n_out, n_first=n_first, tf=tf)
    outs = pl.pallas_call(
        kern,
        grid=(n // tm,),
        in_specs=specs,
        out_specs=out_specs,
        out_shape=out_shape,
        scratch_shapes=[pltpu.VMEM((tm, dff), BF16)],
        compiler_params=_params("arbitrary"),
        name=name,
    )(*args)
    y = tuple(outs[:n_out]) if split_out else outs[0]
    return y, list(outs[n_out:n_out + n_post]), tuple(outs[n_out + n_post:])


def _inproj_kernel(x_ref, gn_ref, w_ref, wgt_ref, bg_ref, q_ref, k_ref, v_ref, o_ref, g_ref, *gt_refs,
                   k_scale):
    h = _rms(x_ref[...], gn_ref[...]).astype(BF16)
    dqk, dv = q_ref.shape[1], v_ref.shape[1]
    qk = _dot(h, w_ref[:, :2 * dqk])
    q_ref[...] = qk[:, :dqk].astype(q_ref.dtype)
    k_ref[...] = (qk[:, dqk:] * k_scale).astype(k_ref.dtype)
    v_ref[...] = _dot(h, w_ref[:, 2 * dqk:2 * dqk + dv]).astype(v_ref.dtype)
    o_ref[...] = _dot(h, w_ref[:, 2 * dqk + dv:2 * dqk + 2 * dv])
    gates = _dot(h, wgt_ref[...]) + bg_ref[...]
    gates = GATE_CAP * jnp.tanh(gates / GATE_CAP)
    logsig = jnp.minimum(gates, 0.0) - jnp.log1p(jnp.exp(-jnp.abs(gates)))
    lane = lax.broadcasted_iota(jnp.int32, gates.shape, 1)
    gates = jnp.where(lane < A_HEADS, gates, logsig)
    g_ref[...] = gates
    for gt_ref in gt_refs:
        chunks, ng, L = gt_ref.shape
        for c in range(chunks):
            gt_ref[c] = gates[c * L:(c + 1) * L, :].T[:ng, :]


def _mlstm_chunk(S):
    return MLSTM_CHUNK if S % MLSTM_CHUNK == 0 else S


def _window_tile(row0, n):
    return _token_tile(math.gcd(row0, n))


def _window(xw):
    arr, row0, n = xw
    tm = _window_tile(row0, n)
    blk0 = row0 // tm
    return n, tm, (arr, pl.BlockSpec((tm, arr.shape[1]), lambda i: (i + blk0, 0)))


def _inproj(xw, seq_len, gn, w, wgt, bg, act_dtype, name):
    n, tm, x_op = _window(xw)
    dv = _shape(w)[0]
    dqk = dv // 2
    kern = functools.partial(_inproj_kernel, k_scale=(dqk // A_HEADS) ** -0.5)
    args, specs = _operands([x_op], [gn, w, wgt, bg])
    out_specs = [_rows(tm, dqk), _rows(tm, dqk), _rows(tm, dv), _rows(tm, dv), _rows(tm, LANES)]
    out_shape = [jax.ShapeDtypeStruct((n, dqk), act_dtype), jax.ShapeDtypeStruct((n, dqk), act_dtype),
                 jax.ShapeDtypeStruct((n, dv), act_dtype), jax.ShapeDtypeStruct((n, dv), F32),
                 jax.ShapeDtypeStruct((n, LANES), F32)]
    L = _mlstm_chunk(seq_len)
    with_gt = L % LANES == 0 and tm % L == 0
    if with_gt:
        out_specs.append(pl.BlockSpec((tm // L, 2 * A_HEADS, L), lambda i: (i, 0, 0)))
        out_shape.append(jax.ShapeDtypeStruct((n // L, 2 * A_HEADS, L), F32))
    outs = pl.pallas_call(
        kern,
        grid=(n // tm,),
        in_specs=specs,
        out_specs=out_specs,
        out_shape=out_shape,
        compiler_params=_params("parallel"),
        name=name,
    )(*args)
    return tuple(outs[:5]) + ((outs[5],) if with_gt else (None,))


def _rope(x, cos, sin):
    width = x.shape[1]
    reps = width // LANES
    cosw = jnp.concatenate([cos] * reps, axis=1)
    sinw = jnp.concatenate([sin] * reps, axis=1)
    half = ROT_DIM // 2
    lane = lax.broadcasted_iota(jnp.int32, (1, width), 1)
    first_half = (lane & (B_HEAD_DIM - 1)) < half
    partner = jnp.where(first_half, pltpu.roll(x, width - half, 1), pltpu.roll(x, half, 1))
    return x * cosw + partner * sinw


def _mlstm_kernel(q_ref, k_ref, v_ref, o_ref, gtm_ref, gt_ref, hn_ref, c0_ref, n0_ref, m0_ref, *rest,
                  bblk, chunk):
    out_ref, c_ref, n_ref, m_ref = rest[-4:]
    L = chunk
    dqk = q_ref.shape[2] // A_HEADS
    dv = v_ref.shape[2] // A_HEADS

    @pl.when(pl.program_id(1) == 0)
    def _():
        c_ref[...] = c0_ref[...]
        n_ref[...] = n0_ref[...]
        m_ref[...] = m0_ref[...]

    row = lax.broadcasted_iota(jnp.int32, (L, L), 0)
    col = lax.broadcasted_iota(jnp.int32, (L, L), 1)
    tril = col <= row
    lower_ones = tril.astype(F32)
    upper_ones = (row <= col).astype(F32)

    chains = [(b, h) for b in range(bblk) for h in range(A_HEADS)]
    per_chain = lambda f: jnp.stack([f(b, h) for b, h in chains])

    gts = [gt_ref[b, 0] for b in range(bblk)]
    gtms = [gtm_ref[b] for b in range(bblk)]
    csum_rows = [_dot_f32(gt, upper_ones) for gt in gts]
    csum_cols = [_dot_f32(lower_ones, gtm) for gtm in gtms]
    i_row = per_chain(lambda b, h: gts[b][h:h + 1, :])
    b_row = per_chain(lambda b, h: csum_rows[b][A_HEADS + h:A_HEADS + h + 1, :])
    i_col = per_chain(lambda b, h: gtms[b][:, h:h + 1])
    b_col = per_chain(lambda b, h: csum_cols[b][:, A_HEADS + h:A_HEADS + h + 1])
    b_last = b_col[:, L - 1:L, :]
    m_prev = per_chain(lambda b, h: m_ref[b, h:h + 1, :])
    n_prev = per_chain(lambda b, h: n_ref[b, h:h + 1, :])
    c_prev = per_chain(lambda b, h: c_ref[b, h])

    q = per_chain(lambda b, h: q_ref[b, :, h * dqk:(h + 1) * dqk]).astype(BF16)
    k = per_chain(lambda b, h: k_ref[b, :, h * dqk:(h + 1) * dqk]).astype(BF16)
    v = per_chain(lambda b, h: v_ref[b, :, h * dv:(h + 1) * dv]).astype(BF16)

    dmat = jnp.where(tril[None], (b_col - b_row) + i_row, -jnp.inf)
    g = b_col + m_prev
    mt = jnp.maximum(g, jnp.max(dmat, axis=-1, keepdims=True))
    w_inter = jnp.exp(g - mt)
    s = jnp.einsum("gtd,gsd->gts", q, k, preferred_element_type=F32) * jnp.exp(dmat - mt)
    G = len(chains)
    v_ones = jnp.concatenate([v, jnp.ones((G, L, LANES), BF16)], axis=-1)
    n_cols = jnp.broadcast_to(jnp.swapaxes(n_prev, 1, 2), (G, dqk, LANES))
    c_n = jnp.concatenate([c_prev, n_cols], axis=-1).astype(BF16)
    intra = jnp.einsum("gts,gsv->gtv", s.astype(BF16), v_ones, preferred_element_type=F32)
    inter = w_inter * jnp.einsum("gtd,gdv->gtv", q, c_n, preferred_element_type=F32)
    both = intra + inter
    den = jnp.maximum(jnp.abs(both[:, :, dv:]), jnp.exp(-mt))
    hc = both[:, :, :dv] / jnp.concatenate([den] * (dv // LANES), axis=-1)

    m_new = mt[:, L - 1:L, :]
    a_col = jnp.exp(((b_last - b_col) + i_col) - m_new)
    decay = jnp.exp((b_last + m_prev) - m_new)
    ka = k.astype(F32) * a_col
    kat = jnp.swapaxes(ka, 1, 2).astype(BF16)
    c_new = decay * c_prev + jnp.einsum("gds,gsv->gdv", kat, v, preferred_element_type=F32)
    n_new = decay * n_prev + jnp.sum(ka, axis=1, keepdims=True)

    gain = per_chain(lambda b, h: hn_ref[:, h * dv:(h + 1) * dv])
    gate = jax.nn.sigmoid(per_chain(lambda b, h: o_ref[b, :, h * dv:(h + 1) * dv]))
    res = (_rms(hc, gain) * gate).astype(out_ref.dtype)
    for gi, (b, h) in enumerate(chains):
        c_ref[b, h] = c_new[gi]
        n_ref[b, h:h + 1, :] = n_new[gi]
        m_ref[b, h:h + 1, :] = m_new[gi]
        out_ref[b, :, h * dv:(h + 1) * dv] = res[gi]


def _mlstm(q, k, v, o, gates, gt, head_norm, state0, layer0, layer, n_layers, prev_final, bblk, out_dtype,
           name):
    B, S, qw = q.shape
    vw = v.shape[2]
    L = _mlstm_chunk(S)
    nc = S // L
    ng = 2 * A_HEADS
    if gt is None:
        gt = jnp.swapaxes(gates[:, :, :ng].reshape(B, nc, L, ng), 2, 3)
    gt = gt.reshape(B, nc, ng, L)
    kern = functools.partial(_mlstm_kernel, bblk=bblk, chunk=L)
    seq = lambda w: pl.BlockSpec((bblk, L, w), lambda i, c: (i, c, 0))

    def state(a, row):
        return pl.BlockSpec((None, bblk) + a.shape[2:], lambda i, c: (row, i) + (0,) * (a.ndim - 2))

    hn_arr, hn_spec = _resident(head_norm)
    args = [q, k, v, o, gates, gt, hn_arr, *state0]
    specs = [seq(qw), seq(qw), seq(vw), seq(vw), seq(LANES),
             pl.BlockSpec((bblk, 1, ng, L), lambda i, c: (i, c, 0, 0)), hn_spec,
             *[state(a, layer0) for a in state0]]
    aliases = {}
    if prev_final is not None:
        aliases = {len(args) + t: 1 + t for t in range(len(prev_final))}
        args += list(prev_final)
        specs += [pl.BlockSpec(memory_space=pl.ANY)] * len(prev_final)
    return pl.pallas_call(
        kern,
        grid=(B // bblk, nc),
        in_specs=specs,
        out_specs=[seq(vw)] + [state(a, layer) for a in state0],
        out_shape=[jax.ShapeDtypeStruct((B, S, vw), out_dtype)]
                  + [jax.ShapeDtypeStruct((n_layers,) + a.shape[1:], F32) for a in state0],
        input_output_aliases=aliases,
        compiler_params=_params("parallel", "arbitrary"),
        name=name,
    )(*args)


def _pair_block_diag(x):
    zero = jnp.zeros_like(x)
    return jnp.concatenate([jnp.concatenate([x, zero], axis=1), jnp.concatenate([zero, x], axis=1)], axis=0)


def _swa_prompt_kernel(sinks_ref, q_ref, kp_ref, kc_ref, vp_ref, vc_ref, o_ref, *, group):
    W = WINDOW
    hd = B_HEAD_DIM
    pairs = group // 2
    R = pairs * W
    nblk = q_ref.shape[0] // W
    has_prev = pl.program_id(1) > 0
    t = lax.broadcasted_iota(jnp.int32, (R, 2 * W), 0) & (W - 1)
    s = lax.broadcasted_iota(jnp.int32, (R, 2 * W), 1) & (W - 1)
    use_cur = s <= t
    visible = jnp.logical_or(use_cur, has_prev)
    row = lax.broadcasted_iota(jnp.int32, (R, 1), 0)
    first_head = lax.broadcasted_iota(jnp.int32, (R, 2 * hd), 1) < hd
    ones = jnp.ones((W, hd), BF16)

    def key_blocks(ref_prev, ref_cur, blk, ksl):
        cur = ref_cur[blk * W:(blk + 1) * W, ksl]
        prev = ref_prev[:, ksl] if blk == 0 else ref_cur[(blk - 1) * W:blk * W, ksl]
        return prev.astype(BF16), cur.astype(BF16)

    def scores(item):
        blk, kh = item
        ksl = slice(kh * hd, (kh + 1) * hd)
        rows = slice(blk * W, (blk + 1) * W)
        q2 = jnp.concatenate(
            [q_ref[rows, (kh * pairs + pr) * 2 * hd:(kh * pairs + pr + 1) * 2 * hd] for pr in range(pairs)],
            axis=0).astype(BF16)
        k_prev, k_cur = key_blocks(kp_ref, kc_ref, blk, ksl)
        sc_cur = _dot_nt(q2, _pair_block_diag(k_cur))
        sc_prev = _dot_nt(q2, _pair_block_diag(k_prev))
        sc = jnp.where(use_cur, sc_cur, sc_prev)
        return jnp.where(visible, sc, -jnp.inf) if blk == 0 else sc

    def softmax(item, sc):
        _, kh = item
        probs, sink_terms = [], []
        for g in range(2):
            sink = jnp.full((R, 1), sinks_ref[0, kh * group + g], F32)
            for pr in range(1, pairs):
                sink = jnp.where(row >= pr * W, sinks_ref[0, kh * group + 2 * pr + g], sink)
            sg = sc[:, g * W:(g + 1) * W]
            mx = jnp.maximum(jnp.max(sg, axis=-1, keepdims=True), sink)
            probs.append(jnp.exp(sg - mx))
            sink_terms.append(jnp.exp(sink - mx))
        p = jnp.concatenate(probs, axis=1)
        p_cur = jnp.where(use_cur, p, 0.0).astype(BF16)
        p_prev = jnp.where(use_cur, 0.0, p).astype(BF16)
        return p_cur, p_prev, jnp.where(first_head, sink_terms[0], sink_terms[1])

    def weighted_values(item, p_cur, p_prev, sink_term):
        blk, kh = item
        v_prev, v_cur = key_blocks(vp_ref, vc_ref, blk, slice(kh * hd, (kh + 1) * hd))
        v_cur = jnp.concatenate([_pair_block_diag(v_cur), _pair_block_diag(ones)], axis=1)
        v_prev = jnp.concatenate([_pair_block_diag(v_prev), _pair_block_diag(ones)], axis=1)
        acc = _dot(p_cur, v_cur) + _dot(p_prev, v_prev)
        out = acc[:, :2 * hd] / (acc[:, 2 * hd:] + sink_term)
        for pr in range(pairs):
            lo = (kh * pairs + pr) * 2 * hd
            o_ref[0, blk * W:(blk + 1) * W, lo:lo + 2 * hd] = out[pr * W:(pr + 1) * W, :].astype(o_ref.dtype)

    items = [(blk, kh) for blk in range(nblk) for kh in range(B_KV_HEADS)]
    sc = {i: scores(items[i]) for i in range(min(2, len(items)))}
    probs = {0: softmax(items[0], sc.pop(0))}
    for i, item in enumerate(items):
        if i + 2 < len(items):
            sc[i + 2] = scores(items[i + 2])
        if i + 1 < len(items):
            probs[i + 1] = softmax(items[i + 1], sc.pop(i + 1))
        weighted_values(item, *probs.pop(i))


def _swa_prompt(q, k, v, row0, B, S, sinks, out_dtype, name):
    qw, kw = q.shape[1], k.shape[1]
    nb = S // WINDOW
    group = qw // kw
    if group % 2 or 2 * B_HEAD_DIM != LANES:
        raise ValueError("the prompt attention kernel pairs query heads into LANES-wide slices")
    nblk = SWA_BLOCKS_PER_STEP if nb % SWA_BLOCKS_PER_STEP == 0 and row0 % (SWA_BLOCKS_PER_STEP * WINDOW) == 0 else 1
    if row0 % (nblk * WINDOW):
        raise ValueError("the stream must start on a query block boundary")
    steps = nb // nblk
    first = row0 // (nblk * WINDOW)
    cur = lambda w: pl.BlockSpec((nblk * WINDOW, w), lambda b, j: (first + b * steps + j, 0))
    prev = lambda w: pl.BlockSpec((WINDOW, w), lambda b, j: (jnp.maximum((first + b * steps + j) * nblk - 1, 0), 0))
    kern = functools.partial(_swa_prompt_kernel, group=group)
    return pl.pallas_call(
        kern,
        grid=(B, steps),
        in_specs=[pl.BlockSpec(memory_space=pltpu.SMEM), cur(qw), prev(kw), cur(kw), prev(kw), cur(kw)],
        out_specs=pl.BlockSpec((1, nblk * WINDOW, qw), lambda b, j: (b, j, 0)),
        out_shape=jax.ShapeDtypeStruct((B, S, qw), out_dtype),
        compiler_params=_params("parallel", "parallel"),
        name=name,
    )(sinks, q, k, k, v, v)


def _swa_sample_kernel(sinks_ref, q_ref, kc_ref, kn_ref, vc_ref, vn_ref, o_ref, *window_refs, group, bblk):
    T = q_ref.shape[0] // bblk
    wb, kw = kc_ref.shape[1:]
    for buf_ref, new_ref, out_ref in zip((kc_ref, vc_ref), (kn_ref, vn_ref), window_refs):
        out_ref[:, :wb - T, :] = buf_ref[:, T:, :]
        out_ref[:, wb - T:, :] = new_ref[...].reshape(bblk, T, kw)
    hd = B_HEAD_DIM
    nq = B_KV_HEADS * group
    R = nq * T
    per_seq = lambda ref: ref[...].astype(F32).reshape(bblk, T, ref.shape[1])
    k_new, v_new = per_seq(kn_ref).astype(BF16), per_seq(vn_ref).astype(BF16)
    qb = per_seq(q_ref)
    blocks = []
    for kh in range(B_KV_HEADS):
        for g in range(group):
            lo = (kh * group + g) * hd
            parts = [qb[:, :, lo:lo + hd]]
            if kh:
                parts.insert(0, jnp.zeros((bblk, T, kh * hd), qb.dtype))
            if kh < B_KV_HEADS - 1:
                parts.append(jnp.zeros((bblk, T, (B_KV_HEADS - 1 - kh) * hd), qb.dtype))
            blocks.append(jnp.concatenate(parts, axis=-1))
    q_all = jnp.concatenate(blocks, axis=1).astype(BF16)

    t = lax.broadcasted_iota(jnp.int32, (R, wb), 0) % T
    i = lax.broadcasted_iota(jnp.int32, (R, wb), 1)
    mask_buf = jnp.logical_and((t + wb) - i < WINDOW, (PAST_LEN - wb) + i >= 0)
    tn = lax.broadcasted_iota(jnp.int32, (R, T), 0) % T
    u = lax.broadcasted_iota(jnp.int32, (R, T), 1)
    mask_new = u <= tn
    ridx = lax.broadcasted_iota(jnp.int32, (R, 1), 0)
    sink = jnp.full((R, 1), sinks_ref[0, 0], F32)
    for hq in range(1, nq):
        sink = jnp.where(ridx >= hq * T, sinks_ref[0, hq], sink)

    sc_buf = jnp.einsum("bqd,bkd->bqk", q_all, kc_ref[...].astype(BF16), preferred_element_type=F32)
    sc_new = jnp.einsum("bqd,bkd->bqk", q_all, k_new, preferred_element_type=F32)
    sc_buf = jnp.where(mask_buf[None], sc_buf, -jnp.inf)
    sc_new = jnp.where(mask_new[None], sc_new, -jnp.inf)
    mx = jnp.maximum(jnp.maximum(jnp.max(sc_buf, axis=-1, keepdims=True),
                                 jnp.max(sc_new, axis=-1, keepdims=True)), sink[None])
    p_buf = jnp.exp(sc_buf - mx)
    p_new = jnp.exp(sc_new - mx)
    denom = (jnp.sum(p_buf, axis=-1, keepdims=True) + jnp.sum(p_new, axis=-1, keepdims=True)
             + jnp.exp(sink[None] - mx))
    acc = (jnp.einsum("bqk,bkd->bqd", p_buf.astype(BF16), vc_ref[...].astype(BF16), preferred_element_type=F32)
           + jnp.einsum("bqk,bkd->bqd", p_new.astype(BF16), v_new, preferred_element_type=F32))
    out = (acc / denom).astype(o_ref.dtype)
    for kh in range(B_KV_HEADS):
        for g in range(group):
            hq = kh * group + g
            o_ref[:, :, hq * hd:(hq + 1) * hd] = out[:, hq * T:(hq + 1) * T, kh * hd:(kh + 1) * hd]


def _swa_sample(q, k_buf, k_new, v_buf, v_new, row0, T, sinks, bblk, with_window, name):
    qw = q.shape[1]
    B, wb, kw = k_buf.shape
    group = qw // kw
    if row0 % (bblk * T):
        raise ValueError("the stream must start on a sequence block boundary")
    if with_window and not 0 < T < wb:
        raise ValueError("the window buffer must be longer than the new tokens")
    first = row0 // (bblk * T)
    blk = lambda r, w: pl.BlockSpec((bblk, r, w), lambda i: (i, 0, 0))
    new = lambda w: pl.BlockSpec((bblk * T, w), lambda i: (first + i, 0))
    kern = functools.partial(_swa_sample_kernel, group=group, bblk=bblk)
    return pl.pallas_call(
        kern,
        grid=(B // bblk,),
        in_specs=[pl.BlockSpec(memory_space=pltpu.SMEM), new(qw), blk(wb, kw), new(kw), blk(wb, kw), new(kw)],
        out_specs=[blk(T, qw)] + with_window * [blk(wb, kw), blk(wb, kw)],
        out_shape=[jax.ShapeDtypeStruct((B, T, qw), BF16)]
                  + with_window * [jax.ShapeDtypeStruct((B, wb, kw), F32)] * 2,
        compiler_params=_params("parallel"),
        name=name,
    )(sinks, q, k_buf, k_new, v_buf, v_new)


def _rope_tables(pos):
    half = ROT_DIM // 2
    inv_freq = jnp.exp(-math.log(ROPE_THETA) * jnp.arange(0, ROT_DIM, 2, dtype=F32) / ROT_DIM)
    ang = pos.astype(F32)[:, None] * inv_freq[None, :]
    d = jnp.arange(LANES, dtype=jnp.int32) % B_HEAD_DIM
    ang_l = jnp.take(ang, d % half, axis=1)
    rot = (d < ROT_DIM)[None, :]
    sign = jnp.where(d < half, -1.0, 1.0)[None, :]
    cos = jnp.where(rot, jnp.cos(ang_l), 1.0)
    sin = jnp.where(rot, jnp.sin(ang_l) * sign, 0.0)
    return cos.astype(F32), sin.astype(F32)


def _largest_divisor(n, cap):
    d = min(n, cap)
    while n % d:
        d -= 1
    return d


def _forward(streams, p):
    n_a = p["a_w_in32"].shape[0]
    depth = p["ffn_norm"].shape[0]
    rows = tuple(st["x"].shape[0] * st["x"].shape[1] for st in streams)
    x = tuple(st["x"].reshape(n, -1) for st, n in zip(streams, rows))

    def window(i):
        if isinstance(x, tuple):
            return (x[i], 0, rows[i])
        return (x, sum(rows[:i]), rows[i])

    ffn_order = [(l, half) for l in range(depth) for half in (0, 1)]
    ffn_w = tuple(w[0, 0].astype(BF16) for w in p["ffn_w32"])

    def ffn_after(l, half):
        k = ffn_order.index((l, half)) + 1
        return tuple(_at(w, *ffn_order[k]) for w in p["ffn_w32"]) if k < len(ffn_order) else ()

    final_state = [None for _ in streams]
    kv_new = None
    windows = [None for _ in streams]
    if not 0 < n_a < depth:
        raise ValueError("expected mLSTM layers followed by attention layers")
    tables = [_rope_tables(st["pos"]) for st in streams]
    for l in range(depth):
        post = None
        if l >= n_a:
            post = ("q", _at(p["mix_norm"], l), _at(p["b_w_q"], l - n_a), _at(p["b_b_q"], l - n_a),
                    *tables, BF16)
        mixer_w = (_at(p["a_w_in32"], l),) if l < n_a else ()
        x, posts, casts = _ffn(x, rows, _at(p["ffn_norm"], l, 0), *ffn_w, cast=ffn_after(l, 0) + mixer_w,
                               post=post, name=f"ffn_{l}a")
        ffn_w, mixer_w = casts[:len(casts) - len(mixer_w)], casts[len(casts) - len(mixer_w):]
        mixed = []
        for i, st in enumerate(streams):
            B, T, _ = st["x"].shape
            tag, act = st["tag"], st["dtype"]
            seq = lambda a: a.reshape(B, T, a.shape[-1])
            if l < n_a:
                q, k, v, o, gates, gt = _inproj(window(i), T, _at(p["mix_norm"], l), mixer_w[0],
                                                _at(p["a_wgt"], l), _at(p["a_bg"], l), act,
                                                name=f"inproj_{tag}_{l}")
                bblk = _largest_divisor(B, 2 if T >= MLSTM_CHUNK else 8)
                layer0 = l if st["state"][0].shape[0] == n_a else 0
                hg, *final_state[i] = _mlstm(seq(q), seq(k), seq(v), seq(o), seq(gates), gt,
                                             _at(p["a_head_norm"], l), st["state"], layer0, l, n_a,
                                             final_state[i], bblk, BF16, name=f"mlstm_{tag}_{l}")
                mixed.append(hg.reshape(B * T, -1))
            else:
                j = l - n_a
                row0 = sum(rows[:i])
                if st["cache"] is None:
                    o = _swa_prompt(posts[0], *kv_new, row0, B, T, p["b_sinks"][j], BF16, name=f"swa_{tag}_{l}")
                else:
                    kc, vc = st["cache"]
                    o, *window = _swa_sample(posts[0], kc.reshape(B, kc.shape[1], -1), kv_new[0],
                                             vc.reshape(B, vc.shape[1], -1), kv_new[1], row0, T,
                                             p["b_sinks"][j], _largest_divisor(B, 8), j == 0,
                                             name=f"swa_{tag}_{l}")
                    if window:
                        windows[i] = window
                mixed.append(o.reshape(B * T, -1))
        if l < n_a:
            mix = (tuple(mixed), _at(p["a_w_out"], l), p["zero_bias"])
        else:
            mix = (tuple(mixed), _at(p["b_w_o"], l - n_a), _at(p["b_b_o"], l - n_a))
        last = l == depth - 1
        post = ("kv", p["kv_norm"], p["w_kv"], p["b_kv"], *tables, F32) if l == n_a - 1 else None
        x, posts, ffn_w = _ffn(x, rows, _at(p["ffn_norm"], l, 1), *ffn_w, mix=mix, cast=ffn_after(l, 1),
                               post=post, final_gain=p["final_norm"] if last else None, split_out=last,
                               name=f"ffn_{l}b")
        if post is not None:
            kv_new = posts

    results = []
    for i, st in enumerate(streams):
        B, T, D = st["x"].shape
        new_c, new_n, new_m = final_state[i]
        if st["cache"] is None:
            row0 = sum(rows[:i])
            k_new, v_new = (a[row0:row0 + rows[i]].reshape(B, T, -1) for a in kv_new)
            wb = min(WINDOW, T)
            buf_k, buf_v = k_new[:, -wb:], v_new[:, -wb:]
        else:
            wb = st["cache"][0].shape[1]
            buf_k, buf_v = windows[i]
        heads = lambda a: a.reshape(B, wb, B_KV_HEADS, B_HEAD_DIM)
        results.append((x[i].reshape(B, T, D), new_c, new_n, new_m[..., 0], heads(buf_k), heads(buf_v)))
    return results


def kernel(x_prompt, x_sample, state_mlstm_C, state_mlstm_n, state_mlstm_m, cache_swa_k, cache_swa_v,
           ffn_norm, ffn_w_gate, ffn_w_up, ffn_w_down, mix_norm, a_w_in, a_b_gate, a_head_norm, a_w_out,
           kv_norm, w_kv, b_kv, b_w_q, b_b_q, b_sinks, b_w_o, b_b_o, final_norm):
    n_a, d_model, a_proj = a_w_in.shape
    ng = 2 * A_HEADS
    qk_w = d_model // 2
    v_w = d_model
    if a_proj != 2 * qk_w + 2 * v_w + ng:
        raise ValueError("unexpected mLSTM projection width")
    row = lambda a: a[..., None, :].astype(F32)
    p = {
        "ffn_norm": row(ffn_norm), "mix_norm": row(mix_norm), "kv_norm": row(kv_norm),
        "final_norm": row(final_norm), "a_head_norm": row(a_head_norm),
        "ffn_w32": (ffn_w_gate, ffn_w_up, ffn_w_down),
        "a_w_in32": a_w_in,
        "a_wgt": jnp.pad(a_w_in[:, :, 2 * qk_w + 2 * v_w:], ((0, 0), (0, 0), (0, LANES - ng))).astype(BF16),
        "a_bg": jnp.pad(a_b_gate, ((0, 0), (0, LANES - ng)))[:, None, :].astype(F32),
        "a_w_out": a_w_out.astype(BF16),
        "zero_bias": jnp.zeros((1, d_model), F32),
        "w_kv": w_kv.astype(BF16), "b_kv": row(b_kv),
        "b_w_q": b_w_q.astype(BF16), "b_b_q": row(b_b_q), "b_sinks": row(b_sinks),
        "b_w_o": b_w_o.astype(BF16), "b_b_o": row(b_b_o),
    }
    Bp, Sp, _ = x_prompt.shape
    Bs, Ts, _ = x_sample.shape
    zc = jnp.zeros((1, Bp) + state_mlstm_C.shape[2:], F32)
    zn = jnp.zeros((1, Bp) + state_mlstm_n.shape[2:], F32)
    zm = jnp.zeros((1, Bp) + state_mlstm_m.shape[2:] + (1,), F32)
    s_state = (state_mlstm_C, state_mlstm_n, state_mlstm_m[..., None])

    p_pos = jnp.arange(math.lcm(Sp, TOKEN_TILE), dtype=jnp.int32) % Sp
    s_pos = PAST_LEN + jnp.arange(math.lcm(Ts, TOKEN_TILE), dtype=jnp.int32) % Ts

    prompt = dict(x=x_prompt, pos=p_pos, state=(zc, zn, zm), cache=None, tag="p", dtype=BF16)
    sample = dict(x=x_sample, pos=s_pos, state=s_state, cache=(cache_swa_k, cache_swa_v), tag="s", dtype=F32)
    (y_p, p_c, p_n, p_m, p_k, p_v), (y_s, s_c, s_n, s_m, s_k, s_v) = _forward([prompt, sample], p)
    return (y_p, y_s, p_c, p_n, p_m, p_k, p_v, s_c, s_n, s_m, s_k, s_v)
```
